```python
import math
import jax, jax.numpy as jnp
from jax import lax
import numpy as np

D_MODEL = 1024
BATCH = 8
SEQ = 4096
DEPTH = 1

D_SSM = D_MODEL // 2
SSM_GROUP = 16
N_SSM_GROUPS = D_SSM // SSM_GROUP
SSM_STATE = 64
SSM_DT_MIN = 1e-3
SSM_DT_MAX = 1e-1
D_ATTN = D_MODEL // 2
DIFF_HEAD_DIM = 64
N_DIFF_HEADS = D_ATTN // (2 * DIFF_HEAD_DIM)
ROPE_THETA = 10000.0
Q_BLOCK = 128
N_BRANCHES = 2
D_IN = D_SSM + 3 * D_ATTN + N_BRANCHES * D_MODEL
D_FF = -(-8 * D_MODEL // (3 * 256)) * 256
DEEPNORM_ALPHA = (2.0 * DEPTH) ** 0.25
DEEPNORM_BETA = (8.0 * DEPTH) ** -0.25
LN_EPS = 1e-5
RMS_EPS = 1e-5

kernel_name = "hybrid_s5_diffattn_gated_deepnorm"


def lambda_init_for(layer_idx):
    return 0.8 - 0.6 * math.exp(-0.3 * layer_idx)


def layer_norm(x, g, b):
    xf = x.astype(jnp.float32)
    mu = jnp.mean(xf, axis=-1, keepdims=True)
    xc = xf - mu
    var = jnp.mean(xc * xc, axis=-1, keepdims=True)
    y = xc * lax.rsqrt(var + LN_EPS) * g.astype(jnp.float32) + b.astype(jnp.float32)
    return y.astype(x.dtype)


def rope_tables(seq_len, dim):
    inv_freq = ROPE_THETA ** (-jnp.arange(0, dim, 2, dtype=jnp.float32) / dim)
    ang = jnp.arange(seq_len, dtype=jnp.float32)[:, None] * inv_freq[None, :]
    return jnp.cos(ang), jnp.sin(ang)


def apply_rope(t, cos, sin):
    half = t.shape[-1] // 2
    t1, t2 = t[..., :half], t[..., half:]
    c = cos[None, :, None, None, :]
    s = sin[None, :, None, None, :]
    return jnp.concatenate([t1 * c - t2 * s, t2 * c + t1 * s], axis=-1)


def s5_branch(u, lam_re, lam_im, log_step, b_re, b_im, c_re, c_im, d_skip, w_glu):
    bsz, seq = u.shape[0], u.shape[1]
    f32 = jnp.float32
    uf = u.astype(f32).reshape(bsz, seq, N_SSM_GROUPS, SSM_GROUP)
    lr, li = lam_re.astype(f32), lam_im.astype(f32)
    dt = jnp.exp(log_step.astype(f32))[:, None]
    mag = jnp.exp(lr * dt)
    abar_re, abar_im = mag * jnp.cos(li * dt), mag * jnp.sin(li * dt)
    nr, ni = abar_re - 1.0, abar_im
    den = lr * lr + li * li
    fr = (nr * lr + ni * li) / den
    fi = (ni * lr - nr * li) / den
    br, bi = b_re.astype(f32), b_im.astype(f32)
    bb_re = fr[..., None] * br - fi[..., None] * bi
    bb_im = fr[..., None] * bi + fi[..., None] * br
    bu_re = jnp.einsum('bsgh,gph->bsgp', uf, bb_re)
    bu_im = jnp.einsum('bsgh,gph->bsgp', uf, bb_im)
    a_re = jnp.broadcast_to(abar_re, bu_re.shape)
    a_im = jnp.broadcast_to(abar_im, bu_im.shape)

    def combine(e1, e2):
        a1r, a1i, b1r, b1i = e1
        a2r, a2i, b2r, b2i = e2
        return (a2r * a1r - a2i * a1i,
                a2r * a1i + a2i * a1r,
                a2r * b1r - a2i * b1i + b2r,
                a2r * b1i + a2i * b1r + b2i)

    _, _, s_re, s_im = lax.associative_scan(combine, (a_re, a_im, bu_re, bu_im), axis=1)
    y = (jnp.einsum('bsgp,ghp->bsgh', s_re, c_re.astype(f32))
         - jnp.einsum('bsgp,ghp->bsgh', s_im, c_im.astype(f32))
         + d_skip.astype(f32) * uf)
    y = jax.nn.gelu(y.reshape(bsz, seq, D_SSM))
    ga = jnp.einsum('bse,ef->bsf', y, w_glu.astype(f32))
    a, g = ga[..., :D_SSM], ga[..., D_SSM:]
    return (a * jax.nn.sigmoid(g)).astype(u.dtype)


def diff_attention(q, k, v, lq1, lk1, lq2, lk2, subln_gain, lambda_init):
    bsz, seq = q.shape[0], q.shape[1]
    f32 = jnp.float32
    qf = q.astype(f32).reshape(bsz, seq, N_DIFF_HEADS, 2, DIFF_HEAD_DIM)
    kf = k.astype(f32).reshape(bsz, seq, N_DIFF_HEADS, 2, DIFF_HEAD_DIM)
    vf = v.astype(f32).reshape(bsz, seq, N_DIFF_HEADS, 2 * DIFF_HEAD_DIM)
    cos, sin = rope_tables(seq, DIFF_HEAD_DIM)
    qf = apply_rope(qf, cos, sin) * (DIFF_HEAD_DIM ** -0.5)
    kf = apply_rope(kf, cos, sin)
    lam = (jnp.exp(jnp.sum(lq1.astype(f32) * lk1.astype(f32)))
           - jnp.exp(jnp.sum(lq2.astype(f32) * lk2.astype(f32))) + lambda_init)
    outs = []
    for i in range(seq // Q_BLOCK):
        q0 = i * Q_BLOCK
        kl = q0 + Q_BLOCK
        qb = qf[:, q0:kl]
        kb = kf[:, :kl]
        vb = vf[:, :kl]
        s = jnp.einsum('bqhcd,bkhcd->bhcqk', qb, kb)
        mask = jnp.arange(kl)[None, :] <= (q0 + jnp.arange(Q_BLOCK))[:, None]
        s = jnp.where(mask, s, -jnp.inf)
        p = jax.nn.softmax(s, axis=-1)
        w = p[:, :, 0] - lam * p[:, :, 1]
        outs.append(jnp.einsum('bhqk,bkhe->bqhe', w, vb))
    o = jnp.concatenate(outs, axis=1)
    ms = jnp.mean(o * o, axis=-1, keepdims=True)
    o = o * lax.rsqrt(ms + RMS_EPS) * subln_gain.astype(f32) * (1.0 - lambda_init)
    return o.reshape(bsz, seq, D_ATTN).astype(q.dtype)


def setup_inputs(seed: int = 0) -> dict:
    key = jax.random.key(seed)
    ks = jax.random.split(key, 32)
    L, D, f32 = DEPTH, D_MODEL, jnp.float32
    G, P, H = N_SSM_GROUPS, SSM_STATE, SSM_GROUP
    nrm = lambda k, shape: jax.random.normal(k, shape, f32)
    x = nrm(ks[0], (BATCH, SEQ, D))
    w_in = nrm(ks[1], (L, D, D_IN)) * D ** -0.5
    b_gate = nrm(ks[2], (L, N_BRANCHES * D)) * 0.01
    ssm_lambda_re = -0.5 + 0.005 * nrm(ks[3], (L, G, P))
    ssm_lambda_im = (math.pi * jnp.arange(P, dtype=f32))[None, None, :] + 0.01 * nrm(ks[4], (L, G, P))
    ssm_log_step = jax.random.uniform(ks[5], (L, G), f32, math.log(SSM_DT_MIN), math.log(SSM_DT_MAX))
    ssm_b_re = nrm(ks[6], (L, G, P, H)) * (2 * H) ** -0.5
    ssm_b_im = nrm(ks[7], (L, G, P, H)) * (2 * H) ** -0.5
    ssm_c_re = nrm(ks[8], (L, G, H, P)) * (2 * P) ** -0.5
    ssm_c_im = nrm(ks[9], (L, G, H, P)) * (2 * P) ** -0.5
    ssm_d = nrm(ks[10], (L, G, H))
    w_glu = nrm(ks[11], (L, D_SSM, 2 * D_SSM)) * D_SSM ** -0.5
    lambda_q1 = nrm(ks[12], (L, DIFF_HEAD_DIM)) * 0.1
    lambda_k1 = nrm(ks[13], (L, DIFF_HEAD_DIM)) * 0.1
    lambda_q2 = nrm(ks[14], (L, DIFF_HEAD_DIM)) * 0.1
    lambda_k2 = nrm(ks[15], (L, DIFF_HEAD_DIM)) * 0.1
    subln_gain = 1.0 + 0.01 * nrm(ks[16], (L, 2 * DIFF_HEAD_DIM))
    w_proj_ssm = nrm(ks[17], (L, D_SSM, D)) * D_SSM ** -0.5
    w_proj_attn = nrm(ks[18], (L, D_ATTN, D)) * D_ATTN ** -0.5
    w_out = nrm(ks[19], (L, D, D)) * D ** -0.5 * DEEPNORM_BETA
    ln1_g = 1.0 + 0.01 * nrm(ks[20], (L, D))
    ln1_b = 0.01 * nrm(ks[21], (L, D))
    w_ffn_gate = nrm(ks[22], (L, D, D_FF)) * D ** -0.5
    w_ffn_up = nrm(ks[23], (L, D, D_FF)) * D ** -0.5
    w_ffn_down = nrm(ks[24], (L, D_FF, D)) * D_FF ** -0.5 * DEEPNORM_BETA
    ln2_g = 1.0 + 0.01 * nrm(ks[25], (L, D))
    ln2_b = 0.01 * nrm(ks[26], (L, D))
    return {"x": x, "w_in": w_in, "b_gate": b_gate,
            "ssm_lambda_re": ssm_lambda_re, "ssm_lambda_im": ssm_lambda_im,
            "ssm_log_step": ssm_log_step, "ssm_b_re": ssm_b_re, "ssm_b_im": ssm_b_im,
            "ssm_c_re": ssm_c_re, "ssm_c_im": ssm_c_im, "ssm_d": ssm_d, "w_glu": w_glu,
            "lambda_q1": lambda_q1, "lambda_k1": lambda_k1, "lambda_q2": lambda_q2,
            "lambda_k2": lambda_k2, "subln_gain": subln_gain,
            "w_proj_ssm": w_proj_ssm, "w_proj_attn": w_proj_attn, "w_out": w_out,
            "ln1_g": ln1_g, "ln1_b": ln1_b,
            "w_ffn_gate": w_ffn_gate, "w_ffn_up": w_ffn_up, "w_ffn_down": w_ffn_down,
            "ln2_g": ln2_g, "ln2_b": ln2_b}


def reference(x, w_in, b_gate, ssm_lambda_re, ssm_lambda_im, ssm_log_step, ssm_b_re, ssm_b_im,
              ssm_c_re, ssm_c_im, ssm_d, w_glu, lambda_q1, lambda_k1, lambda_q2, lambda_k2,
              subln_gain, w_proj_ssm, w_proj_attn, w_out, ln1_g, ln1_b,
              w_ffn_gate, w_ffn_up, w_ffn_down, ln2_g, ln2_b):
    bsz, seq = x.shape[0], x.shape[1]
    o_q = D_SSM
    o_k = o_q + D_ATTN
    o_v = o_k + D_ATTN
    o_g = o_v + D_ATTN
    for l in range(DEPTH):
        lambda_init = lambda_init_for(l)
        h = jnp.einsum('bsd,de->bse', x, w_in[l])
        u = h[..., :o_q]
        q = h[..., o_q:o_k]
        k = h[..., o_k:o_v]
        v = h[..., o_v:o_g]
        gates = jax.nn.sigmoid((h[..., o_g:] + b_gate[l]).astype(jnp.float32))
        gates = gates.reshape(bsz, seq, N_BRANCHES, D_MODEL)
        y_ssm = s5_branch(u, ssm_lambda_re[l], ssm_lambda_im[l], ssm_log_step[l],
                          ssm_b_re[l], ssm_b_im[l], ssm_c_re[l], ssm_c_im[l], ssm_d[l], w_glu[l])
        y_att = diff_attention(q, k, v, lambda_q1[l], lambda_k1[l], lambda_q2[l], lambda_k2[l],
                               subln_gain[l], lambda_init)
        p_ssm = jnp.einsum('bse,ed->bsd', y_ssm, w_proj_ssm[l]).astype(jnp.float32)
        p_att = jnp.einsum('bse,ed->bsd', y_att, w_proj_attn[l]).astype(jnp.float32)
        merged = (gates[:, :, 0] * p_ssm + gates[:, :, 1] * p_att).astype(x.dtype)
        mix = jnp.einsum('bsd,de->bse', merged, w_out[l])
        x = layer_norm(DEEPNORM_ALPHA * x + mix, ln1_g[l], ln1_b[l])
        a = jnp.einsum('bsd,df->bsf', x, w_ffn_gate[l])
        b = jnp.einsum('bsd,df->bsf', x, w_ffn_up[l])
        ff = jnp.einsum('bsf,fd->bsd', jax.nn.silu(a) * b, w_ffn_down[l])
        x = layer_norm(DEEPNORM_ALPHA * x + ff, ln2_g[l], ln2_b[l])
    return x
```

```python
import functools
import math

import jax
import jax.numpy as jnp
from jax import lax
from jax.experimental import pallas as pl
from jax.experimental.pallas import tpu as pltpu

F32 = jnp.float32
BF16 = jnp.bfloat16

D_MODEL = 1024
D_SSM = 512
SSM_GROUP = 16
N_SSM_GROUPS = 32
SSM_STATE = 64
N_STATES = N_SSM_GROUPS * SSM_STATE
D_ATTN = 512
HEAD_DIM = 64
HEAD_WIDTH = 2 * HEAD_DIM
N_HEADS = 4
ROPE_THETA = 10000.0
D_FF = 2816
DEPTH = 1
DEEPNORM_ALPHA = (2.0 * DEPTH) ** 0.25
LN_EPS = 1e-5
RMS_EPS = 1e-5
LAMBDA_INIT = 0.8 - 0.6 * math.exp(-0.3 * 0)

LANES = 128
SUBLANES = 8
VMEM_LIMIT_BYTES = 56 * 1024 * 1024

IN_TS = 512
SSM_L = 64
SSM_SLAB = 4
ATT_TQ = 256
MERGE_TM = 512
FFN_TM = 512
FFN_CHUNK = 1408


def _dot(a, b):
    return jnp.dot(a, b, preferred_element_type=F32)


def _layer_norm(z, g, b):
    mu = jnp.mean(z, axis=-1, keepdims=True)
    zc = z - mu
    var = jnp.mean(zc * zc, axis=-1, keepdims=True)
    return zc * lax.rsqrt(var + LN_EPS) * g + b


def _in_proj_kernel(x_ref, w_ref, bg_ref, cos_ref, sin_ref,
                    u_ref, q_ref, k_ref, v_ref, g_ref):
    xb = x_ref[...].astype(BF16)
    u_ref[...] = _dot(xb, w_ref[:, 0:D_SSM])
    cos = cos_ref[...]
    sin = sin_ref[...]
    lane = lax.broadcasted_iota(jnp.int32, cos.shape, 1)
    low_half = (lane & (HEAD_DIM // 2)) == 0
    for out_ref, off, scale in ((q_ref, D_SSM, HEAD_DIM ** -0.5),
                                (k_ref, D_SSM + D_ATTN, 1.0)):
        h = _dot(xb, w_ref[:, off:off + D_ATTN])
        for hd in range(N_HEADS):
            slab = h[:, hd * LANES:(hd + 1) * LANES]
            partner = jnp.where(low_half,
                                pltpu.roll(slab, LANES - HEAD_DIM // 2, 1),
                                pltpu.roll(slab, HEAD_DIM // 2, 1))
            r = slab * cos + partner * sin
            if scale != 1.0:
                r = r * scale
            out_ref[:, hd * LANES:(hd + 1) * LANES] = r.astype(BF16)
    off_v = D_SSM + 2 * D_ATTN
    v_ref[...] = _dot(xb, w_ref[:, off_v:off_v + D_ATTN]).astype(BF16)
    off_g = off_v + D_ATTN
    g = _dot(xb, w_ref[:, off_g:off_g + 2 * D_MODEL]) + bg_ref[...]
    g_ref[...] = jax.nn.sigmoid(g).astype(BF16)


def _in_proj(x, w_in_b, b_gate, cos_t, sin_t):
    bsz, seq, _ = x.shape
    ts = IN_TS
    grid = (bsz, seq // ts)
    row = lambda b, i: (b, i, 0)
    const2 = lambda b, i: (0, 0)
    out_shape = (
        jax.ShapeDtypeStruct((bsz, seq, D_SSM), F32),
        jax.ShapeDtypeStruct((bsz, seq, D_ATTN), BF16),
        jax.ShapeDtypeStruct((bsz, seq, D_ATTN), BF16),
        jax.ShapeDtypeStruct((bsz, seq, D_ATTN), BF16),
        jax.ShapeDtypeStruct((bsz, seq, 2 * D_MODEL), BF16),
    )
    return pl.pallas_call(
        _in_proj_kernel,
        out_shape=out_shape,
        grid=grid,
        in_specs=[
            pl.BlockSpec((None, ts, D_MODEL), row),
            pl.BlockSpec(w_in_b.shape, const2),
            pl.BlockSpec(b_gate.shape, const2),
            pl.BlockSpec((ts, LANES), lambda b, i: (i, 0)),
            pl.BlockSpec((ts, LANES), lambda b, i: (i, 0)),
        ],
        out_specs=(
            pl.BlockSpec((None, ts, D_SSM), row),
            pl.BlockSpec((None, ts, D_ATTN), row),
            pl.BlockSpec((None, ts, D_ATTN), row),
            pl.BlockSpec((None, ts, D_ATTN), row),
            pl.BlockSpec((None, ts, 2 * D_MODEL), row),
        ),
        compiler_params=pltpu.CompilerParams(
            dimension_semantics=("parallel", "parallel"),
            vmem_limit_bytes=VMEM_LIMIT_BYTES),
        name="in_proj",
    )(x, w_in_b, b_gate, cos_t, sin_t)


def _s5_kernel(u_ref, lre_ref, lim_ref, lstep_ref, bre_ref, bim_ref, cre_ref, cim_ref,
               d_ref, wglu_ref, o_ref,
               are_s, aim_s, bb_s, state_s, ut_s, bu_s, yt_s):
    n_b, n_t, _ = u_ref.shape
    rows = n_b * n_t
    chunk = N_STATES // SSM_SLAB

    @pl.when(pl.program_id(0) == 0)
    def _():
        lr = lre_ref[...]
        li = lim_ref[...]
        dt = jnp.exp(lstep_ref[...])
        mag = jnp.exp(lr * dt)
        ar = mag * jnp.cos(li * dt)
        ai = mag * jnp.sin(li * dt)
        are_s[...] = jnp.broadcast_to(ar, are_s.shape)
        aim_s[...] = jnp.broadcast_to(ai, aim_s.shape)
        nr, ni = ar - 1.0, ai
        den = lr * lr + li * li
        fr = (nr * lr + ni * li) / den
        fi = (ni * lr - nr * li) / den
        for j in range(SSM_SLAB):
            frj = fr[:, j * chunk:(j + 1) * chunk]
            fij = fi[:, j * chunk:(j + 1) * chunk]
            br = bre_ref[j]
            bi = bim_ref[j]
            bb_s[j] = (frj * br - fij * bi).astype(BF16)
            bb_s[SSM_SLAB + j] = (frj * bi + fij * br).astype(BF16)
        state_s[...] = jnp.zeros(state_s.shape, F32)

    for b in range(n_b):
        for j in range(SSM_SLAB):
            ut_s[j, pl.ds(b, n_t, stride=n_b), :] = u_ref[b, :, j * LANES:(j + 1) * LANES]

    for j in range(SSM_SLAB):
        lhs = ut_s[j].astype(BF16)
        bu_s[:, j * chunk:(j + 1) * chunk] = _dot(lhs, bb_s[j])
        bu_s[:, N_STATES + j * chunk:N_STATES + (j + 1) * chunk] = _dot(lhs, bb_s[SSM_SLAB + j])

    for j in range(SSM_SLAB):
        c_re = slice(j * chunk, (j + 1) * chunk)
        c_im = slice(N_STATES + j * chunk, N_STATES + (j + 1) * chunk)
        ar = are_s[:, c_re]
        ai = aim_s[:, c_re]

        def step(t, carry, c_re=c_re, c_im=c_im, ar=ar, ai=ai):
            sr, si = carry
            r0 = pl.multiple_of(t * n_b, n_b)
            nsr = ar * sr - ai * si + bu_s[pl.ds(r0, n_b), c_re]
            nsi = ar * si + ai * sr + bu_s[pl.ds(r0, n_b), c_im]
            bu_s[pl.ds(r0, n_b), c_re] = nsr
            bu_s[pl.ds(r0, n_b), c_im] = nsi
            return nsr, nsi

        sr, si = lax.fori_loop(0, n_t, step, (state_s[0, :, c_re], state_s[1, :, c_re]),
                               unroll=8)
        state_s[0, :, c_re] = sr
        state_s[1, :, c_re] = si

    for j in range(SSM_SLAB):
        s_re = bu_s[:, j * chunk:(j + 1) * chunk].astype(BF16)
        s_im = bu_s[:, N_STATES + j * chunk:N_STATES + (j + 1) * chunk].astype(BF16)
        y = (_dot(s_re, cre_ref[j]) - _dot(s_im, cim_ref[j])
             + d_ref[:, j * LANES:(j + 1) * LANES] * ut_s[j])
        yt_s[j] = jax.nn.gelu(y)

    yg = jnp.concatenate([yt_s[j] for j in range(SSM_SLAB)], axis=-1).astype(BF16)
    ga = _dot(yg, wglu_ref[...])
    out = ga[:, :D_SSM] * jax.nn.sigmoid(ga[:, D_SSM:])
    for j in range(SSM_SLAB):
        yt_s[j] = out[:, j * LANES:(j + 1) * LANES]
    for b in range(n_b):
        for j in range(SSM_SLAB):
            o_ref[b, :, j * LANES:(j + 1) * LANES] = (
                yt_s[j, pl.ds(b, n_t, stride=n_b), :].astype(BF16))


def _s5(u, lam_re, lam_im, lstep, b_re_blk, b_im_blk, c_re_blk, c_im_blk, d_skip, w_glu_b):
    bsz, seq, _ = u.shape
    assert bsz == SUBLANES
    n_t = SSM_L
    rows = bsz * n_t
    chunk = N_STATES // SSM_SLAB
    c2 = lambda c: (0, 0)
    c3 = lambda c: (0, 0, 0)
    return pl.pallas_call(
        _s5_kernel,
        out_shape=jax.ShapeDtypeStruct((bsz, seq, D_SSM), BF16),
        grid=(seq // n_t,),
        in_specs=[
            pl.BlockSpec((bsz, n_t, D_SSM), lambda c: (0, c, 0)),
            pl.BlockSpec(lam_re.shape, c2),
            pl.BlockSpec(lam_im.shape, c2),
            pl.BlockSpec(lstep.shape, c2),
            pl.BlockSpec(b_re_blk.shape, c3),
            pl.BlockSpec(b_im_blk.shape, c3),
            pl.BlockSpec(c_re_blk.shape, c3),
            pl.BlockSpec(c_im_blk.shape, c3),
            pl.BlockSpec(d_skip.shape, c2),
            pl.BlockSpec(w_glu_b.shape, c2),
        ],
        out_specs=pl.BlockSpec((bsz, n_t, D_SSM), lambda c: (0, c, 0)),
        scratch_shapes=[
            pltpu.VMEM((bsz, N_STATES), F32),
            pltpu.VMEM((bsz, N_STATES), F32),
            pltpu.VMEM((2 * SSM_SLAB, LANES, chunk), BF16),
            pltpu.VMEM((2, bsz, N_STATES), F32),
            pltpu.VMEM((SSM_SLAB, rows, LANES), F32),
            pltpu.VMEM((rows, 2 * N_STATES), F32),
            pltpu.VMEM((SSM_SLAB, rows, LANES), F32),
        ],
        compiler_params=pltpu.CompilerParams(
            dimension_semantics=("arbitrary",),
            vmem_limit_bytes=VMEM_LIMIT_BYTES),
        name="s5",
    )(u, lam_re, lam_im, lstep, b_re_blk, b_im_blk, c_re_blk, c_im_blk, d_skip, w_glu_b)


def _diffattn_kernel(q_ref, k_ref, v_ref, lq1_ref, lk1_ref, lq2_ref, lk2_ref, gain_ref,
                     o_ref, qm_s, m_s, l_s, acc_s):
    qi = pl.program_id(2)
    tq = q_ref.shape[0]
    tk = tq

    q = q_ref[...]
    lane = lax.broadcasted_iota(jnp.int32, q.shape, 1)
    zero = jnp.zeros_like(q)
    qm_s[0:tq, :] = jnp.where(lane < HEAD_DIM, q, zero)
    qm_s[tq:2 * tq, :] = jnp.where(lane >= HEAD_DIM, q, zero)
    m_s[...] = jnp.full(m_s.shape, -jnp.inf, F32)
    l_s[...] = jnp.zeros(l_s.shape, F32)
    acc_s[...] = jnp.zeros(acc_s.shape, F32)

    def update(j, masked):
        r0 = pl.multiple_of(j * tk, tk)
        kt = k_ref[pl.ds(r0, tk), :]
        vt = v_ref[pl.ds(r0, tk), :]
        s = lax.dot_general(qm_s[...], kt, (((1,), (1,)), ((), ())),
                            preferred_element_type=F32)
        if masked:
            row = lax.broadcasted_iota(jnp.int32, s.shape, 0) & (tq - 1)
            col = lax.broadcasted_iota(jnp.int32, s.shape, 1)
            s = jnp.where(col <= row, s, -jnp.inf)
        m_prev = m_s[...]
        m_new = jnp.maximum(m_prev, jnp.max(s, axis=-1, keepdims=True))
        alpha = jnp.exp(m_prev - m_new)
        p = jnp.exp(s - m_new)
        l_s[...] = alpha * l_s[...] + jnp.sum(p, axis=-1, keepdims=True)
        acc_s[...] = alpha * acc_s[...] + _dot(p.astype(BF16), vt)
        m_s[...] = m_new

    def body(j, carry):
        update(j, False)
        return carry

    lax.fori_loop(0, qi, body, 0)
    update(qi, True)

    lam = (jnp.exp(jnp.sum(lq1_ref[...] * lk1_ref[...]))
           - jnp.exp(jnp.sum(lq2_ref[...] * lk2_ref[...])) + LAMBDA_INIT)
    o = acc_s[0:tq, :] / l_s[0:tq, :] - lam * (acc_s[tq:2 * tq, :] / l_s[tq:2 * tq, :])
    ms = jnp.mean(o * o, axis=-1, keepdims=True)
    o = o * lax.rsqrt(ms + RMS_EPS) * gain_ref[...] * (1.0 - LAMBDA_INIT)
    o_ref[...] = o.astype(BF16)


def _diffattn(q, k, v, lq1, lk1, lq2, lk2, gain):
    bsz, seq, _ = q.shape
    tq = ATT_TQ
    assert tq & (tq - 1) == 0
    grid = (bsz, N_HEADS, seq // tq)
    q_map = lambda b, h, i: (b, i, h)
    kv_map = lambda b, h, i: (b, 0, h)
    c2 = lambda b, h, i: (0, 0)
    return pl.pallas_call(
        _diffattn_kernel,
        out_shape=jax.ShapeDtypeStruct((bsz, seq, D_ATTN), BF16),
        grid=grid,
        in_specs=[
            pl.BlockSpec((None, tq, HEAD_WIDTH), q_map),
            pl.BlockSpec((None, seq, HEAD_WIDTH), kv_map),
            pl.BlockSpec((None, seq, HEAD_WIDTH), kv_map),
            pl.BlockSpec(lq1.shape, c2),
            pl.BlockSpec(lk1.shape, c2),
            pl.BlockSpec(lq2.shape, c2),
            pl.BlockSpec(lk2.shape, c2),
            pl.BlockSpec(gain.shape, c2),
        ],
        out_specs=pl.BlockSpec((None, tq, HEAD_WIDTH), q_map),
        scratch_shapes=[
            pltpu.VMEM((2 * tq, HEAD_WIDTH), BF16),
            pltpu.VMEM((2 * tq, 1), F32),
            pltpu.VMEM((2 * tq, 1), F32),
            pltpu.VMEM((2 * tq, HEAD_WIDTH), F32),
        ],
        compiler_params=pltpu.CompilerParams(
            dimension_semantics=("parallel", "parallel", "arbitrary"),
            vmem_limit_bytes=VMEM_LIMIT_BYTES),
        name="diffattn",
    )(q, k, v, lq1, lk1, lq2, lk2, gain)


def _merge_kernel(x_ref, g_ref, ys_ref, ya_ref, wps_ref, wpa_ref, wo_ref, lng_ref, lnb_ref,
                  o_ref):
    p_ssm = _dot(ys_ref[...], wps_ref[...])
    p_att = _dot(ya_ref[...], wpa_ref[...])
    merged = (g_ref[:, 0:D_MODEL].astype(F32) * p_ssm
              + g_ref[:, D_MODEL:2 * D_MODEL].astype(F32) * p_att)
    mix = _dot(merged.astype(BF16), wo_ref[...])
    z = DEEPNORM_ALPHA * x_ref[...] + mix
    o_ref[...] = _layer_norm(z, lng_ref[...], lnb_ref[...])


def _merge(x2, gates2, ys2, ya2, wps, wpa, wo, ln_g, ln_b):
    n, _ = x2.shape
    tm = MERGE_TM
    row = lambda i: (i, 0)
    c2 = lambda i: (0, 0)
    return pl.pallas_call(
        _merge_kernel,
        out_shape=jax.ShapeDtypeStruct((n, D_MODEL), F32),
        grid=(n // tm,),
        in_specs=[
            pl.BlockSpec((tm, D_MODEL), row),
            pl.BlockSpec((tm, 2 * D_MODEL), row),
            pl.BlockSpec((tm, D_SSM), row),
            pl.BlockSpec((tm, D_ATTN), row),
            pl.BlockSpec(wps.shape, c2),
            pl.BlockSpec(wpa.shape, c2),
            pl.BlockSpec(wo.shape, c2),
            pl.BlockSpec(ln_g.shape, c2),
            pl.BlockSpec(ln_b.shape, c2),
        ],
        out_specs=pl.BlockSpec((tm, D_MODEL), row),
        compiler_params=pltpu.CompilerParams(
            dimension_semantics=("parallel",),
            vmem_limit_bytes=VMEM_LIMIT_BYTES),
        name="merge",
    )(x2, gates2, ys2, ya2, wps, wpa, wo, ln_g, ln_b)


def _ffn_kernel(x_ref, wg_ref, wu_ref, wd_ref, lng_ref, lnb_ref, o_ref):
    x = x_ref[...]
    xb = x.astype(BF16)
    ff = jnp.zeros(x.shape, F32)
    for c in range(D_FF // FFN_CHUNK):
        cols = slice(c * FFN_CHUNK, (c + 1) * FFN_CHUNK)
        a = _dot(xb, wg_ref[:, cols])
        b = _dot(xb, wu_ref[:, cols])
        h = (jax.nn.silu(a) * b).astype(BF16)
        ff = ff + _dot(h, wd_ref[cols, :])
    z = DEEPNORM_ALPHA * x + ff
    o_ref[...] = _layer_norm(z, lng_ref[...], lnb_ref[...])


def _ffn(x2, wg, wu, wd, ln_g, ln_b):
    n, _ = x2.shape
    tm = FFN_TM
    row = lambda i: (i, 0)
    c2 = lambda i: (0, 0)
    resident = functools.partial(pl.BlockSpec, index_map=c2, pipeline_mode=pl.Buffered(1))
    return pl.pallas_call(
        _ffn_kernel,
        out_shape=jax.ShapeDtypeStruct((n, D_MODEL), F32),
        grid=(n // tm,),
        in_specs=[
            pl.BlockSpec((tm, D_MODEL), row),
            resident(wg.shape),
            resident(wu.shape),
            resident(wd.shape),
            pl.BlockSpec(ln_g.shape, c2),
            pl.BlockSpec(ln_b.shape, c2),
        ],
        out_specs=pl.BlockSpec((tm, D_MODEL), row),
        compiler_params=pltpu.CompilerParams(
            dimension_semantics=("parallel",),
            vmem_limit_bytes=VMEM_LIMIT_BYTES),
        name="ffn",
    )(x2, wg, wu, wd, ln_g, ln_b)


def _rope_tables(seq):
    half = HEAD_DIM // 2
    inv_freq = ROPE_THETA ** (-jnp.arange(0, HEAD_DIM, 2, dtype=F32) / HEAD_DIM)
    ang = jnp.arange(seq, dtype=F32)[:, None] * inv_freq[None, :]
    cos, sin = jnp.cos(ang), jnp.sin(ang)
    reps = LANES // half
    cos_t = jnp.tile(cos, (1, reps))
    sign = jnp.tile(jnp.concatenate([-jnp.ones((half,), F32), jnp.ones((half,), F32)]),
                    LANES // HEAD_DIM)
    sin_t = jnp.tile(sin, (1, reps)) * sign[None, :]
    return cos_t, sin_t


def _block_diag_in(b):
    gps = N_SSM_GROUPS // SSM_SLAB
    bt = jnp.swapaxes(b, 1, 2).reshape(SSM_SLAB, gps, SSM_GROUP, SSM_STATE)
    eye = jnp.eye(gps, dtype=b.dtype)
    blk = bt[:, :, :, None, :] * eye[None, :, None, :, None]
    return blk.reshape(SSM_SLAB, gps * SSM_GROUP, gps * SSM_STATE)


def _block_diag_out(c):
    gps = N_SSM_GROUPS // SSM_SLAB
    ct = jnp.swapaxes(c, 1, 2).reshape(SSM_SLAB, gps, SSM_STATE, SSM_GROUP)
    eye = jnp.eye(gps, dtype=c.dtype)
    blk = ct[:, :, :, None, :] * eye[None, :, None, :, None]
    return blk.reshape(SSM_SLAB, gps * SSM_STATE, gps * SSM_GROUP)


def kernel(x, w_in, b_gate, ssm_lambda_re, ssm_lambda_im, ssm_log_step, ssm_b_re, ssm_b_im, ssm_c_re, ssm_c_im, ssm_d, w_glu, lambda_q1, lambda_k1, lambda_q2, lambda_k2, subln_gain, w_proj_ssm, w_proj_attn, w_out, ln1_g, ln1_b, w_ffn_gate, w_ffn_up, w_ffn_down, ln2_g, ln2_b):
    bsz, seq, _ = x.shape
    cos_t, sin_t = _rope_tables(seq)
    for l in range(DEPTH):
        u, q, k, v, gates = _in_proj(x, w_in[l].astype(BF16), b_gate[l][None, :], cos_t, sin_t)

        lstep = jnp.broadcast_to(ssm_log_step[l][:, None], (N_SSM_GROUPS, SSM_STATE))
        y_ssm = _s5(
            u,
            ssm_lambda_re[l].reshape(1, N_STATES),
            ssm_lambda_im[l].reshape(1, N_STATES),
            lstep.reshape(1, N_STATES),
            _block_diag_in(ssm_b_re[l]), _block_diag_in(ssm_b_im[l]),
            _block_diag_out(ssm_c_re[l]).astype(BF16), _block_diag_out(ssm_c_im[l]).astype(BF16),
            ssm_d[l].reshape(1, D_SSM),
            w_glu[l].astype(BF16))

        y_att = _diffattn(q, k, v, lambda_q1[l][None, :], lambda_k1[l][None, :],
                          lambda_q2[l][None, :], lambda_k2[l][None, :], subln_gain[l][None, :])

        n = bsz * seq
        x1 = _merge(x.reshape(n, D_MODEL), gates.reshape(n, 2 * D_MODEL),
                    y_ssm.reshape(n, D_SSM), y_att.reshape(n, D_ATTN),
                    w_proj_ssm[l].astype(BF16), w_proj_attn[l].astype(BF16),
                    w_out[l].astype(BF16), ln1_g[l][None, :], ln1_b[l][None, :])
        x2 = _ffn(x1, w_ffn_gate[l].astype(BF16), w_ffn_up[l].astype(BF16),
                  w_ffn_down[l].astype(BF16), ln2_g[l][None, :], ln2_b[l][None, :])
        x = x2.reshape(bsz, seq, D_MODEL)
    return x
```

```python
import functools
import math

import jax
import jax.numpy as jnp
from jax import lax
from jax.experimental import pallas as pl
from jax.experimental.pallas import tpu as pltpu

F32 = jnp.float32
BF16 = jnp.bfloat16

D_MODEL = 1024
D_SSM = 512
SSM_GROUP = 16
N_SSM_GROUPS = 32
SSM_STATE = 64
N_STATES = N_SSM_GROUPS * SSM_STATE
D_ATTN = 512
HEAD_DIM = 64
HEAD_WIDTH = 2 * HEAD_DIM
N_HEADS = 4
ROPE_THETA = 10000.0
D_FF = 2816
DEPTH = 1
DEEPNORM_ALPHA = (2.0 * DEPTH) ** 0.25
LN_EPS = 1e-5
RMS_EPS = 1e-5
LAMBDA_INIT = 0.8 - 0.6 * math.exp(-0.3 * 0)

LANES = 128
SUBLANES = 8
VMEM_LIMIT_BYTES = 56 * 1024 * 1024

IN_TS = 512
SSM_L = 64
SSM_SLAB = 4
ATT_TQ = 256
VT_ROWS = HEAD_WIDTH + 16
MERGE_TM = 512
FFN_TM = 512
FFN_CHUNK = 1408


def _dot(a, b):
    return jnp.dot(a, b, preferred_element_type=F32)


def _layer_norm(z, g, b):
    mu = jnp.mean(z, axis=-1, keepdims=True)
    zc = z - mu
    var = jnp.mean(zc * zc, axis=-1, keepdims=True)
    return zc * lax.rsqrt(var + LN_EPS) * g + b


def _in_proj_kernel(x_ref, w_ref, bg_ref, cos_ref, sin_ref,
                    u_ref, q_ref, k_ref, vt_ref, g_ref):
    xb = x_ref[...].astype(BF16)
    u_ref[...] = _dot(xb, w_ref[:, 0:D_SSM])
    cos = cos_ref[...]
    sin = sin_ref[...]
    lane = lax.broadcasted_iota(jnp.int32, cos.shape, 1)
    low_half = (lane & (HEAD_DIM // 2)) == 0
    for out_ref, off, scale in ((q_ref, D_SSM, HEAD_DIM ** -0.5 * math.log2(math.e)),
                                (k_ref, D_SSM + D_ATTN, 1.0)):
        h = _dot(xb, w_ref[:, off:off + D_ATTN])
        for hd in range(N_HEADS):
            slab = h[:, hd * LANES:(hd + 1) * LANES]
            partner = jnp.where(low_half,
                                pltpu.roll(slab, LANES - HEAD_DIM // 2, 1),
                                pltpu.roll(slab, HEAD_DIM // 2, 1))
            r = slab * cos + partner * sin
            if scale != 1.0:
                r = r * scale
            out_ref[:, hd * LANES:(hd + 1) * LANES] = r.astype(BF16)
    off_v = D_SSM + 2 * D_ATTN
    v = _dot(xb, w_ref[:, off_v:off_v + D_ATTN])
    ones = jnp.ones((VT_ROWS - HEAD_WIDTH, ATT_TQ), BF16)
    for hd in range(N_HEADS):
        for r in range(vt_ref.shape[1]):
            tile = v[r * ATT_TQ:(r + 1) * ATT_TQ, hd * LANES:(hd + 1) * LANES]
            vt_ref[hd, r, 0:HEAD_WIDTH, :] = tile.T.astype(BF16)
            vt_ref[hd, r, HEAD_WIDTH:VT_ROWS, :] = ones
    off_g = off_v + D_ATTN
    g = _dot(xb, w_ref[:, off_g:off_g + 2 * D_MODEL]) + bg_ref[...]
    g_ref[...] = jax.nn.sigmoid(g).astype(BF16)


def _in_proj(x, w_in_b, b_gate, cos_t, sin_t):
    bsz, seq, _ = x.shape
    ts = IN_TS
    grid = (bsz, seq // ts)
    row = lambda b, i: (b, i, 0)
    const2 = lambda b, i: (0, 0)
    out_shape = (
        jax.ShapeDtypeStruct((bsz, seq, D_SSM), F32),
        jax.ShapeDtypeStruct((bsz, seq, D_ATTN), BF16),
        jax.ShapeDtypeStruct((bsz, seq, D_ATTN), BF16),
        jax.ShapeDtypeStruct((bsz, N_HEADS, seq // ATT_TQ, VT_ROWS, ATT_TQ), BF16),
        jax.ShapeDtypeStruct((bsz, seq, 2 * D_MODEL), BF16),
    )
    return pl.pallas_call(
        _in_proj_kernel,
        out_shape=out_shape,
        grid=grid,
        in_specs=[
            pl.BlockSpec((None, ts, D_MODEL), row),
            pl.BlockSpec(w_in_b.shape, const2),
            pl.BlockSpec(b_gate.shape, const2),
            pl.BlockSpec((ts, LANES), lambda b, i: (i, 0)),
            pl.BlockSpec((ts, LANES), lambda b, i: (i, 0)),
        ],
        out_specs=(
            pl.BlockSpec((None, ts, D_SSM), row),
            pl.BlockSpec((None, ts, D_ATTN), row),
            pl.BlockSpec((None, ts, D_ATTN), row),
            pl.BlockSpec((None, N_HEADS, ts // ATT_TQ, VT_ROWS, ATT_TQ),
                         lambda b, i: (b, 0, i, 0, 0)),
            pl.BlockSpec((None, ts, 2 * D_MODEL), row),
        ),
        compiler_params=pltpu.CompilerParams(
            dimension_semantics=("parallel", "parallel"),
            vmem_limit_bytes=VMEM_LIMIT_BYTES),
        name="in_proj",
    )(x, w_in_b, b_gate, cos_t, sin_t)


def _s5_kernel(u_ref, lre_ref, lim_ref, lstep_ref, bre_ref, bim_ref, cre_ref, cim_ref,
               d_ref, wglu_ref, o_ref,
               are_s, aim_s, bb_s, state_s, ut_s, bu_s, yt_s):
    n_b, n_t, _ = u_ref.shape
    rows = n_b * n_t
    chunk = N_STATES // SSM_SLAB

    @pl.when(pl.program_id(0) == 0)
    def _():
        lr = lre_ref[...]
        li = lim_ref[...]
        dt = jnp.exp(lstep_ref[...])
        mag = jnp.exp(lr * dt)
        ar = mag * jnp.cos(li * dt)
        ai = mag * jnp.sin(li * dt)
        are_s[...] = jnp.broadcast_to(ar, are_s.shape)
        aim_s[...] = jnp.broadcast_to(ai, aim_s.shape)
        nr, ni = ar - 1.0, ai
        den = lr * lr + li * li
        fr = (nr * lr + ni * li) / den
        fi = (ni * lr - nr * li) / den
        for j in range(SSM_SLAB):
            frj = fr[:, j * chunk:(j + 1) * chunk]
            fij = fi[:, j * chunk:(j + 1) * chunk]
            br = bre_ref[j]
            bi = bim_ref[j]
            bb_s[j] = (frj * br - fij * bi).astype(BF16)
            bb_s[SSM_SLAB + j] = (frj * bi + fij * br).astype(BF16)
        state_s[...] = jnp.zeros(state_s.shape, F32)

    for b in range(n_b):
        for j in range(SSM_SLAB):
            ut_s[j, pl.ds(b, n_t, stride=n_b), :] = u_ref[b, :, j * LANES:(j + 1) * LANES]

    for j in range(SSM_SLAB):
        lhs = ut_s[j].astype(BF16)
        bu_s[:, j * chunk:(j + 1) * chunk] = _dot(lhs, bb_s[j])
        bu_s[:, N_STATES + j * chunk:N_STATES + (j + 1) * chunk] = _dot(lhs, bb_s[SSM_SLAB + j])

    for j in range(SSM_SLAB):
        c_re = slice(j * chunk, (j + 1) * chunk)
        c_im = slice(N_STATES + j * chunk, N_STATES + (j + 1) * chunk)
        ar = are_s[:, c_re]
        ai = aim_s[:, c_re]

        def step(t, carry, c_re=c_re, c_im=c_im, ar=ar, ai=ai):
            sr, si = carry
            r0 = pl.multiple_of(t * n_b, n_b)
            nsr = ar * sr - ai * si + bu_s[pl.ds(r0, n_b), c_re]
            nsi = ar * si + ai * sr + bu_s[pl.ds(r0, n_b), c_im]
            bu_s[pl.ds(r0, n_b), c_re] = nsr
            bu_s[pl.ds(r0, n_b), c_im] = nsi
            return nsr, nsi

        sr, si = lax.fori_loop(0, n_t, step, (state_s[0, :, c_re], state_s[1, :, c_re]),
                               unroll=8)
        state_s[0, :, c_re] = sr
        state_s[1, :, c_re] = si

    for j in range(SSM_SLAB):
        s_re = bu_s[:, j * chunk:(j + 1) * chunk].astype(BF16)
        s_im = bu_s[:, N_STATES + j * chunk:N_STATES + (j + 1) * chunk].astype(BF16)
        y = (_dot(s_re, cre_ref[j]) - _dot(s_im, cim_ref[j])
             + d_ref[:, j * LANES:(j + 1) * LANES] * ut_s[j])
        yt_s[j] = jax.nn.gelu(y)

    yg = jnp.concatenate([yt_s[j] for j in range(SSM_SLAB)], axis=-1).astype(BF16)
    ga = _dot(yg, wglu_ref[...])
    out = ga[:, :D_SSM] * jax.nn.sigmoid(ga[:, D_SSM:])
    for j in range(SSM_SLAB):
        yt_s[j] = out[:, j * LANES:(j + 1) * LANES]
    for b in range(n_b):
        for j in range(SSM_SLAB):
            o_ref[b, :, j * LANES:(j + 1) * LANES] = (
                yt_s[j, pl.ds(b, n_t, stride=n_b), :].astype(BF16))


def _s5(u, lam_re, lam_im, lstep, b_re_blk, b_im_blk, c_re_blk, c_im_blk, d_skip, w_glu_b):
    bsz, seq, _ = u.shape
    assert bsz == SUBLANES
    n_t = SSM_L
    rows = bsz * n_t
    chunk = N_STATES // SSM_SLAB
    c2 = lambda c: (0, 0)
    c3 = lambda c: (0, 0, 0)
    return pl.pallas_call(
        _s5_kernel,
        out_shape=jax.ShapeDtypeStruct((bsz, seq, D_SSM), BF16),
        grid=(seq // n_t,),
        in_specs=[
            pl.BlockSpec((bsz, n_t, D_SSM), lambda c: (0, c, 0)),
            pl.BlockSpec(lam_re.shape, c2),
            pl.BlockSpec(lam_im.shape, c2),
            pl.BlockSpec(lstep.shape, c2),
            pl.BlockSpec(b_re_blk.shape, c3),
            pl.BlockSpec(b_im_blk.shape, c3),
            pl.BlockSpec(c_re_blk.shape, c3),
            pl.BlockSpec(c_im_blk.shape, c3),
            pl.BlockSpec(d_skip.shape, c2),
            pl.BlockSpec(w_glu_b.shape, c2),
        ],
        out_specs=pl.BlockSpec((bsz, n_t, D_SSM), lambda c: (0, c, 0)),
        scratch_shapes=[
            pltpu.VMEM((bsz, N_STATES), F32),
            pltpu.VMEM((bsz, N_STATES), F32),
            pltpu.VMEM((2 * SSM_SLAB, LANES, chunk), BF16),
            pltpu.VMEM((2, bsz, N_STATES), F32),
            pltpu.VMEM((SSM_SLAB, rows, LANES), F32),
            pltpu.VMEM((rows, 2 * N_STATES), F32),
            pltpu.VMEM((SSM_SLAB, rows, LANES), F32),
        ],
        compiler_params=pltpu.CompilerParams(
            dimension_semantics=("arbitrary",),
            vmem_limit_bytes=VMEM_LIMIT_BYTES),
        name="s5",
    )(u, lam_re, lam_im, lstep, b_re_blk, b_im_blk, c_re_blk, c_im_blk, d_skip, w_glu_b)


def _sublane_allreduce(x, op):
    for shift in (4, 2, 1):
        x = op(x, pltpu.roll(x, shift, 0))
    return x


def _diffattn_kernel(q_ref, k_ref, vt_ref, lq1_ref, lk1_ref, lq2_ref, lk2_ref, gain_ref,
                     o_ref, qmt_s, m_s, acc_s):
    qi = pl.program_id(1)
    tq = q_ref.shape[0]
    tk = tq
    n = 2 * tq

    for h in range(N_HEADS):
        qt = q_ref[:, h * HEAD_WIDTH:(h + 1) * HEAD_WIDTH].astype(F32).T
        d = lax.broadcasted_iota(jnp.int32, qt.shape, 0)
        qmt_s[h, :, 0:tq] = jnp.where(d < HEAD_DIM, qt, 0.0).astype(BF16)
        qmt_s[h, :, tq:n] = jnp.where(d >= HEAD_DIM, qt, 0.0).astype(BF16)
    m_s[...] = jnp.full(m_s.shape, -jnp.inf, F32)
    acc_s[...] = jnp.zeros(acc_s.shape, F32)
    acc_groups = acc_s.shape[1] // SUBLANES

    def update(j, masked):
        r0 = pl.multiple_of(j * tk, tk)

        def scores(h):
            kt = k_ref[pl.ds(r0, tk), h * HEAD_WIDTH:(h + 1) * HEAD_WIDTH]
            return _dot(kt, qmt_s[h])

        s_next = scores(0)
        for h in range(N_HEADS):
            s = s_next
            if h + 1 < N_HEADS:
                s_next = scores(h + 1)
            if masked:
                key = lax.broadcasted_iota(jnp.int32, s.shape, 0)
                qry = lax.broadcasted_iota(jnp.int32, s.shape, 1) & (tq - 1)
                s = jnp.where(key <= qry, s, -jnp.inf)
            s3 = s.reshape(tk // SUBLANES, SUBLANES, n)
            m_prev = m_s[h]
            m_new = jnp.maximum(m_prev, _sublane_allreduce(jnp.max(s3, axis=0), jnp.maximum))
            alpha = jnp.exp2(m_prev - m_new)
            p3 = jnp.exp2(s3 - m_new[None])
            pv = _dot(vt_ref[h, j], p3.reshape(tk, n).astype(BF16))
            acc3 = acc_s[h].reshape(acc_groups, SUBLANES, n)
            acc_s[h] = (alpha[None] * acc3).reshape(acc_s.shape[1], n) + pv
            m_s[h] = m_new

    def body(j, carry):
        update(j, False)
        return carry

    lax.fori_loop(0, qi, body, 0)
    update(qi, True)

    lam = (jnp.exp(jnp.sum(lq1_ref[...] * lk1_ref[...], keepdims=True))
           - jnp.exp(jnp.sum(lq2_ref[...] * lk2_ref[...], keepdims=True)) + LAMBDA_INIT)
    for h in range(N_HEADS):
        num3 = acc_s[h, 0:HEAD_WIDTH, :].reshape(HEAD_WIDTH // SUBLANES, SUBLANES, n)
        den = acc_s[h, HEAD_WIDTH:HEAD_WIDTH + SUBLANES, :]
        on = (num3 / den[None]).reshape(HEAD_WIDTH, n)
        o = (on[:, 0:tq] - lam * on[:, tq:n]).T
        ms = jnp.mean(o * o, axis=-1, keepdims=True)
        o = o * lax.rsqrt(ms + RMS_EPS) * gain_ref[...] * (1.0 - LAMBDA_INIT)
        o_ref[:, h * HEAD_WIDTH:(h + 1) * HEAD_WIDTH] = o.astype(BF16)


def _diffattn(q, k, vt, lq1, lk1, lq2, lk2, gain):
    bsz, seq, _ = q.shape
    tq = ATT_TQ
    assert tq & (tq - 1) == 0
    n_kv = seq // tq
    grid = (bsz, n_kv)
    c2 = lambda b, i: (0, 0)
    return pl.pallas_call(
        _diffattn_kernel,
        out_shape=jax.ShapeDtypeStruct((bsz, seq, D_ATTN), BF16),
        grid=grid,
        in_specs=[
            pl.BlockSpec((None, tq, D_ATTN), lambda b, i: (b, i, 0)),
            pl.BlockSpec((None, seq, D_ATTN), lambda b, i: (b, 0, 0)),
            pl.BlockSpec((None, N_HEADS, n_kv, VT_ROWS, tq), lambda b, i: (b, 0, 0, 0, 0)),
            pl.BlockSpec(lq1.shape, c2),
            pl.BlockSpec(lk1.shape, c2),
            pl.BlockSpec(lq2.shape, c2),
            pl.BlockSpec(lk2.shape, c2),
            pl.BlockSpec(gain.shape, c2),
        ],
        out_specs=pl.BlockSpec((None, tq, D_ATTN), lambda b, i: (b, i, 0)),
        scratch_shapes=[
            pltpu.VMEM((N_HEADS, HEAD_WIDTH, 2 * tq), BF16),
            pltpu.VMEM((N_HEADS, SUBLANES, 2 * tq), F32),
            pltpu.VMEM((N_HEADS, VT_ROWS, 2 * tq), F32),
        ],
        compiler_params=pltpu.CompilerParams(
            dimension_semantics=("parallel", "arbitrary"),
            vmem_limit_bytes=VMEM_LIMIT_BYTES),
        name="diffattn",
    )(q, k, vt, lq1, lk1, lq2, lk2, gain)


def _merge_kernel(x_ref, g_ref, ys_ref, ya_ref, wps_ref, wpa_ref, wo_ref, lng_ref, lnb_ref,
                  o_ref):
    p_ssm = _dot(ys_ref[...], wps_ref[...])
    p_att = _dot(ya_ref[...], wpa_ref[...])
    merged = (g_ref[:, 0:D_MODEL].astype(F32) * p_ssm
              + g_ref[:, D_MODEL:2 * D_MODEL].astype(F32) * p_att)
    mix = _dot(merged.astype(BF16), wo_ref[...])
    z = DEEPNORM_ALPHA * x_ref[...] + mix
    o_ref[...] = _layer_norm(z, lng_ref[...], lnb_ref[...])


def _merge(x2, gates2, ys2, ya2, wps, wpa, wo, ln_g, ln_b):
    n, _ = x2.shape
    tm = MERGE_TM
    row = lambda i: (i, 0)
    c2 = lambda i: (0, 0)
    return pl.pallas_call(
        _merge_kernel,
        out_shape=jax.ShapeDtypeStruct((n, D_MODEL), F32),
        grid=(n // tm,),
        in_specs=[
            pl.BlockSpec((tm, D_MODEL), row),
            pl.BlockSpec((tm, 2 * D_MODEL), row),
            pl.BlockSpec((tm, D_SSM), row),
            pl.BlockSpec((tm, D_ATTN), row),
            pl.BlockSpec(wps.shape, c2),
            pl.BlockSpec(wpa.shape, c2),
            pl.BlockSpec(wo.shape, c2),
            pl.BlockSpec(ln_g.shape, c2),
            pl.BlockSpec(ln_b.shape, c2),
        ],
        out_specs=pl.BlockSpec((tm, D_MODEL), row),
        compiler_params=pltpu.CompilerParams(
            dimension_semantics=("parallel",),
            vmem_limit_bytes=VMEM_LIMIT_BYTES),
        name="merge",
    )(x2, gates2, ys2, ya2, wps, wpa, wo, ln_g, ln_b)


def _ffn_kernel(x_ref, wg_ref, wu_ref, wd_ref, lng_ref, lnb_ref, o_ref):
    x = x_ref[...]
    xb = x.astype(BF16)
    ff = jnp.zeros(x.shape, F32)
    for c in range(D_FF // FFN_CHUNK):
        cols = slice(c * FFN_CHUNK, (c + 1) * FFN_CHUNK)
        a = _dot(xb, wg_ref[:, cols])
        b = _dot(xb, wu_ref[:, cols])
        h = (jax.nn.silu(a) * b).astype(BF16)
        ff = ff + _dot(h, wd_ref[cols, :])
    z = DEEPNORM_ALPHA * x + ff
    o_ref[...] = _layer_norm(z, lng_ref[...], lnb_ref[...])


def _ffn(x2, wg, wu, wd, ln_g, ln_b):
    n, _ = x2.shape
    tm = FFN_TM
    row = lambda i: (i, 0)
    c2 = lambda i: (0, 0)
    resident = functools.partial(pl.BlockSpec, index_map=c2, pipeline_mode=pl.Buffered(1))
    return pl.pallas_call(
        _ffn_kernel,
        out_shape=jax.ShapeDtypeStruct((n, D_MODEL), F32),
        grid=(n // tm,),
        in_specs=[
            pl.BlockSpec((tm, D_MODEL), row),
            resident(wg.shape),
            resident(wu.shape),
            resident(wd.shape),
            pl.BlockSpec(ln_g.shape, c2),
            pl.BlockSpec(ln_b.shape, c2),
        ],
        out_specs=pl.BlockSpec((tm, D_MODEL), row),
        compiler_params=pltpu.CompilerParams(
            dimension_semantics=("parallel",),
            vmem_limit_bytes=VMEM_LIMIT_BYTES),
        name="ffn",
    )(x2, wg, wu, wd, ln_g, ln_b)


def _rope_tables(seq):
    half = HEAD_DIM // 2
    inv_freq = ROPE_THETA ** (-jnp.arange(0, HEAD_DIM, 2, dtype=F32) / HEAD_DIM)
    ang = jnp.arange(seq, dtype=F32)[:, None] * inv_freq[None, :]
    cos, sin = jnp.cos(ang), jnp.sin(ang)
    reps = LANES // half
    cos_t = jnp.tile(cos, (1, reps))
    sign = jnp.tile(jnp.concatenate([-jnp.ones((half,), F32), jnp.ones((half,), F32)]),
                    LANES // HEAD_DIM)
    sin_t = jnp.tile(sin, (1, reps)) * sign[None, :]
    return cos_t, sin_t


def _block_diag_in(b):
    gps = N_SSM_GROUPS // SSM_SLAB
    bt = jnp.swapaxes(b, 1, 2).reshape(SSM_SLAB, gps, SSM_GROUP, SSM_STATE)
    eye = jnp.eye(gps, dtype=b.dtype)
    blk = bt[:, :, :, None, :] * eye[None, :, None, :, None]
    return blk.reshape(SSM_SLAB, gps * SSM_GROUP, gps * SSM_STATE)


def _block_diag_out(c):
    gps = N_SSM_GROUPS // SSM_SLAB
    ct = jnp.swapaxes(c, 1, 2).reshape(SSM_SLAB, gps, SSM_STATE, SSM_GROUP)
    eye = jnp.eye(gps, dtype=c.dtype)
    blk = ct[:, :, :, None, :] * eye[None, :, None, :, None]
    return blk.reshape(SSM_SLAB, gps * SSM_STATE, gps * SSM_GROUP)


def kernel(x, w_in, b_gate, ssm_lambda_re, ssm_lambda_im, ssm_log_step, ssm_b_re, ssm_b_im, ssm_c_re, ssm_c_im, ssm_d, w_glu, lambda_q1, lambda_k1, lambda_q2, lambda_k2, subln_gain, w_proj_ssm, w_proj_attn, w_out, ln1_g, ln1_b, w_ffn_gate, w_ffn_up, w_ffn_down, ln2_g, ln2_b):
    bsz, seq, _ = x.shape
    cos_t, sin_t = _rope_tables(seq)
    for l in range(DEPTH):
        u, q, k, vt, gates = _in_proj(x, w_in[l].astype(BF16), b_gate[l][None, :], cos_t, sin_t)

        lstep = jnp.broadcast_to(ssm_log_step[l][:, None], (N_SSM_GROUPS, SSM_STATE))
        y_ssm = _s5(
            u,
            ssm_lambda_re[l].reshape(1, N_STATES),
            ssm_lambda_im[l].reshape(1, N_STATES),
            lstep.reshape(1, N_STATES),
            _block_diag_in(ssm_b_re[l]), _block_diag_in(ssm_b_im[l]),
            _block_diag_out(ssm_c_re[l]).astype(BF16), _block_diag_out(ssm_c_im[l]).astype(BF16),
            ssm_d[l].reshape(1, D_SSM),
            w_glu[l].astype(BF16))

        y_att = _diffattn(q, k, vt,lambda_q1[l][None, :], lambda_k1[l][None, :],
                          lambda_q2[l][None, :], lambda_k2[l][None, :], subln_gain[l][None, :])

        n = bsz * seq
        x1 = _merge(x.reshape(n, D_MODEL), gates.reshape(n, 2 * D_MODEL),
                    y_ssm.reshape(n, D_SSM), y_att.reshape(n, D_ATTN),
                    w_proj_ssm[l].astype(BF16), w_proj_attn[l].astype(BF16),
                    w_out[l].astype(BF16), ln1_g[l][None, :], ln1_b[l][None, :])
        x2 = _ffn(x1, w_ffn_gate[l].astype(BF16), w_ffn_up[l].astype(BF16),
                  w_ffn_down[l].astype(BF16), ln2_g[l][None, :], ln2_b[l][None, :])
        x = x2.reshape(bsz, seq, D_MODEL)
    return x
```

```python
import functools
import math

import jax
import jax.numpy as jnp
from jax import lax
from jax.experimental import pallas as pl
from jax.experimental.pallas import tpu as pltpu

F32 = jnp.float32
BF16 = jnp.bfloat16

D_MODEL = 1024
D_SSM = 512
SSM_GROUP = 16
N_SSM_GROUPS = 32
SSM_STATE = 64
N_STATES = N_SSM_GROUPS * SSM_STATE
D_ATTN = 512
HEAD_DIM = 64
HEAD_WIDTH = 2 * HEAD_DIM
N_HEADS = 4
ROPE_THETA = 10000.0
D_FF = 2816
DEPTH = 1
DEEPNORM_ALPHA = (2.0 * DEPTH) ** 0.25
LN_EPS = 1e-5
RMS_EPS = 1e-5
LAMBDA_INIT = 0.8 - 0.6 * math.exp(-0.3 * 0)

LANES = 128
SUBLANES = 8
VMEM_LIMIT_BYTES = 56 * 1024 * 1024

IN_TS = 512
SSM_L = 64
SSM_SLAB = 4
ATT_TQ = 256
VT_ROWS = HEAD_WIDTH + 16
MERGE_TM = 512
FFN_TM = 512
FFN_CHUNK = 1408


def _dot(a, b):
    return jnp.dot(a, b, preferred_element_type=F32)


def _layer_norm(z, g, b):
    mu = jnp.mean(z, axis=-1, keepdims=True)
    zc = z - mu
    var = jnp.mean(zc * zc, axis=-1, keepdims=True)
    return zc * lax.rsqrt(var + LN_EPS) * g + b


def _in_proj_kernel(x_ref, w_ref, bg_ref, cos_ref, sin_ref,
                    u_ref, q_ref, k_ref, vt_ref, g_ref):
    xb = x_ref[...].astype(BF16)
    u_ref[...] = _dot(xb, w_ref[:, 0:D_SSM])
    cos = cos_ref[...]
    sin = sin_ref[...]
    lane = lax.broadcasted_iota(jnp.int32, cos.shape, 1)
    low_half = (lane & (HEAD_DIM // 2)) == 0
    for out_ref, off, scale in ((q_ref, D_SSM, HEAD_DIM ** -0.5 * math.log2(math.e)),
                                (k_ref, D_SSM + D_ATTN, 1.0)):
        h = _dot(xb, w_ref[:, off:off + D_ATTN])
        for hd in range(N_HEADS):
            slab = h[:, hd * LANES:(hd + 1) * LANES]
            partner = jnp.where(low_half,
                                pltpu.roll(slab, LANES - HEAD_DIM // 2, 1),
                                pltpu.roll(slab, HEAD_DIM // 2, 1))
            r = slab * cos + partner * sin
            if scale != 1.0:
                r = r * scale
            out_ref[:, hd * LANES:(hd + 1) * LANES] = r.astype(BF16)
    off_v = D_SSM + 2 * D_ATTN
    v = _dot(xb, w_ref[:, off_v:off_v + D_ATTN])
    ones = jnp.ones((VT_ROWS - HEAD_WIDTH, ATT_TQ), BF16)
    for hd in range(N_HEADS):
        for r in range(vt_ref.shape[1]):
            tile = v[r * ATT_TQ:(r + 1) * ATT_TQ, hd * LANES:(hd + 1) * LANES]
            vt_ref[hd, r, 0:HEAD_WIDTH, :] = tile.T.astype(BF16)
            vt_ref[hd, r, HEAD_WIDTH:VT_ROWS, :] = ones
    off_g = off_v + D_ATTN
    g = _dot(xb, w_ref[:, off_g:off_g + 2 * D_MODEL]) + bg_ref[...]
    g_ref[...] = jax.nn.sigmoid(g).astype(BF16)


def _in_proj(x, w_in_b, b_gate, cos_t, sin_t):
    bsz, seq, _ = x.shape
    ts = IN_TS
    grid = (bsz, seq // ts)
    row = lambda b, i: (b, i, 0)
    const2 = lambda b, i: (0, 0)
    out_shape = (
        jax.ShapeDtypeStruct((bsz, seq, D_SSM), F32),
        jax.ShapeDtypeStruct((bsz, seq, D_ATTN), BF16),
        jax.ShapeDtypeStruct((bsz, seq, D_ATTN), BF16),
        jax.ShapeDtypeStruct((bsz, N_HEADS, seq // ATT_TQ, VT_ROWS, ATT_TQ), BF16),
        jax.ShapeDtypeStruct((bsz, seq, 2 * D_MODEL), BF16),
    )
    return pl.pallas_call(
        _in_proj_kernel,
        out_shape=out_shape,
        grid=grid,
        in_specs=[
            pl.BlockSpec((None, ts, D_MODEL), row),
            pl.BlockSpec(w_in_b.shape, const2),
            pl.BlockSpec(b_gate.shape, const2),
            pl.BlockSpec((ts, LANES), lambda b, i: (i, 0)),
            pl.BlockSpec((ts, LANES), lambda b, i: (i, 0)),
        ],
        out_specs=(
            pl.BlockSpec((None, ts, D_SSM), row),
            pl.BlockSpec((None, ts, D_ATTN), row),
            pl.BlockSpec((None, ts, D_ATTN), row),
            pl.BlockSpec((None, N_HEADS, ts // ATT_TQ, VT_ROWS, ATT_TQ),
                         lambda b, i: (b, 0, i, 0, 0)),
            pl.BlockSpec((None, ts, 2 * D_MODEL), row),
        ),
        compiler_params=pltpu.CompilerParams(
            dimension_semantics=("parallel", "parallel"),
            vmem_limit_bytes=VMEM_LIMIT_BYTES),
        name="in_proj",
    )(x, w_in_b, b_gate, cos_t, sin_t)


def _s5_kernel(u_ref, lre_ref, lim_ref, lstep_ref, bre_ref, bim_ref, cre_ref, cim_ref,
               d_ref, wglu_ref, o_ref,
               are_s, aim_s, bb_s, state_s, ut_s, bu_s, yt_s):
    n_b, n_t, _ = u_ref.shape
    rows = n_b * n_t
    chunk = N_STATES // SSM_SLAB

    @pl.when(pl.program_id(0) == 0)
    def _():
        lr = lre_ref[...]
        li = lim_ref[...]
        dt = jnp.exp(lstep_ref[...])
        mag = jnp.exp(lr * dt)
        ar = mag * jnp.cos(li * dt)
        ai = mag * jnp.sin(li * dt)
        are_s[...] = jnp.broadcast_to(ar, are_s.shape)
        aim_s[...] = jnp.broadcast_to(ai, aim_s.shape)
        nr, ni = ar - 1.0, ai
        den = lr * lr + li * li
        fr = (nr * lr + ni * li) / den
        fi = (ni * lr - nr * li) / den
        for j in range(SSM_SLAB):
            frj = fr[:, j * chunk:(j + 1) * chunk]
            fij = fi[:, j * chunk:(j + 1) * chunk]
            br = bre_ref[j]
            bi = bim_ref[j]
            bb_s[j] = (frj * br - fij * bi).astype(BF16)
            bb_s[SSM_SLAB + j] = (frj * bi + fij * br).astype(BF16)
        state_s[...] = jnp.zeros(state_s.shape, F32)

    for b in range(n_b):
        for j in range(SSM_SLAB):
            ut_s[j, pl.ds(b, n_t, stride=n_b), :] = u_ref[b, :, j * LANES:(j + 1) * LANES]

    for j in range(SSM_SLAB):
        lhs = ut_s[j].astype(BF16)
        bu_s[:, j * chunk:(j + 1) * chunk] = _dot(lhs, bb_s[j])
        bu_s[:, N_STATES + j * chunk:N_STATES + (j + 1) * chunk] = _dot(lhs, bb_s[SSM_SLAB + j])

    for j in range(SSM_SLAB):
        c_re = slice(j * chunk, (j + 1) * chunk)
        c_im = slice(N_STATES + j * chunk, N_STATES + (j + 1) * chunk)
        ar = are_s[:, c_re]
        ai = aim_s[:, c_re]

        def step(t, carry, c_re=c_re, c_im=c_im, ar=ar, ai=ai):
            sr, si = carry
            r0 = pl.multiple_of(t * n_b, n_b)
            nsr = ar * sr - ai * si + bu_s[pl.ds(r0, n_b), c_re]
            nsi = ar * si + ai * sr + bu_s[pl.ds(r0, n_b), c_im]
            bu_s[pl.ds(r0, n_b), c_re] = nsr
            bu_s[pl.ds(r0, n_b), c_im] = nsi
            return nsr, nsi

        sr, si = lax.fori_loop(0, n_t, step, (state_s[0, :, c_re], state_s[1, :, c_re]),
                               unroll=8)
        state_s[0, :, c_re] = sr
        state_s[1, :, c_re] = si

    for j in range(SSM_SLAB):
        s_re = bu_s[:, j * chunk:(j + 1) * chunk].astype(BF16)
        s_im = bu_s[:, N_STATES + j * chunk:N_STATES + (j + 1) * chunk].astype(BF16)
        y = (_dot(s_re, cre_ref[j]) - _dot(s_im, cim_ref[j])
             + d_ref[:, j * LANES:(j + 1) * LANES] * ut_s[j])
        yt_s[j] = jax.nn.gelu(y)

    yg = jnp.concatenate([yt_s[j] for j in range(SSM_SLAB)], axis=-1).astype(BF16)
    ga = _dot(yg, wglu_ref[...])
    out = ga[:, :D_SSM] * jax.nn.sigmoid(ga[:, D_SSM:])
    for j in range(SSM_SLAB):
        yt_s[j] = out[:, j * LANES:(j + 1) * LANES]
    for b in range(n_b):
        for j in range(SSM_SLAB):
            o_ref[b, :, j * LANES:(j + 1) * LANES] = (
                yt_s[j, pl.ds(b, n_t, stride=n_b), :].astype(BF16))


def _s5(u, lam_re, lam_im, lstep, b_re_blk, b_im_blk, c_re_blk, c_im_blk, d_skip, w_glu_b):
    bsz, seq, _ = u.shape
    assert bsz == SUBLANES
    n_t = SSM_L
    rows = bsz * n_t
    chunk = N_STATES // SSM_SLAB
    c2 = lambda c: (0, 0)
    c3 = lambda c: (0, 0, 0)
    return pl.pallas_call(
        _s5_kernel,
        out_shape=jax.ShapeDtypeStruct((bsz, seq, D_SSM), BF16),
        grid=(seq // n_t,),
        in_specs=[
            pl.BlockSpec((bsz, n_t, D_SSM), lambda c: (0, c, 0)),
            pl.BlockSpec(lam_re.shape, c2),
            pl.BlockSpec(lam_im.shape, c2),
            pl.BlockSpec(lstep.shape, c2),
            pl.BlockSpec(b_re_blk.shape, c3),
            pl.BlockSpec(b_im_blk.shape, c3),
            pl.BlockSpec(c_re_blk.shape, c3),
            pl.BlockSpec(c_im_blk.shape, c3),
            pl.BlockSpec(d_skip.shape, c2),
            pl.BlockSpec(w_glu_b.shape, c2),
        ],
        out_specs=pl.BlockSpec((bsz, n_t, D_SSM), lambda c: (0, c, 0)),
        scratch_shapes=[
            pltpu.VMEM((bsz, N_STATES), F32),
            pltpu.VMEM((bsz, N_STATES), F32),
            pltpu.VMEM((2 * SSM_SLAB, LANES, chunk), BF16),
            pltpu.VMEM((2, bsz, N_STATES), F32),
            pltpu.VMEM((SSM_SLAB, rows, LANES), F32),
            pltpu.VMEM((rows, 2 * N_STATES), F32),
            pltpu.VMEM((SSM_SLAB, rows, LANES), F32),
        ],
        compiler_params=pltpu.CompilerParams(
            dimension_semantics=("arbitrary",),
            vmem_limit_bytes=VMEM_LIMIT_BYTES),
        name="s5",
    )(u, lam_re, lam_im, lstep, b_re_blk, b_im_blk, c_re_blk, c_im_blk, d_skip, w_glu_b)


def _sublane_allreduce(x, op):
    for shift in (4, 2, 1):
        x = op(x, pltpu.roll(x, shift, 0))
    return x


def _diffattn_kernel(q_ref, k_ref, vt_ref, lq1_ref, lk1_ref, lq2_ref, lk2_ref, gain_ref,
                     o_ref, qmt_s, m_s, acc_s, sbuf_s):
    qi = pl.program_id(1)
    tq = q_ref.shape[0]
    tk = tq
    n = 2 * tq

    for h in range(N_HEADS):
        qt = q_ref[:, h * HEAD_WIDTH:(h + 1) * HEAD_WIDTH].astype(F32).T
        d = lax.broadcasted_iota(jnp.int32, qt.shape, 0)
        qmt_s[h, :, 0:tq] = jnp.where(d < HEAD_DIM, qt, 0.0).astype(BF16)
        qmt_s[h, :, tq:n] = jnp.where(d >= HEAD_DIM, qt, 0.0).astype(BF16)
    m_s[...] = jnp.full(m_s.shape, -jnp.inf, F32)
    acc_s[...] = jnp.zeros(acc_s.shape, F32)
    acc_groups = acc_s.shape[1] // SUBLANES

    def update(blocks):
        items = [(j, masked, h) for j, masked in blocks for h in range(N_HEADS)]

        def scores(item):
            j, _, h = item
            r0 = pl.multiple_of(j * tk, tk)
            kt = k_ref[pl.ds(r0, tk), h * HEAD_WIDTH:(h + 1) * HEAD_WIDTH]
            return _dot(kt, qmt_s[h])

        sbuf_s[0] = scores(items[0])
        for idx, (j, masked, h) in enumerate(items):
            if idx + 1 < len(items):
                sbuf_s[(idx + 1) % 2] = scores(items[idx + 1])
            s = sbuf_s[idx % 2]
            if masked:
                key = lax.broadcasted_iota(jnp.int32, s.shape, 0)
                qry = lax.broadcasted_iota(jnp.int32, s.shape, 1) & (tq - 1)
                s = jnp.where(key <= qry, s, -jnp.inf)
            s3 = s.reshape(tk // SUBLANES, SUBLANES, n)
            m_prev = m_s[h]
            m_new = jnp.maximum(m_prev, _sublane_allreduce(jnp.max(s3, axis=0), jnp.maximum))
            alpha = jnp.exp2(m_prev - m_new)
            p3 = jnp.exp2(s3 - m_new[None])
            pv = _dot(vt_ref[h, j], p3.reshape(tk, n).astype(BF16))
            acc3 = acc_s[h].reshape(acc_groups, SUBLANES, n)
            acc_s[h] = (alpha[None] * acc3).reshape(acc_s.shape[1], n) + pv
            m_s[h] = m_new

    def body(jj, carry):
        update([(2 * jj, False), (2 * jj + 1, False)])
        return carry

    lax.fori_loop(0, qi // 2, body, 0)

    @pl.when(qi % 2 == 1)
    def _():
        update([(qi - 1, False), (qi, True)])

    @pl.when(qi % 2 == 0)
    def _():
        update([(qi, True)])

    lam = (jnp.exp(jnp.sum(lq1_ref[...] * lk1_ref[...], keepdims=True))
           - jnp.exp(jnp.sum(lq2_ref[...] * lk2_ref[...], keepdims=True)) + LAMBDA_INIT)
    for h in range(N_HEADS):
        num3 = acc_s[h, 0:HEAD_WIDTH, :].reshape(HEAD_WIDTH // SUBLANES, SUBLANES, n)
        den = acc_s[h, HEAD_WIDTH:HEAD_WIDTH + SUBLANES, :]
        on = (num3 / den[None]).reshape(HEAD_WIDTH, n)
        o = (on[:, 0:tq] - lam * on[:, tq:n]).T
        ms = jnp.mean(o * o, axis=-1, keepdims=True)
        o = o * lax.rsqrt(ms + RMS_EPS) * gain_ref[...] * (1.0 - LAMBDA_INIT)
        o_ref[:, h * HEAD_WIDTH:(h + 1) * HEAD_WIDTH] = o.astype(BF16)


def _diffattn(q, k, vt, lq1, lk1, lq2, lk2, gain):
    bsz, seq, _ = q.shape
    tq = ATT_TQ
    assert tq & (tq - 1) == 0
    n_kv = seq // tq
    grid = (bsz, n_kv)
    c2 = lambda b, i: (0, 0)
    return pl.pallas_call(
        _diffattn_kernel,
        out_shape=jax.ShapeDtypeStruct((bsz, seq, D_ATTN), BF16),
        grid=grid,
        in_specs=[
            pl.BlockSpec((None, tq, D_ATTN), lambda b, i: (b, i, 0)),
            pl.BlockSpec((None, seq, D_ATTN), lambda b, i: (b, 0, 0)),
            pl.BlockSpec((None, N_HEADS, n_kv, VT_ROWS, tq), lambda b, i: (b, 0, 0, 0, 0)),
            pl.BlockSpec(lq1.shape, c2),
            pl.BlockSpec(lk1.shape, c2),
            pl.BlockSpec(lq2.shape, c2),
            pl.BlockSpec(lk2.shape, c2),
            pl.BlockSpec(gain.shape, c2),
        ],
        out_specs=pl.BlockSpec((None, tq, D_ATTN), lambda b, i: (b, i, 0)),
        scratch_shapes=[
            pltpu.VMEM((N_HEADS, HEAD_WIDTH, 2 * tq), BF16),
            pltpu.VMEM((N_HEADS, SUBLANES, 2 * tq), F32),
            pltpu.VMEM((N_HEADS, VT_ROWS, 2 * tq), F32),
            pltpu.VMEM((2, tq, 2 * tq), F32),
        ],
        compiler_params=pltpu.CompilerParams(
            dimension_semantics=("parallel", "arbitrary"),
            vmem_limit_bytes=VMEM_LIMIT_BYTES),
        name="diffattn",
    )(q, k, vt, lq1, lk1, lq2, lk2, gain)


def _merge_kernel(x_ref, g_ref, ys_ref, ya_ref, wps_ref, wpa_ref, wo_ref, lng_ref, lnb_ref,
                  o_ref):
    p_ssm = _dot(ys_ref[...], wps_ref[...])
    p_att = _dot(ya_ref[...], wpa_ref[...])
    merged = (g_ref[:, 0:D_MODEL].astype(F32) * p_ssm
              + g_ref[:, D_MODEL:2 * D_MODEL].astype(F32) * p_att)
    mix = _dot(merged.astype(BF16), wo_ref[...])
    z = DEEPNORM_ALPHA * x_ref[...] + mix
    o_ref[...] = _layer_norm(z, lng_ref[...], lnb_ref[...])


def _merge(x2, gates2, ys2, ya2, wps, wpa, wo, ln_g, ln_b):
    n, _ = x2.shape
    tm = MERGE_TM
    row = lambda i: (i, 0)
    c2 = lambda i: (0, 0)
    return pl.pallas_call(
        _merge_kernel,
        out_shape=jax.ShapeDtypeStruct((n, D_MODEL), F32),
        grid=(n // tm,),
        in_specs=[
            pl.BlockSpec((tm, D_MODEL), row),
            pl.BlockSpec((tm, 2 * D_MODEL), row),
            pl.BlockSpec((tm, D_SSM), row),
            pl.BlockSpec((tm, D_ATTN), row),
            pl.BlockSpec(wps.shape, c2),
            pl.BlockSpec(wpa.shape, c2),
            pl.BlockSpec(wo.shape, c2),
            pl.BlockSpec(ln_g.shape, c2),
            pl.BlockSpec(ln_b.shape, c2),
        ],
        out_specs=pl.BlockSpec((tm, D_MODEL), row),
        compiler_params=pltpu.CompilerParams(
            dimension_semantics=("parallel",),
            vmem_limit_bytes=VMEM_LIMIT_BYTES),
        name="merge",
    )(x2, gates2, ys2, ya2, wps, wpa, wo, ln_g, ln_b)


def _ffn_kernel(x_ref, wg_ref, wu_ref, wd_ref, lng_ref, lnb_ref, o_ref):
    x = x_ref[...]
    xb = x.astype(BF16)
    ff = jnp.zeros(x.shape, F32)
    for c in range(D_FF // FFN_CHUNK):
        cols = slice(c * FFN_CHUNK, (c + 1) * FFN_CHUNK)
        a = _dot(xb, wg_ref[:, cols])
        b = _dot(xb, wu_ref[:, cols])
        h = (jax.nn.silu(a) * b).astype(BF16)
        ff = ff + _dot(h, wd_ref[cols, :])
    z = DEEPNORM_ALPHA * x + ff
    o_ref[...] = _layer_norm(z, lng_ref[...], lnb_ref[...])


def _ffn(x2, wg, wu, wd, ln_g, ln_b):
    n, _ = x2.shape
    tm = FFN_TM
    row = lambda i: (i, 0)
    c2 = lambda i: (0, 0)
    resident = functools.partial(pl.BlockSpec, index_map=c2, pipeline_mode=pl.Buffered(1))
    return pl.pallas_call(
        _ffn_kernel,
        out_shape=jax.ShapeDtypeStruct((n, D_MODEL), F32),
        grid=(n // tm,),
        in_specs=[
            pl.BlockSpec((tm, D_MODEL), row),
            resident(wg.shape),
            resident(wu.shape),
            resident(wd.shape),
            pl.BlockSpec(ln_g.shape, c2),
            pl.BlockSpec(ln_b.shape, c2),
        ],
        out_specs=pl.BlockSpec((tm, D_MODEL), row),
        compiler_params=pltpu.CompilerParams(
            dimension_semantics=("parallel",),
            vmem_limit_bytes=VMEM_LIMIT_BYTES),
        name="ffn",
    )(x2, wg, wu, wd, ln_g, ln_b)


def _rope_tables(seq):
    half = HEAD_DIM // 2
    inv_freq = ROPE_THETA ** (-jnp.arange(0, HEAD_DIM, 2, dtype=F32) / HEAD_DIM)
    ang = jnp.arange(seq, dtype=F32)[:, None] * inv_freq[None, :]
    cos, sin = jnp.cos(ang), jnp.sin(ang)
    reps = LANES // half
    cos_t = jnp.tile(cos, (1, reps))
    sign = jnp.tile(jnp.concatenate([-jnp.ones((half,), F32), jnp.ones((half,), F32)]),
                    LANES // HEAD_DIM)
    sin_t = jnp.tile(sin, (1, reps)) * sign[None, :]
    return cos_t, sin_t


def _block_diag_in(b):
    gps = N_SSM_GROUPS // SSM_SLAB
    bt = jnp.swapaxes(b, 1, 2).reshape(SSM_SLAB, gps, SSM_GROUP, SSM_STATE)
    eye = jnp.eye(gps, dtype=b.dtype)
    blk = bt[:, :, :, None, :] * eye[None, :, None, :, None]
    return blk.reshape(SSM_SLAB, gps * SSM_GROUP, gps * SSM_STATE)


def _block_diag_out(c):
    gps = N_SSM_GROUPS // SSM_SLAB
    ct = jnp.swapaxes(c, 1, 2).reshape(SSM_SLAB, gps, SSM_STATE, SSM_GROUP)
    eye = jnp.eye(gps, dtype=c.dtype)
    blk = ct[:, :, :, None, :] * eye[None, :, None, :, None]
    return blk.reshape(SSM_SLAB, gps * SSM_STATE, gps * SSM_GROUP)


def kernel(x, w_in, b_gate, ssm_lambda_re, ssm_lambda_im, ssm_log_step, ssm_b_re, ssm_b_im, ssm_c_re, ssm_c_im, ssm_d, w_glu, lambda_q1, lambda_k1, lambda_q2, lambda_k2, subln_gain, w_proj_ssm, w_proj_attn, w_out, ln1_g, ln1_b, w_ffn_gate, w_ffn_up, w_ffn_down, ln2_g, ln2_b):
    bsz, seq, _ = x.shape
    cos_t, sin_t = _rope_tables(seq)
    for l in range(DEPTH):
        u, q, k, vt, gates = _in_proj(x, w_in[l].astype(BF16), b_gate[l][None, :], cos_t, sin_t)

        lstep = jnp.broadcast_to(ssm_log_step[l][:, None], (N_SSM_GROUPS, SSM_STATE))
        y_ssm = _s5(
            u,
            ssm_lambda_re[l].reshape(1, N_STATES),
            ssm_lambda_im[l].reshape(1, N_STATES),
            lstep.reshape(1, N_STATES),
            _block_diag_in(ssm_b_re[l]), _block_diag_in(ssm_b_im[l]),
            _block_diag_out(ssm_c_re[l]).astype(BF16), _block_diag_out(ssm_c_im[l]).astype(BF16),
            ssm_d[l].reshape(1, D_SSM),
            w_glu[l].astype(BF16))

        y_att = _diffattn(q, k, vt,lambda_q1[l][None, :], lambda_k1[l][None, :],
                          lambda_q2[l][None, :], lambda_k2[l][None, :], subln_gain[l][None, :])

        n = bsz * seq
        x1 = _merge(x.reshape(n, D_MODEL), gates.reshape(n, 2 * D_MODEL),
                    y_ssm.reshape(n, D_SSM), y_att.reshape(n, D_ATTN),
                    w_proj_ssm[l].astype(BF16), w_proj_attn[l].astype(BF16),
                    w_out[l].astype(BF16), ln1_g[l][None, :], ln1_b[l][None, :])
        x2 = _ffn(x1, w_ffn_gate[l].astype(BF16), w_ffn_up[l].astype(BF16),
                  w_ffn_down[l].astype(BF16), ln2_g[l][None, :], ln2_b[l][None, :])
        x = x2.reshape(bsz, seq, D_MODEL)
    return x
```

```python
import functools
import math

import jax
import jax.numpy as jnp
from jax import lax
from jax.experimental import pallas as pl
from jax.experimental.pallas import tpu as pltpu

F32 = jnp.float32
BF16 = jnp.bfloat16

D_MODEL = 1024
D_SSM = 512
SSM_GROUP = 16
N_SSM_GROUPS = 32
SSM_STATE = 64
N_STATES = N_SSM_GROUPS * SSM_STATE
D_ATTN = 512
HEAD_DIM = 64
HEAD_WIDTH = 2 * HEAD_DIM
N_HEADS = 4
ROPE_THETA = 10000.0
D_FF = 2816
DEPTH = 1
DEEPNORM_ALPHA = (2.0 * DEPTH) ** 0.25
LN_EPS = 1e-5
RMS_EPS = 1e-5
LAMBDA_INIT = 0.8 - 0.6 * math.exp(-0.3 * 0)

LANES = 128
SUBLANES = 8
VMEM_LIMIT_BYTES = 56 * 1024 * 1024

IN_TS = 512
SSM_L = 64
SSM_SLAB = 4
ATT_TQ = 256
VT_ROWS = HEAD_WIDTH + 16
MERGE_TM = 1024
FFN_TM = 1024
SUB_ROWS = 512
FFN_CHUNK = 1408


def _dot(a, b):
    return jnp.dot(a, b, preferred_element_type=F32)


def _layer_norm(z, g, b):
    mu = jnp.mean(z, axis=-1, keepdims=True)
    zc = z - mu
    var = jnp.mean(zc * zc, axis=-1, keepdims=True)
    return zc * lax.rsqrt(var + LN_EPS) * g + b


def _in_proj_kernel(x_ref, w_ref, bg_ref, cos_ref, sin_ref,
                    u_ref, q_ref, k_ref, vt_ref, g_ref):
    xb = x_ref[...].astype(BF16)
    u_ref[...] = _dot(xb, w_ref[:, 0:D_SSM])
    cos = cos_ref[...]
    sin = sin_ref[...]
    lane = lax.broadcasted_iota(jnp.int32, cos.shape, 1)
    low_half = (lane & (HEAD_DIM // 2)) == 0
    for out_ref, off, scale in ((q_ref, D_SSM, HEAD_DIM ** -0.5 * math.log2(math.e)),
                                (k_ref, D_SSM + D_ATTN, 1.0)):
        h = _dot(xb, w_ref[:, off:off + D_ATTN])
        for hd in range(N_HEADS):
            slab = h[:, hd * LANES:(hd + 1) * LANES]
            partner = jnp.where(low_half,
                                pltpu.roll(slab, LANES - HEAD_DIM // 2, 1),
                                pltpu.roll(slab, HEAD_DIM // 2, 1))
            r = slab * cos + partner * sin
            if scale != 1.0:
                r = r * scale
            out_ref[:, hd * LANES:(hd + 1) * LANES] = r.astype(BF16)
    off_v = D_SSM + 2 * D_ATTN
    v = _dot(xb, w_ref[:, off_v:off_v + D_ATTN])
    ones = jnp.ones((VT_ROWS - HEAD_WIDTH, ATT_TQ), BF16)
    for hd in range(N_HEADS):
        for r in range(vt_ref.shape[1]):
            tile = v[r * ATT_TQ:(r + 1) * ATT_TQ, hd * LANES:(hd + 1) * LANES]
            vt_ref[hd, r, 0:HEAD_WIDTH, :] = tile.T.astype(BF16)
            vt_ref[hd, r, HEAD_WIDTH:VT_ROWS, :] = ones
    off_g = off_v + D_ATTN
    g = _dot(xb, w_ref[:, off_g:off_g + 2 * D_MODEL]) + bg_ref[...]
    g_ref[...] = jax.nn.sigmoid(g).astype(BF16)


def _in_proj(x, w_in_b, b_gate, cos_t, sin_t):
    bsz, seq, _ = x.shape
    ts = IN_TS
    grid = (bsz, seq // ts)
    row = lambda b, i: (b, i, 0)
    const2 = lambda b, i: (0, 0)
    out_shape = (
        jax.ShapeDtypeStruct((bsz, seq, D_SSM), F32),
        jax.ShapeDtypeStruct((bsz, seq, D_ATTN), BF16),
        jax.ShapeDtypeStruct((bsz, seq, D_ATTN), BF16),
        jax.ShapeDtypeStruct((bsz, N_HEADS, seq // ATT_TQ, VT_ROWS, ATT_TQ), BF16),
        jax.ShapeDtypeStruct((bsz, seq, 2 * D_MODEL), BF16),
    )
    return pl.pallas_call(
        _in_proj_kernel,
        out_shape=out_shape,
        grid=grid,
        in_specs=[
            pl.BlockSpec((None, ts, D_MODEL), row),
            pl.BlockSpec(w_in_b.shape, const2),
            pl.BlockSpec(b_gate.shape, const2),
            pl.BlockSpec((ts, LANES), lambda b, i: (i, 0)),
            pl.BlockSpec((ts, LANES), lambda b, i: (i, 0)),
        ],
        out_specs=(
            pl.BlockSpec((None, ts, D_SSM), row),
            pl.BlockSpec((None, ts, D_ATTN), row),
            pl.BlockSpec((None, ts, D_ATTN), row),
            pl.BlockSpec((None, N_HEADS, ts // ATT_TQ, VT_ROWS, ATT_TQ),
                         lambda b, i: (b, 0, i, 0, 0)),
            pl.BlockSpec((None, ts, 2 * D_MODEL), row),
        ),
        compiler_params=pltpu.CompilerParams(
            dimension_semantics=("parallel", "parallel"),
            vmem_limit_bytes=VMEM_LIMIT_BYTES),
        name="in_proj",
    )(x, w_in_b, b_gate, cos_t, sin_t)


def _s5_kernel(u_ref, lre_ref, lim_ref, lstep_ref, bre_ref, bim_ref, cre_ref, cim_ref,
               d_ref, wglu_ref, o_ref,
               are_s, aim_s, bb_s, state_s, ut_s, bu_s, yt_s):
    n_b, n_t, _ = u_ref.shape
    rows = n_b * n_t
    chunk = N_STATES // SSM_SLAB

    @pl.when(pl.program_id(0) == 0)
    def _():
        lr = lre_ref[...]
        li = lim_ref[...]
        dt = jnp.exp(lstep_ref[...])
        mag = jnp.exp(lr * dt)
        ar = mag * jnp.cos(li * dt)
        ai = mag * jnp.sin(li * dt)
        are_s[...] = jnp.broadcast_to(ar, are_s.shape)
        aim_s[...] = jnp.broadcast_to(ai, aim_s.shape)
        nr, ni = ar - 1.0, ai
        den = lr * lr + li * li
        fr = (nr * lr + ni * li) / den
        fi = (ni * lr - nr * li) / den
        for j in range(SSM_SLAB):
            frj = fr[:, j * chunk:(j + 1) * chunk]
            fij = fi[:, j * chunk:(j + 1) * chunk]
            br = bre_ref[j]
            bi = bim_ref[j]
            bb_s[j] = (frj * br - fij * bi).astype(BF16)
            bb_s[SSM_SLAB + j] = (frj * bi + fij * br).astype(BF16)
        state_s[...] = jnp.zeros(state_s.shape, F32)

    for b in range(n_b):
        for j in range(SSM_SLAB):
            ut_s[j, pl.ds(b, n_t, stride=n_b), :] = u_ref[b, :, j * LANES:(j + 1) * LANES]

    for j in range(SSM_SLAB):
        lhs = ut_s[j].astype(BF16)
        bu_s[:, j * chunk:(j + 1) * chunk] = _dot(lhs, bb_s[j])
        bu_s[:, N_STATES + j * chunk:N_STATES + (j + 1) * chunk] = _dot(lhs, bb_s[SSM_SLAB + j])

    for j in range(SSM_SLAB):
        c_re = slice(j * chunk, (j + 1) * chunk)
        c_im = slice(N_STATES + j * chunk, N_STATES + (j + 1) * chunk)
        ar = are_s[:, c_re]
        ai = aim_s[:, c_re]
        sr = state_s[0, :, c_re]
        si = state_s[1, :, c_re]
        for t in range(n_t):
            rows_t = slice(t * n_b, (t + 1) * n_b)
            sr, si = (ar * sr - ai * si + bu_s[rows_t, c_re],
                      ar * si + ai * sr + bu_s[rows_t, c_im])
            bu_s[rows_t, c_re] = sr
            bu_s[rows_t, c_im] = si
        state_s[0, :, c_re] = sr
        state_s[1, :, c_re] = si

        s_re = bu_s[:, c_re].astype(BF16)
        s_im = bu_s[:, c_im].astype(BF16)
        y = (_dot(s_re, cre_ref[j]) - _dot(s_im, cim_ref[j])
             + d_ref[:, j * LANES:(j + 1) * LANES] * ut_s[j])
        yt_s[j] = jax.nn.gelu(y)

    yg = jnp.concatenate([yt_s[j] for j in range(SSM_SLAB)], axis=-1).astype(BF16)
    ga = _dot(yg, wglu_ref[...])
    out = ga[:, :D_SSM] * jax.nn.sigmoid(ga[:, D_SSM:])
    for j in range(SSM_SLAB):
        yt_s[j] = out[:, j * LANES:(j + 1) * LANES]
    for b in range(n_b):
        for j in range(SSM_SLAB):
            o_ref[b, :, j * LANES:(j + 1) * LANES] = (
                yt_s[j, pl.ds(b, n_t, stride=n_b), :].astype(BF16))


def _s5(u, lam_re, lam_im, lstep, b_re_blk, b_im_blk, c_re_blk, c_im_blk, d_skip, w_glu_b):
    bsz, seq, _ = u.shape
    assert bsz == SUBLANES
    n_t = SSM_L
    rows = bsz * n_t
    chunk = N_STATES // SSM_SLAB
    c2 = lambda c: (0, 0)
    c3 = lambda c: (0, 0, 0)
    return pl.pallas_call(
        _s5_kernel,
        out_shape=jax.ShapeDtypeStruct((bsz, seq, D_SSM), BF16),
        grid=(seq // n_t,),
        in_specs=[
            pl.BlockSpec((bsz, n_t, D_SSM), lambda c: (0, c, 0)),
            pl.BlockSpec(lam_re.shape, c2),
            pl.BlockSpec(lam_im.shape, c2),
            pl.BlockSpec(lstep.shape, c2),
            pl.BlockSpec(b_re_blk.shape, c3),
            pl.BlockSpec(b_im_blk.shape, c3),
            pl.BlockSpec(c_re_blk.shape, c3),
            pl.BlockSpec(c_im_blk.shape, c3),
            pl.BlockSpec(d_skip.shape, c2),
            pl.BlockSpec(w_glu_b.shape, c2),
        ],
        out_specs=pl.BlockSpec((bsz, n_t, D_SSM), lambda c: (0, c, 0)),
        scratch_shapes=[
            pltpu.VMEM((bsz, N_STATES), F32),
            pltpu.VMEM((bsz, N_STATES), F32),
            pltpu.VMEM((2 * SSM_SLAB, LANES, chunk), BF16),
            pltpu.VMEM((2, bsz, N_STATES), F32),
            pltpu.VMEM((SSM_SLAB, rows, LANES), F32),
            pltpu.VMEM((rows, 2 * N_STATES), F32),
            pltpu.VMEM((SSM_SLAB, rows, LANES), F32),
        ],
        compiler_params=pltpu.CompilerParams(
            dimension_semantics=("arbitrary",),
            vmem_limit_bytes=VMEM_LIMIT_BYTES),
        name="s5",
    )(u, lam_re, lam_im, lstep, b_re_blk, b_im_blk, c_re_blk, c_im_blk, d_skip, w_glu_b)


def _sublane_allreduce(x, op):
    for shift in (4, 2, 1):
        x = op(x, pltpu.roll(x, shift, 0))
    return x


def _diffattn_kernel(q_ref, k_ref, vt_ref, lq1_ref, lk1_ref, lq2_ref, lk2_ref, gain_ref,
                     o_ref, qmt_s, m_s, acc_s, sbuf_s):
    qi = pl.program_id(1)
    tq = q_ref.shape[0]
    tk = tq
    n = 2 * tq

    for h in range(N_HEADS):
        qt = q_ref[:, h * HEAD_WIDTH:(h + 1) * HEAD_WIDTH].astype(F32).T
        d = lax.broadcasted_iota(jnp.int32, qt.shape, 0)
        qmt_s[h, :, 0:tq] = jnp.where(d < HEAD_DIM, qt, 0.0).astype(BF16)
        qmt_s[h, :, tq:n] = jnp.where(d >= HEAD_DIM, qt, 0.0).astype(BF16)
    m_s[...] = jnp.full(m_s.shape, -jnp.inf, F32)
    acc_s[...] = jnp.zeros(acc_s.shape, F32)
    acc_groups = acc_s.shape[1] // SUBLANES

    def update(blocks):
        items = [(j, masked, h) for j, masked in blocks for h in range(N_HEADS)]

        def scores(item):
            j, _, h = item
            r0 = pl.multiple_of(j * tk, tk)
            kt = k_ref[pl.ds(r0, tk), h * HEAD_WIDTH:(h + 1) * HEAD_WIDTH]
            return _dot(kt, qmt_s[h])

        sbuf_s[0] = scores(items[0])
        for idx, (j, masked, h) in enumerate(items):
            if idx + 1 < len(items):
                sbuf_s[(idx + 1) % 2] = scores(items[idx + 1])
            s = sbuf_s[idx % 2]
            if masked:
                key = lax.broadcasted_iota(jnp.int32, s.shape, 0)
                qry = lax.broadcasted_iota(jnp.int32, s.shape, 1) & (tq - 1)
                s = jnp.where(key <= qry, s, -jnp.inf)
            s3 = s.reshape(tk // SUBLANES, SUBLANES, n)
            m_prev = m_s[h]
            m_new = jnp.maximum(m_prev, _sublane_allreduce(jnp.max(s3, axis=0), jnp.maximum))
            alpha = jnp.exp2(m_prev - m_new)
            p3 = jnp.exp2(s3 - m_new[None])
            pv = _dot(vt_ref[h, j], p3.reshape(tk, n).astype(BF16))
            acc3 = acc_s[h].reshape(acc_groups, SUBLANES, n)
            acc_s[h] = (alpha[None] * acc3).reshape(acc_s.shape[1], n) + pv
            m_s[h] = m_new

    def body(jj, carry):
        update([(2 * jj, False), (2 * jj + 1, False)])
        return carry

    lax.fori_loop(0, qi // 2, body, 0)

    @pl.when(qi % 2 == 1)
    def _():
        update([(qi - 1, False), (qi, True)])

    @pl.when(qi % 2 == 0)
    def _():
        update([(qi, True)])

    lam = (jnp.exp(jnp.sum(lq1_ref[...] * lk1_ref[...], keepdims=True))
           - jnp.exp(jnp.sum(lq2_ref[...] * lk2_ref[...], keepdims=True)) + LAMBDA_INIT)
    for h in range(N_HEADS):
        num3 = acc_s[h, 0:HEAD_WIDTH, :].reshape(HEAD_WIDTH // SUBLANES, SUBLANES, n)
        den = acc_s[h, HEAD_WIDTH:HEAD_WIDTH + SUBLANES, :]
        on = (num3 / den[None]).reshape(HEAD_WIDTH, n)
        o = (on[:, 0:tq] - lam * on[:, tq:n]).T
        ms = jnp.mean(o * o, axis=-1, keepdims=True)
        o = o * lax.rsqrt(ms + RMS_EPS) * gain_ref[...] * (1.0 - LAMBDA_INIT)
        o_ref[:, h * HEAD_WIDTH:(h + 1) * HEAD_WIDTH] = o.astype(BF16)


def _diffattn(q, k, vt, lq1, lk1, lq2, lk2, gain):
    bsz, seq, _ = q.shape
    tq = ATT_TQ
    assert tq & (tq - 1) == 0
    n_kv = seq // tq
    grid = (bsz, n_kv)
    c2 = lambda b, i: (0, 0)
    return pl.pallas_call(
        _diffattn_kernel,
        out_shape=jax.ShapeDtypeStruct((bsz, seq, D_ATTN), BF16),
        grid=grid,
        in_specs=[
            pl.BlockSpec((None, tq, D_ATTN), lambda b, i: (b, i, 0)),
            pl.BlockSpec((None, seq, D_ATTN), lambda b, i: (b, 0, 0)),
            pl.BlockSpec((None, N_HEADS, n_kv, VT_ROWS, tq), lambda b, i: (b, 0, 0, 0, 0)),
            pl.BlockSpec(lq1.shape, c2),
            pl.BlockSpec(lk1.shape, c2),
            pl.BlockSpec(lq2.shape, c2),
            pl.BlockSpec(lk2.shape, c2),
            pl.BlockSpec(gain.shape, c2),
        ],
        out_specs=pl.BlockSpec((None, tq, D_ATTN), lambda b, i: (b, i, 0)),
        scratch_shapes=[
            pltpu.VMEM((N_HEADS, HEAD_WIDTH, 2 * tq), BF16),
            pltpu.VMEM((N_HEADS, SUBLANES, 2 * tq), F32),
            pltpu.VMEM((N_HEADS, VT_ROWS, 2 * tq), F32),
            pltpu.VMEM((2, tq, 2 * tq), F32),
        ],
        compiler_params=pltpu.CompilerParams(
            dimension_semantics=("parallel", "arbitrary"),
            vmem_limit_bytes=VMEM_LIMIT_BYTES),
        name="diffattn",
    )(q, k, vt, lq1, lk1, lq2, lk2, gain)


def _merge_kernel(x_ref, g_ref, ys_ref, ya_ref, wps_ref, wpa_ref, wo_ref, lng_ref, lnb_ref,
                  o_ref):
    for r in range(x_ref.shape[0] // SUB_ROWS):
        rows = slice(r * SUB_ROWS, (r + 1) * SUB_ROWS)
        p_ssm = _dot(ys_ref[rows, :], wps_ref[...])
        p_att = _dot(ya_ref[rows, :], wpa_ref[...])
        merged = (g_ref[rows, 0:D_MODEL].astype(F32) * p_ssm
                  + g_ref[rows, D_MODEL:2 * D_MODEL].astype(F32) * p_att)
        mix = _dot(merged.astype(BF16), wo_ref[...])
        z = DEEPNORM_ALPHA * x_ref[rows, :] + mix
        o_ref[rows, :] = _layer_norm(z, lng_ref[...], lnb_ref[...])


def _merge(x2, gates2, ys2, ya2, wps, wpa, wo, ln_g, ln_b):
    n, _ = x2.shape
    tm = MERGE_TM
    row = lambda i: (i, 0)
    c2 = lambda i: (0, 0)
    return pl.pallas_call(
        _merge_kernel,
        out_shape=jax.ShapeDtypeStruct((n, D_MODEL), F32),
        grid=(n // tm,),
        in_specs=[
            pl.BlockSpec((tm, D_MODEL), row),
            pl.BlockSpec((tm, 2 * D_MODEL), row),
            pl.BlockSpec((tm, D_SSM), row),
            pl.BlockSpec((tm, D_ATTN), row),
            pl.BlockSpec(wps.shape, c2),
            pl.BlockSpec(wpa.shape, c2),
            pl.BlockSpec(wo.shape, c2),
            pl.BlockSpec(ln_g.shape, c2),
            pl.BlockSpec(ln_b.shape, c2),
        ],
        out_specs=pl.BlockSpec((tm, D_MODEL), row),
        compiler_params=pltpu.CompilerParams(
            dimension_semantics=("parallel",),
            vmem_limit_bytes=VMEM_LIMIT_BYTES),
        name="merge",
    )(x2, gates2, ys2, ya2, wps, wpa, wo, ln_g, ln_b)


def _ffn_kernel(x_ref, wg_ref, wu_ref, wd_ref, lng_ref, lnb_ref, o_ref):
    for r in range(x_ref.shape[0] // SUB_ROWS):
        rows = slice(r * SUB_ROWS, (r + 1) * SUB_ROWS)
        x = x_ref[rows, :]
        xb = x.astype(BF16)
        ff = jnp.zeros(x.shape, F32)
        for c in range(D_FF // FFN_CHUNK):
            cols = slice(c * FFN_CHUNK, (c + 1) * FFN_CHUNK)
            a = _dot(xb, wg_ref[:, cols])
            b = _dot(xb, wu_ref[:, cols])
            h = (jax.nn.silu(a) * b).astype(BF16)
            ff = ff + _dot(h, wd_ref[cols, :])
        z = DEEPNORM_ALPHA * x + ff
        o_ref[rows, :] = _layer_norm(z, lng_ref[...], lnb_ref[...])


def _ffn(x2, wg, wu, wd, ln_g, ln_b):
    n, _ = x2.shape
    tm = FFN_TM
    row = lambda i: (i, 0)
    c2 = lambda i: (0, 0)
    resident = functools.partial(pl.BlockSpec, index_map=c2, pipeline_mode=pl.Buffered(1))
    return pl.pallas_call(
        _ffn_kernel,
        out_shape=jax.ShapeDtypeStruct((n, D_MODEL), F32),
        grid=(n // tm,),
        in_specs=[
            pl.BlockSpec((tm, D_MODEL), row),
            resident(wg.shape),
            resident(wu.shape),
            resident(wd.shape),
            pl.BlockSpec(ln_g.shape, c2),
            pl.BlockSpec(ln_b.shape, c2),
        ],
        out_specs=pl.BlockSpec((tm, D_MODEL), row),
        compiler_params=pltpu.CompilerParams(
            dimension_semantics=("parallel",),
            vmem_limit_bytes=VMEM_LIMIT_BYTES),
        name="ffn",
    )(x2, wg, wu, wd, ln_g, ln_b)


def _rope_tables(seq):
    half = HEAD_DIM // 2
    inv_freq = ROPE_THETA ** (-jnp.arange(0, HEAD_DIM, 2, dtype=F32) / HEAD_DIM)
    ang = jnp.arange(seq, dtype=F32)[:, None] * inv_freq[None, :]
    cos, sin = jnp.cos(ang), jnp.sin(ang)
    reps = LANES // half
    cos_t = jnp.tile(cos, (1, reps))
    sign = jnp.tile(jnp.concatenate([-jnp.ones((half,), F32), jnp.ones((half,), F32)]),
                    LANES // HEAD_DIM)
    sin_t = jnp.tile(sin, (1, reps)) * sign[None, :]
    return cos_t, sin_t


def _block_diag_in(b):
    gps = N_SSM_GROUPS // SSM_SLAB
    bt = jnp.swapaxes(b, 1, 2).reshape(SSM_SLAB, gps, SSM_GROUP, SSM_STATE)
    eye = jnp.eye(gps, dtype=b.dtype)
    blk = bt[:, :, :, None, :] * eye[None, :, None, :, None]
    return blk.reshape(SSM_SLAB, gps * SSM_GROUP, gps * SSM_STATE)


def _block_diag_out(c):
    gps = N_SSM_GROUPS // SSM_SLAB
    ct = jnp.swapaxes(c, 1, 2).reshape(SSM_SLAB, gps, SSM_STATE, SSM_GROUP)
    eye = jnp.eye(gps, dtype=c.dtype)
    blk = ct[:, :, :, None, :] * eye[None, :, None, :, None]
    return blk.reshape(SSM_SLAB, gps * SSM_STATE, gps * SSM_GROUP)


def kernel(x, w_in, b_gate, ssm_lambda_re, ssm_lambda_im, ssm_log_step, ssm_b_re, ssm_b_im, ssm_c_re, ssm_c_im, ssm_d, w_glu, lambda_q1, lambda_k1, lambda_q2, lambda_k2, subln_gain, w_proj_ssm, w_proj_attn, w_out, ln1_g, ln1_b, w_ffn_gate, w_ffn_up, w_ffn_down, ln2_g, ln2_b):
    bsz, seq, _ = x.shape
    cos_t, sin_t = _rope_tables(seq)
    for l in range(DEPTH):
        u, q, k, vt, gates = _in_proj(x, w_in[l].astype(BF16), b_gate[l][None, :], cos_t, sin_t)

        lstep = jnp.broadcast_to(ssm_log_step[l][:, None], (N_SSM_GROUPS, SSM_STATE))
        y_ssm = _s5(
            u,
            ssm_lambda_re[l].reshape(1, N_STATES),
            ssm_lambda_im[l].reshape(1, N_STATES),
            lstep.reshape(1, N_STATES),
            _block_diag_in(ssm_b_re[l]), _block_diag_in(ssm_b_im[l]),
            _block_diag_out(ssm_c_re[l]).astype(BF16), _block_diag_out(ssm_c_im[l]).astype(BF16),
            ssm_d[l].reshape(1, D_SSM),
            w_glu[l].astype(BF16))

        y_att = _diffattn(q, k, vt,lambda_q1[l][None, :], lambda_k1[l][None, :],
                          lambda_q2[l][None, :], lambda_k2[l][None, :], subln_gain[l][None, :])

        n = bsz * seq
        x1 = _merge(x.reshape(n, D_MODEL), gates.reshape(n, 2 * D_MODEL),
                    y_ssm.reshape(n, D_SSM), y_att.reshape(n, D_ATTN),
                    w_proj_ssm[l].astype(BF16), w_proj_attn[l].astype(BF16),
                    w_out[l].astype(BF16), ln1_g[l][None, :], ln1_b[l][None, :])
        x2 = _ffn(x1, w_ffn_gate[l].astype(BF16), w_ffn_up[l].astype(BF16),
                  w_ffn_down[l].astype(BF16), ln2_g[l][None, :], ln2_b[l][None, :])
        x = x2.reshape(bsz, seq, D_MODEL)
    return x
```

```python
import functools
import math

import jax
import jax.numpy as jnp
from jax import lax
from jax.experimental import pallas as pl
from jax.experimental.pallas import tpu as pltpu

F32 = jnp.float32
BF16 = jnp.bfloat16

D_MODEL = 1024
D_SSM = 512
SSM_GROUP = 16
N_SSM_GROUPS = 32
SSM_STATE = 64
N_STATES = N_SSM_GROUPS * SSM_STATE
D_ATTN = 512
HEAD_DIM = 64
HEAD_WIDTH = 2 * HEAD_DIM
N_HEADS = 4
ROPE_THETA = 10000.0
D_FF = 2816
DEPTH = 1
DEEPNORM_ALPHA = (2.0 * DEPTH) ** 0.25
LN_EPS = 1e-5
RMS_EPS = 1e-5
LAMBDA_INIT = 0.8 - 0.6 * math.exp(-0.3 * 0)

LANES = 128
SUBLANES = 8
VMEM_LIMIT_BYTES = 56 * 1024 * 1024

IN_TS = 512
SSM_L = 128
SSM_SLAB = 4
ATT_TQ = 256
VT_ROWS = HEAD_WIDTH + 16
MERGE_TM = 1024
FFN_TM = 1024
SUB_ROWS = 512
FFN_CHUNK = 1408


def _dot(a, b):
    return jnp.dot(a, b, preferred_element_type=F32)


def _layer_norm(z, g, b):
    mu = jnp.mean(z, axis=-1, keepdims=True)
    zc = z - mu
    var = jnp.mean(zc * zc, axis=-1, keepdims=True)
    return zc * lax.rsqrt(var + LN_EPS) * g + b


def _in_proj_kernel(x_ref, w_ref, bg_ref, cos_ref, sin_ref,
                    u_ref, q_ref, k_ref, vt_ref, g_ref):
    xb = x_ref[...].astype(BF16)
    u_ref[...] = _dot(xb, w_ref[:, 0:D_SSM])
    cos = cos_ref[...]
    sin = sin_ref[...]
    lane = lax.broadcasted_iota(jnp.int32, cos.shape, 1)
    low_half = (lane & (HEAD_DIM // 2)) == 0
    for out_ref, off, scale in ((q_ref, D_SSM, HEAD_DIM ** -0.5 * math.log2(math.e)),
                                (k_ref, D_SSM + D_ATTN, 1.0)):
        h = _dot(xb, w_ref[:, off:off + D_ATTN])
        for hd in range(N_HEADS):
            slab = h[:, hd * LANES:(hd + 1) * LANES]
            partner = jnp.where(low_half,
                                pltpu.roll(slab, LANES - HEAD_DIM // 2, 1),
                                pltpu.roll(slab, HEAD_DIM // 2, 1))
            r = slab * cos + partner * sin
            if scale != 1.0:
                r = r * scale
            out_ref[:, hd * LANES:(hd + 1) * LANES] = r.astype(BF16)
    off_v = D_SSM + 2 * D_ATTN
    v = _dot(xb, w_ref[:, off_v:off_v + D_ATTN])
    ones = jnp.ones((VT_ROWS - HEAD_WIDTH, ATT_TQ), BF16)
    for hd in range(N_HEADS):
        for r in range(vt_ref.shape[1]):
            tile = v[r * ATT_TQ:(r + 1) * ATT_TQ, hd * LANES:(hd + 1) * LANES]
            vt_ref[hd, r, 0:HEAD_WIDTH, :] = tile.T.astype(BF16)
            vt_ref[hd, r, HEAD_WIDTH:VT_ROWS, :] = ones
    off_g = off_v + D_ATTN
    g = _dot(xb, w_ref[:, off_g:off_g + 2 * D_MODEL]) + bg_ref[...]
    g_ref[...] = jax.nn.sigmoid(g).astype(BF16)


def _in_proj(x, w_in_b, b_gate, cos_t, sin_t):
    bsz, seq, _ = x.shape
    ts = IN_TS
    grid = (bsz, seq // ts)
    row = lambda b, i: (b, i, 0)
    const2 = lambda b, i: (0, 0)
    out_shape = (
        jax.ShapeDtypeStruct((bsz, seq, D_SSM), F32),
        jax.ShapeDtypeStruct((bsz, seq, D_ATTN), BF16),
        jax.ShapeDtypeStruct((bsz, seq, D_ATTN), BF16),
        jax.ShapeDtypeStruct((bsz, N_HEADS, seq // ATT_TQ, VT_ROWS, ATT_TQ), BF16),
        jax.ShapeDtypeStruct((bsz, seq, 2 * D_MODEL), BF16),
    )
    return pl.pallas_call(
        _in_proj_kernel,
        out_shape=out_shape,
        grid=grid,
        in_specs=[
            pl.BlockSpec((None, ts, D_MODEL), row),
            pl.BlockSpec(w_in_b.shape, const2),
            pl.BlockSpec(b_gate.shape, const2),
            pl.BlockSpec((ts, LANES), lambda b, i: (i, 0)),
            pl.BlockSpec((ts, LANES), lambda b, i: (i, 0)),
        ],
        out_specs=(
            pl.BlockSpec((None, ts, D_SSM), row),
            pl.BlockSpec((None, ts, D_ATTN), row),
            pl.BlockSpec((None, ts, D_ATTN), row),
            pl.BlockSpec((None, N_HEADS, ts // ATT_TQ, VT_ROWS, ATT_TQ),
                         lambda b, i: (b, 0, i, 0, 0)),
            pl.BlockSpec((None, ts, 2 * D_MODEL), row),
        ),
        compiler_params=pltpu.CompilerParams(
            dimension_semantics=("parallel", "parallel"),
            vmem_limit_bytes=VMEM_LIMIT_BYTES),
        name="in_proj",
    )(x, w_in_b, b_gate, cos_t, sin_t)


def _s5_kernel(u_ref, lre_ref, lim_ref, lstep_ref, bre_ref, bim_ref, cre_ref, cim_ref,
               d_ref, wglu_ref, o_ref,
               are_s, aim_s, bb_s, state_s, ut_s, bu_s, yt_s):
    n_b, n_t, _ = u_ref.shape
    rows = n_b * n_t
    chunk = N_STATES // SSM_SLAB

    @pl.when(pl.program_id(0) == 0)
    def _():
        lr = lre_ref[...]
        li = lim_ref[...]
        dt = jnp.exp(lstep_ref[...])
        mag = jnp.exp(lr * dt)
        ar = mag * jnp.cos(li * dt)
        ai = mag * jnp.sin(li * dt)
        are_s[...] = jnp.broadcast_to(ar, are_s.shape)
        aim_s[...] = jnp.broadcast_to(ai, aim_s.shape)
        nr, ni = ar - 1.0, ai
        den = lr * lr + li * li
        fr = (nr * lr + ni * li) / den
        fi = (ni * lr - nr * li) / den
        for j in range(SSM_SLAB):
            frj = fr[:, j * chunk:(j + 1) * chunk]
            fij = fi[:, j * chunk:(j + 1) * chunk]
            br = bre_ref[j]
            bi = bim_ref[j]
            bb_s[j] = (frj * br - fij * bi).astype(BF16)
            bb_s[SSM_SLAB + j] = (frj * bi + fij * br).astype(BF16)
        state_s[...] = jnp.zeros(state_s.shape, F32)

    for b in range(n_b):
        for j in range(SSM_SLAB):
            ut_s[j, pl.ds(b, n_t, stride=n_b), :] = u_ref[b, :, j * LANES:(j + 1) * LANES]

    for j in range(SSM_SLAB):
        lhs = ut_s[j].astype(BF16)
        bu_s[:, j * chunk:(j + 1) * chunk] = _dot(lhs, bb_s[j])
        bu_s[:, N_STATES + j * chunk:N_STATES + (j + 1) * chunk] = _dot(lhs, bb_s[SSM_SLAB + j])

    for j in range(SSM_SLAB):
        c_re = slice(j * chunk, (j + 1) * chunk)
        c_im = slice(N_STATES + j * chunk, N_STATES + (j + 1) * chunk)
        ar = are_s[:, c_re]
        ai = aim_s[:, c_re]
        sr = state_s[0, :, c_re]
        si = state_s[1, :, c_re]
        for t in range(n_t):
            rows_t = slice(t * n_b, (t + 1) * n_b)
            sr, si = (ar * sr - ai * si + bu_s[rows_t, c_re],
                      ar * si + ai * sr + bu_s[rows_t, c_im])
            bu_s[rows_t, c_re] = sr
            bu_s[rows_t, c_im] = si
        state_s[0, :, c_re] = sr
        state_s[1, :, c_re] = si

        s_re = bu_s[:, c_re].astype(BF16)
        s_im = bu_s[:, c_im].astype(BF16)
        y = (_dot(s_re, cre_ref[j]) - _dot(s_im, cim_ref[j])
             + d_ref[:, j * LANES:(j + 1) * LANES] * ut_s[j])
        yt_s[j] = jax.nn.gelu(y)

    yg = jnp.concatenate([yt_s[j] for j in range(SSM_SLAB)], axis=-1).astype(BF16)
    ga = _dot(yg, wglu_ref[...])
    out = ga[:, :D_SSM] * jax.nn.sigmoid(ga[:, D_SSM:])
    for j in range(SSM_SLAB):
        yt_s[j] = out[:, j * LANES:(j + 1) * LANES]
    for b in range(n_b):
        for j in range(SSM_SLAB):
            o_ref[b, :, j * LANES:(j + 1) * LANES] = (
                yt_s[j, pl.ds(b, n_t, stride=n_b), :].astype(BF16))


def _s5(u, lam_re, lam_im, lstep, b_re_blk, b_im_blk, c_re_blk, c_im_blk, d_skip, w_glu_b):
    bsz, seq, _ = u.shape
    assert bsz == SUBLANES
    n_t = SSM_L
    rows = bsz * n_t
    chunk = N_STATES // SSM_SLAB
    c2 = lambda c: (0, 0)
    c3 = lambda c: (0, 0, 0)
    return pl.pallas_call(
        _s5_kernel,
        out_shape=jax.ShapeDtypeStruct((bsz, seq, D_SSM), BF16),
        grid=(seq // n_t,),
        in_specs=[
            pl.BlockSpec((bsz, n_t, D_SSM), lambda c: (0, c, 0)),
            pl.BlockSpec(lam_re.shape, c2),
            pl.BlockSpec(lam_im.shape, c2),
            pl.BlockSpec(lstep.shape, c2),
            pl.BlockSpec(b_re_blk.shape, c3),
            pl.BlockSpec(b_im_blk.shape, c3),
            pl.BlockSpec(c_re_blk.shape, c3),
            pl.BlockSpec(c_im_blk.shape, c3),
            pl.BlockSpec(d_skip.shape, c2),
            pl.BlockSpec(w_glu_b.shape, c2),
        ],
        out_specs=pl.BlockSpec((bsz, n_t, D_SSM), lambda c: (0, c, 0)),
        scratch_shapes=[
            pltpu.VMEM((bsz, N_STATES), F32),
            pltpu.VMEM((bsz, N_STATES), F32),
            pltpu.VMEM((2 * SSM_SLAB, LANES, chunk), BF16),
            pltpu.VMEM((2, bsz, N_STATES), F32),
            pltpu.VMEM((SSM_SLAB, rows, LANES), F32),
            pltpu.VMEM((rows, 2 * N_STATES), F32),
            pltpu.VMEM((SSM_SLAB, rows, LANES), F32),
        ],
        compiler_params=pltpu.CompilerParams(
            dimension_semantics=("arbitrary",),
            vmem_limit_bytes=VMEM_LIMIT_BYTES),
        name="s5",
    )(u, lam_re, lam_im, lstep, b_re_blk, b_im_blk, c_re_blk, c_im_blk, d_skip, w_glu_b)


def _sublane_allreduce(x, op):
    for shift in (4, 2, 1):
        x = op(x, pltpu.roll(x, shift, 0))
    return x


def _diffattn_kernel(q_ref, k_ref, vt_ref, lq1_ref, lk1_ref, lq2_ref, lk2_ref, gain_ref,
                     o_ref, qmt_s, m_s, acc_s, sbuf_s):
    qi = pl.program_id(1)
    tq = q_ref.shape[0]
    tk = tq
    n = 2 * tq

    for h in range(N_HEADS):
        qt = q_ref[:, h * HEAD_WIDTH:(h + 1) * HEAD_WIDTH].astype(F32).T
        d = lax.broadcasted_iota(jnp.int32, qt.shape, 0)
        qmt_s[h, :, 0:tq] = jnp.where(d < HEAD_DIM, qt, 0.0).astype(BF16)
        qmt_s[h, :, tq:n] = jnp.where(d >= HEAD_DIM, qt, 0.0).astype(BF16)
    acc_groups = acc_s.shape[1] // SUBLANES

    n_slots = sbuf_s.shape[0]

    def update(blocks):
        items = [(j, masked, first, h) for j, masked, first in blocks for h in range(N_HEADS)]

        def issue_scores(idx):
            j, _, _, h = items[idx]
            r0 = pl.multiple_of(j * tk, tk)
            kt = k_ref[pl.ds(r0, tk), h * HEAD_WIDTH:(h + 1) * HEAD_WIDTH]
            sbuf_s[idx % n_slots] = _dot(kt, qmt_s[h])

        def load_scores(idx):
            s = sbuf_s[idx % n_slots]
            if items[idx][1]:
                key = lax.broadcasted_iota(jnp.int32, s.shape, 0)
                qry = lax.broadcasted_iota(jnp.int32, s.shape, 1) & (tq - 1)
                s = jnp.where(key <= qry, s, -jnp.inf)
            return s.reshape(tk // SUBLANES, SUBLANES, n)

        def running_max(idx):
            _, _, first, h = items[idx]
            m_cur = _sublane_allreduce(jnp.max(load_scores(idx), axis=0), jnp.maximum)
            if first:
                m_s[h] = m_cur
                return m_cur, None
            m_prev = m_s[h]
            m_new = jnp.maximum(m_prev, m_cur)
            m_s[h] = m_new
            return m_new, jnp.exp2(m_prev - m_new)

        def accumulate(idx, m_new, alpha):
            j, _, first, h = items[idx]
            p3 = jnp.exp2(load_scores(idx) - m_new[None])
            pv = _dot(vt_ref[h, j], p3.reshape(tk, n).astype(BF16))
            if first:
                acc_s[h] = pv
            else:
                acc3 = acc_s[h].reshape(acc_groups, SUBLANES, n)
                acc_s[h] = (alpha[None] * acc3).reshape(acc_s.shape[1], n) + pv

        for idx in range(min(2, len(items))):
            issue_scores(idx)
        stats = running_max(0)
        for idx in range(len(items)):
            if idx + 2 < len(items):
                issue_scores(idx + 2)
            nxt = running_max(idx + 1) if idx + 1 < len(items) else None
            accumulate(idx, *stats)
            stats = nxt

    @pl.when(qi % 2 == 0)
    def _():
        update([(qi, True, True)])

    @pl.when(qi % 2 == 1)
    def _():
        update([(qi, True, True), (qi - 1, False, False)])

    def body(jj, carry):
        update([(2 * jj, False, False), (2 * jj + 1, False, False)])
        return carry

    lax.fori_loop(0, qi // 2, body, 0)

    lam = (jnp.exp(jnp.sum(lq1_ref[...] * lk1_ref[...], keepdims=True))
           - jnp.exp(jnp.sum(lq2_ref[...] * lk2_ref[...], keepdims=True)) + LAMBDA_INIT)
    for h in range(N_HEADS):
        num3 = acc_s[h, 0:HEAD_WIDTH, :].reshape(HEAD_WIDTH // SUBLANES, SUBLANES, n)
        den = acc_s[h, HEAD_WIDTH:HEAD_WIDTH + SUBLANES, :]
        on = (num3 / den[None]).reshape(HEAD_WIDTH, n)
        o = (on[:, 0:tq] - lam * on[:, tq:n]).T
        ms = jnp.mean(o * o, axis=-1, keepdims=True)
        o = o * lax.rsqrt(ms + RMS_EPS) * gain_ref[...] * (1.0 - LAMBDA_INIT)
        o_ref[:, h * HEAD_WIDTH:(h + 1) * HEAD_WIDTH] = o.astype(BF16)


def _diffattn(q, k, vt, lq1, lk1, lq2, lk2, gain):
    bsz, seq, _ = q.shape
    tq = ATT_TQ
    assert tq & (tq - 1) == 0
    n_kv = seq // tq
    grid = (bsz, n_kv)
    c2 = lambda b, i: (0, 0)
    return pl.pallas_call(
        _diffattn_kernel,
        out_shape=jax.ShapeDtypeStruct((bsz, seq, D_ATTN), BF16),
        grid=grid,
        in_specs=[
            pl.BlockSpec((None, tq, D_ATTN), lambda b, i: (b, i, 0)),
            pl.BlockSpec((None, seq, D_ATTN), lambda b, i: (b, 0, 0)),
            pl.BlockSpec((None, N_HEADS, n_kv, VT_ROWS, tq), lambda b, i: (b, 0, 0, 0, 0)),
            pl.BlockSpec(lq1.shape, c2),
            pl.BlockSpec(lk1.shape, c2),
            pl.BlockSpec(lq2.shape, c2),
            pl.BlockSpec(lk2.shape, c2),
            pl.BlockSpec(gain.shape, c2),
        ],
        out_specs=pl.BlockSpec((None, tq, D_ATTN), lambda b, i: (b, i, 0)),
        scratch_shapes=[
            pltpu.VMEM((N_HEADS, HEAD_WIDTH, 2 * tq), BF16),
            pltpu.VMEM((N_HEADS, SUBLANES, 2 * tq), F32),
            pltpu.VMEM((N_HEADS, VT_ROWS, 2 * tq), F32),
            pltpu.VMEM((3, tq, 2 * tq), F32),
        ],
        compiler_params=pltpu.CompilerParams(
            dimension_semantics=("parallel", "arbitrary"),
            vmem_limit_bytes=VMEM_LIMIT_BYTES),
        name="diffattn",
    )(q, k, vt, lq1, lk1, lq2, lk2, gain)


def _merge_kernel(x_ref, g_ref, ys_ref, ya_ref, wps_ref, wpa_ref, wo_ref, lng_ref, lnb_ref,
                  o_ref):
    for r in range(x_ref.shape[0] // SUB_ROWS):
        rows = slice(r * SUB_ROWS, (r + 1) * SUB_ROWS)
        p_ssm = _dot(ys_ref[rows, :], wps_ref[...])
        p_att = _dot(ya_ref[rows, :], wpa_ref[...])
        merged = (g_ref[rows, 0:D_MODEL].astype(F32) * p_ssm
                  + g_ref[rows, D_MODEL:2 * D_MODEL].astype(F32) * p_att)
        mix = _dot(merged.astype(BF16), wo_ref[...])
        z = DEEPNORM_ALPHA * x_ref[rows, :] + mix
        o_ref[rows, :] = _layer_norm(z, lng_ref[...], lnb_ref[...])


def _merge(x2, gates2, ys2, ya2, wps, wpa, wo, ln_g, ln_b):
    n, _ = x2.shape
    tm = MERGE_TM
    row = lambda i: (i, 0)
    c2 = lambda i: (0, 0)
    return pl.pallas_call(
        _merge_kernel,
        out_shape=jax.ShapeDtypeStruct((n, D_MODEL), F32),
        grid=(n // tm,),
        in_specs=[
            pl.BlockSpec((tm, D_MODEL), row),
            pl.BlockSpec((tm, 2 * D_MODEL), row),
            pl.BlockSpec((tm, D_SSM), row),
            pl.BlockSpec((tm, D_ATTN), row),
            pl.BlockSpec(wps.shape, c2),
            pl.BlockSpec(wpa.shape, c2),
            pl.BlockSpec(wo.shape, c2),
            pl.BlockSpec(ln_g.shape, c2),
            pl.BlockSpec(ln_b.shape, c2),
        ],
        out_specs=pl.BlockSpec((tm, D_MODEL), row),
        compiler_params=pltpu.CompilerParams(
            dimension_semantics=("parallel",),
            vmem_limit_bytes=VMEM_LIMIT_BYTES),
        name="merge",
    )(x2, gates2, ys2, ya2, wps, wpa, wo, ln_g, ln_b)


def _ffn_kernel(x_ref, wg_ref, wu_ref, wd_ref, lng_ref, lnb_ref, o_ref):
    for r in range(x_ref.shape[0] // SUB_ROWS):
        rows = slice(r * SUB_ROWS, (r + 1) * SUB_ROWS)
        x = x_ref[rows, :]
        xb = x.astype(BF16)
        ff = jnp.zeros(x.shape, F32)
        for c in range(D_FF // FFN_CHUNK):
            cols = slice(c * FFN_CHUNK, (c + 1) * FFN_CHUNK)
            a = _dot(xb, wg_ref[:, cols])
            b = _dot(xb, wu_ref[:, cols])
            h = (jax.nn.silu(a) * b).astype(BF16)
            ff = ff + _dot(h, wd_ref[cols, :])
        z = DEEPNORM_ALPHA * x + ff
        o_ref[rows, :] = _layer_norm(z, lng_ref[...], lnb_ref[...])


def _ffn(x2, wg, wu, wd, ln_g, ln_b):
    n, _ = x2.shape
    tm = FFN_TM
    row = lambda i: (i, 0)
    c2 = lambda i: (0, 0)
    resident = functools.partial(pl.BlockSpec, index_map=c2, pipeline_mode=pl.Buffered(1))
    return pl.pallas_call(
        _ffn_kernel,
        out_shape=jax.ShapeDtypeStruct((n, D_MODEL), F32),
        grid=(n // tm,),
        in_specs=[
            pl.BlockSpec((tm, D_MODEL), row),
            resident(wg.shape),
            resident(wu.shape),
            resident(wd.shape),
            pl.BlockSpec(ln_g.shape, c2),
            pl.BlockSpec(ln_b.shape, c2),
        ],
        out_specs=pl.BlockSpec((tm, D_MODEL), row),
        compiler_params=pltpu.CompilerParams(
            dimension_semantics=("parallel",),
            vmem_limit_bytes=VMEM_LIMIT_BYTES),
        name="ffn",
    )(x2, wg, wu, wd, ln_g, ln_b)


def _rope_tables(seq):
    half = HEAD_DIM // 2
    inv_freq = ROPE_THETA ** (-jnp.arange(0, HEAD_DIM, 2, dtype=F32) / HEAD_DIM)
    ang = jnp.arange(seq, dtype=F32)[:, None] * inv_freq[None, :]
    cos, sin = jnp.cos(ang), jnp.sin(ang)
    reps = LANES // half
    cos_t = jnp.tile(cos, (1, reps))
    sign = jnp.tile(jnp.concatenate([-jnp.ones((half,), F32), jnp.ones((half,), F32)]),
                    LANES // HEAD_DIM)
    sin_t = jnp.tile(sin, (1, reps)) * sign[None, :]
    return cos_t, sin_t


def _block_diag_in(b):
    gps = N_SSM_GROUPS // SSM_SLAB
    bt = jnp.swapaxes(b, 1, 2).reshape(SSM_SLAB, gps, SSM_GROUP, SSM_STATE)
    eye = jnp.eye(gps, dtype=b.dtype)
    blk = bt[:, :, :, None, :] * eye[None, :, None, :, None]
    return blk.reshape(SSM_SLAB, gps * SSM_GROUP, gps * SSM_STATE)


def _block_diag_out(c):
    gps = N_SSM_GROUPS // SSM_SLAB
    ct = jnp.swapaxes(c, 1, 2).reshape(SSM_SLAB, gps, SSM_STATE, SSM_GROUP)
    eye = jnp.eye(gps, dtype=c.dtype)
    blk = ct[:, :, :, None, :] * eye[None, :, None, :, None]
    return blk.reshape(SSM_SLAB, gps * SSM_STATE, gps * SSM_GROUP)


def kernel(x, w_in, b_gate, ssm_lambda_re, ssm_lambda_im, ssm_log_step, ssm_b_re, ssm_b_im, ssm_c_re, ssm_c_im, ssm_d, w_glu, lambda_q1, lambda_k1, lambda_q2, lambda_k2, subln_gain, w_proj_ssm, w_proj_attn, w_out, ln1_g, ln1_b, w_ffn_gate, w_ffn_up, w_ffn_down, ln2_g, ln2_b):
    bsz, seq, _ = x.shape
    cos_t, sin_t = _rope_tables(seq)
    for l in range(DEPTH):
        u, q, k, vt, gates = _in_proj(x, w_in[l].astype(BF16), b_gate[l][None, :], cos_t, sin_t)

        lstep = jnp.broadcast_to(ssm_log_step[l][:, None], (N_SSM_GROUPS, SSM_STATE))
        y_ssm = _s5(
            u,
            ssm_lambda_re[l].reshape(1, N_STATES),
            ssm_lambda_im[l].reshape(1, N_STATES),
            lstep.reshape(1, N_STATES),
            _block_diag_in(ssm_b_re[l]), _block_diag_in(ssm_b_im[l]),
            _block_diag_out(ssm_c_re[l]).astype(BF16), _block_diag_out(ssm_c_im[l]).astype(BF16),
            ssm_d[l].reshape(1, D_SSM),
            w_glu[l].astype(BF16))

        y_att = _diffattn(q, k, vt,lambda_q1[l][None, :], lambda_k1[l][None, :],
                          lambda_q2[l][None, :], lambda_k2[l][None, :], subln_gain[l][None, :])

        n = bsz * seq
        x1 = _merge(x.reshape(n, D_MODEL), gates.reshape(n, 2 * D_MODEL),
                    y_ssm.reshape(n, D_SSM), y_att.reshape(n, D_ATTN),
                    w_proj_ssm[l].astype(BF16), w_proj_attn[l].astype(BF16),
                    w_out[l].astype(BF16), ln1_g[l][None, :], ln1_b[l][None, :])
        x2 = _ffn(x1, w_ffn_gate[l].astype(BF16), w_ffn_up[l].astype(BF16),
                  w_ffn_down[l].astype(BF16), ln2_g[l][None, :], ln2_b[l][None, :])
        x = x2.reshape(bsz, seq, D_MODEL)
    return x
```

```python
import functools
import math

import jax
import jax.numpy as jnp
from jax import lax
from jax.experimental import pallas as pl
from jax.experimental.pallas import tpu as pltpu

F32 = jnp.float32
BF16 = jnp.bfloat16

D_MODEL = 1024
D_SSM = 512
SSM_GROUP = 16
N_SSM_GROUPS = 32
SSM_STATE = 64
N_STATES = N_SSM_GROUPS * SSM_STATE
D_ATTN = 512
HEAD_DIM = 64
HEAD_WIDTH = 2 * HEAD_DIM
N_HEADS = 4
ROPE_THETA = 10000.0
D_FF = 2816
DEPTH = 1
DEEPNORM_ALPHA = (2.0 * DEPTH) ** 0.25
LN_EPS = 1e-5
RMS_EPS = 1e-5
LAMBDA_INIT = 0.8 - 0.6 * math.exp(-0.3 * 0)

LANES = 128
SUBLANES = 8
VMEM_LIMIT_BYTES = 56 * 1024 * 1024

IN_TS = 512
SSM_L = 128
SSM_SLAB = 4
ATT_TQ = 256
VT_ROWS = HEAD_WIDTH + 16
MERGE_TM = 1024
FFN_TM = 1024
SUB_ROWS = 512
MERGE_SUB_ROWS = 256
MXU_TILE = 256
FFN_CHUNKS = (6 * MXU_TILE, 5 * MXU_TILE)
assert sum(FFN_CHUNKS) == D_FF


def _dot(a, b):
    return jnp.dot(a, b, preferred_element_type=F32)


def _layer_norm(z, g, b):
    mu = jnp.mean(z, axis=-1, keepdims=True)
    zc = z - mu
    var = jnp.mean(zc * zc, axis=-1, keepdims=True)
    return zc * lax.rsqrt(var + LN_EPS) * g + b


def _in_proj_kernel(x_ref, w_ref, bg_ref, cos_ref, sin_ref,
                    u_ref, q_ref, k_ref, vt_ref, g_ref):
    xb = x_ref[...].astype(BF16)
    off_q, off_k, off_v = D_SSM, D_SSM + D_ATTN, D_SSM + 2 * D_ATTN
    off_g = off_v + D_ATTN
    u_ref[...] = _dot(xb, w_ref[:, 0:D_SSM])
    cos = cos_ref[...]
    sin = sin_ref[...]
    lane = lax.broadcasted_iota(jnp.int32, cos.shape, 1)
    low_half = (lane & (HEAD_DIM // 2)) == 0
    for out_ref, off, scale in ((q_ref, off_q, HEAD_DIM ** -0.5 * math.log2(math.e)),
                                (k_ref, off_k, 1.0)):
        h = _dot(xb, w_ref[:, off:off + D_ATTN])
        for hd in range(N_HEADS):
            slab = h[:, hd * LANES:(hd + 1) * LANES]
            partner = jnp.where(low_half,
                                pltpu.roll(slab, LANES - HEAD_DIM // 2, 1),
                                pltpu.roll(slab, HEAD_DIM // 2, 1))
            r = slab * cos + partner * sin
            if scale != 1.0:
                r = r * scale
            out_ref[:, hd * LANES:(hd + 1) * LANES] = r.astype(BF16)
    v = _dot(xb, w_ref[:, off_v:off_v + D_ATTN])
    ones = jnp.ones((VT_ROWS - HEAD_WIDTH, ATT_TQ), BF16)
    for hd in range(N_HEADS):
        for r in range(vt_ref.shape[1]):
            tile = v[r * ATT_TQ:(r + 1) * ATT_TQ, hd * LANES:(hd + 1) * LANES]
            vt_ref[hd, r, 0:HEAD_WIDTH, :] = tile.T.astype(BF16)
            vt_ref[hd, r, HEAD_WIDTH:VT_ROWS, :] = ones
    g = _dot(xb, w_ref[:, off_g:off_g + 2 * D_MODEL]) + bg_ref[...]
    g_ref[...] = jax.nn.sigmoid(g).astype(BF16)


def _in_proj(x, w_in_b, b_gate, cos_t, sin_t):
    bsz, seq, _ = x.shape
    ts = IN_TS
    grid = (bsz, seq // ts)
    row = lambda b, i: (b, i, 0)
    const2 = lambda b, i: (0, 0)
    out_shape = (
        jax.ShapeDtypeStruct((bsz, seq, D_SSM), F32),
        jax.ShapeDtypeStruct((bsz, seq, D_ATTN), BF16),
        jax.ShapeDtypeStruct((bsz, seq, D_ATTN), BF16),
        jax.ShapeDtypeStruct((bsz, N_HEADS, seq // ATT_TQ, VT_ROWS, ATT_TQ), BF16),
        jax.ShapeDtypeStruct((bsz, seq, 2 * D_MODEL), BF16),
    )
    return pl.pallas_call(
        _in_proj_kernel,
        out_shape=out_shape,
        grid=grid,
        in_specs=[
            pl.BlockSpec((None, ts, D_MODEL), row),
            pl.BlockSpec(w_in_b.shape, const2),
            pl.BlockSpec(b_gate.shape, const2),
            pl.BlockSpec((ts, LANES), lambda b, i: (i, 0)),
            pl.BlockSpec((ts, LANES), lambda b, i: (i, 0)),
        ],
        out_specs=(
            pl.BlockSpec((None, ts, D_SSM), row),
            pl.BlockSpec((None, ts, D_ATTN), row),
            pl.BlockSpec((None, ts, D_ATTN), row),
            pl.BlockSpec((None, N_HEADS, ts // ATT_TQ, VT_ROWS, ATT_TQ),
                         lambda b, i: (b, 0, i, 0, 0)),
            pl.BlockSpec((None, ts, 2 * D_MODEL), row),
        ),
        compiler_params=pltpu.CompilerParams(
            dimension_semantics=("parallel", "parallel"),
            vmem_limit_bytes=VMEM_LIMIT_BYTES),
        name="in_proj",
    )(x, w_in_b, b_gate, cos_t, sin_t)


def _s5_kernel(u_ref, lre_ref, lim_ref, lstep_ref, bre_ref, bim_ref, cre_ref, cim_ref,
               d_ref, wglu_ref, o_ref,
               are_s, aim_s, bb_s, state_s, ut_s, bu_s, yt_s):
    n_b, n_t, _ = u_ref.shape
    rows = n_b * n_t
    chunk = N_STATES // SSM_SLAB

    @pl.when(pl.program_id(0) == 0)
    def _():
        lr = lre_ref[...]
        li = lim_ref[...]
        dt = jnp.exp(lstep_ref[...])
        mag = jnp.exp(lr * dt)
        ar = mag * jnp.cos(li * dt)
        ai = mag * jnp.sin(li * dt)
        are_s[...] = jnp.broadcast_to(ar, are_s.shape)
        aim_s[...] = jnp.broadcast_to(ai, aim_s.shape)
        nr, ni = ar - 1.0, ai
        den = lr * lr + li * li
        fr = (nr * lr + ni * li) / den
        fi = (ni * lr - nr * li) / den
        for j in range(SSM_SLAB):
            frj = fr[:, j * chunk:(j + 1) * chunk]
            fij = fi[:, j * chunk:(j + 1) * chunk]
            br = bre_ref[j]
            bi = bim_ref[j]
            bb_s[j] = (frj * br - fij * bi).astype(BF16)
            bb_s[SSM_SLAB + j] = (frj * bi + fij * br).astype(BF16)
        state_s[...] = jnp.zeros(state_s.shape, F32)

    for b in range(n_b):
        for j in range(SSM_SLAB):
            ut_s[j, pl.ds(b, n_t, stride=n_b), :] = u_ref[b, :, j * LANES:(j + 1) * LANES]

    for j in range(SSM_SLAB):
        lhs = ut_s[j].astype(BF16)
        bu_s[:, j * chunk:(j + 1) * chunk] = _dot(lhs, bb_s[j])
        bu_s[:, N_STATES + j * chunk:N_STATES + (j + 1) * chunk] = _dot(lhs, bb_s[SSM_SLAB + j])

    for j in range(SSM_SLAB):
        c_re = slice(j * chunk, (j + 1) * chunk)
        c_im = slice(N_STATES + j * chunk, N_STATES + (j + 1) * chunk)
        ar = are_s[:, c_re]
        ai = aim_s[:, c_re]
        sr = state_s[0, :, c_re]
        si = state_s[1, :, c_re]
        for t in range(n_t):
            rows_t = slice(t * n_b, (t + 1) * n_b)
            sr, si = (ar * sr - ai * si + bu_s[rows_t, c_re],
                      ar * si + ai * sr + bu_s[rows_t, c_im])
            bu_s[rows_t, c_re] = sr
            bu_s[rows_t, c_im] = si
        state_s[0, :, c_re] = sr
        state_s[1, :, c_re] = si

        s_re = bu_s[:, c_re].astype(BF16)
        s_im = bu_s[:, c_im].astype(BF16)
        y = (_dot(s_re, cre_ref[j]) - _dot(s_im, cim_ref[j])
             + d_ref[:, j * LANES:(j + 1) * LANES] * ut_s[j])
        yt_s[j] = jax.nn.gelu(y)

    yg = jnp.concatenate([yt_s[j] for j in range(SSM_SLAB)], axis=-1).astype(BF16)
    ga = _dot(yg, wglu_ref[...])
    out = ga[:, :D_SSM] * jax.nn.sigmoid(ga[:, D_SSM:])
    for j in range(SSM_SLAB):
        yt_s[j] = out[:, j * LANES:(j + 1) * LANES]
    for b in range(n_b):
        for j in range(SSM_SLAB):
            o_ref[b, :, j * LANES:(j + 1) * LANES] = (
                yt_s[j, pl.ds(b, n_t, stride=n_b), :].astype(BF16))


def _s5(u, lam_re, lam_im, lstep, b_re_blk, b_im_blk, c_re_blk, c_im_blk, d_skip, w_glu_b):
    bsz, seq, _ = u.shape
    assert bsz == SUBLANES
    n_t = SSM_L
    rows = bsz * n_t
    chunk = N_STATES // SSM_SLAB
    c2 = lambda c: (0, 0)
    c3 = lambda c: (0, 0, 0)
    return pl.pallas_call(
        _s5_kernel,
        out_shape=jax.ShapeDtypeStruct((bsz, seq, D_SSM), BF16),
        grid=(seq // n_t,),
        in_specs=[
            pl.BlockSpec((bsz, n_t, D_SSM), lambda c: (0, c, 0)),
            pl.BlockSpec(lam_re.shape, c2),
            pl.BlockSpec(lam_im.shape, c2),
            pl.BlockSpec(lstep.shape, c2),
            pl.BlockSpec(b_re_blk.shape, c3),
            pl.BlockSpec(b_im_blk.shape, c3),
            pl.BlockSpec(c_re_blk.shape, c3),
            pl.BlockSpec(c_im_blk.shape, c3),
            pl.BlockSpec(d_skip.shape, c2),
            pl.BlockSpec(w_glu_b.shape, c2),
        ],
        out_specs=pl.BlockSpec((bsz, n_t, D_SSM), lambda c: (0, c, 0)),
        scratch_shapes=[
            pltpu.VMEM((bsz, N_STATES), F32),
            pltpu.VMEM((bsz, N_STATES), F32),
            pltpu.VMEM((2 * SSM_SLAB, LANES, chunk), BF16),
            pltpu.VMEM((2, bsz, N_STATES), F32),
            pltpu.VMEM((SSM_SLAB, rows, LANES), F32),
            pltpu.VMEM((rows, 2 * N_STATES), F32),
            pltpu.VMEM((SSM_SLAB, rows, LANES), F32),
        ],
        compiler_params=pltpu.CompilerParams(
            dimension_semantics=("arbitrary",),
            vmem_limit_bytes=VMEM_LIMIT_BYTES),
        name="s5",
    )(u, lam_re, lam_im, lstep, b_re_blk, b_im_blk, c_re_blk, c_im_blk, d_skip, w_glu_b)


def _sublane_allreduce(x, op):
    for shift in (4, 2, 1):
        x = op(x, pltpu.roll(x, shift, 0))
    return x


def _diffattn_kernel(q_ref, k_ref, vt_ref, lq1_ref, lk1_ref, lq2_ref, lk2_ref, gain_ref,
                     o_ref, qmt_s, m_s, acc_s, sbuf_s):
    qi = pl.program_id(1)
    tq = q_ref.shape[0]
    tk = tq
    n = 2 * tq

    for h in range(N_HEADS):
        qt = q_ref[:, h * HEAD_WIDTH:(h + 1) * HEAD_WIDTH].astype(F32).T
        d = lax.broadcasted_iota(jnp.int32, qt.shape, 0)
        qmt_s[h, :, 0:tq] = jnp.where(d < HEAD_DIM, qt, 0.0).astype(BF16)
        qmt_s[h, :, tq:n] = jnp.where(d >= HEAD_DIM, qt, 0.0).astype(BF16)
    acc_groups = acc_s.shape[1] // SUBLANES

    n_slots = sbuf_s.shape[0]

    def update(blocks):
        items = [(j, masked, first, h) for j, masked, first in blocks for h in range(N_HEADS)]

        def issue_scores(idx):
            j, _, _, h = items[idx]
            r0 = pl.multiple_of(j * tk, tk)
            kt = k_ref[pl.ds(r0, tk), h * HEAD_WIDTH:(h + 1) * HEAD_WIDTH]
            sbuf_s[idx % n_slots] = _dot(kt, qmt_s[h])

        def load_scores(idx):
            s = sbuf_s[idx % n_slots]
            if items[idx][1]:
                key = lax.broadcasted_iota(jnp.int32, s.shape, 0)
                qry = lax.broadcasted_iota(jnp.int32, s.shape, 1) & (tq - 1)
                s = jnp.where(key <= qry, s, -jnp.inf)
            return s.reshape(tk // SUBLANES, SUBLANES, n)

        def running_max(idx):
            _, _, first, h = items[idx]
            m_cur = _sublane_allreduce(jnp.max(load_scores(idx), axis=0), jnp.maximum)
            if first:
                m_s[h] = m_cur
                return m_cur, None
            m_prev = m_s[h]
            m_new = jnp.maximum(m_prev, m_cur)
            m_s[h] = m_new
            return m_new, jnp.exp2(m_prev - m_new)

        def accumulate(idx, m_new, alpha):
            j, _, first, h = items[idx]
            p3 = jnp.exp2(load_scores(idx) - m_new[None])
            pv = _dot(vt_ref[h, j], p3.reshape(tk, n).astype(BF16))
            if first:
                acc_s[h] = pv
            else:
                acc3 = acc_s[h].reshape(acc_groups, SUBLANES, n)
                acc_s[h] = (alpha[None] * acc3).reshape(acc_s.shape[1], n) + pv

        for idx in range(min(2, len(items))):
            issue_scores(idx)
        stats = running_max(0)
        for idx in range(len(items)):
            if idx + 2 < len(items):
                issue_scores(idx + 2)
            nxt = running_max(idx + 1) if idx + 1 < len(items) else None
            accumulate(idx, *stats)
            stats = nxt

    @pl.when(qi % 2 == 0)
    def _():
        update([(qi, True, True)])

    @pl.when(qi % 2 == 1)
    def _():
        update([(qi, True, True), (qi - 1, False, False)])

    below = qi - qi % 2

    @pl.when(below % 4 == 2)
    def _():
        update([(below - 2, False, False), (below - 1, False, False)])

    def body(jj, carry):
        update([(4 * jj + t, False, False) for t in range(4)])
        return carry

    lax.fori_loop(0, below // 4, body, 0)

    lam = (jnp.exp(jnp.sum(lq1_ref[...] * lk1_ref[...], keepdims=True))
           - jnp.exp(jnp.sum(lq2_ref[...] * lk2_ref[...], keepdims=True)) + LAMBDA_INIT)
    for h in range(N_HEADS):
        num3 = acc_s[h, 0:HEAD_WIDTH, :].reshape(HEAD_WIDTH // SUBLANES, SUBLANES, n)
        den = acc_s[h, HEAD_WIDTH:HEAD_WIDTH + SUBLANES, :]
        on = (num3 / den[None]).reshape(HEAD_WIDTH, n)
        o = (on[:, 0:tq] - lam * on[:, tq:n]).T
        ms = jnp.mean(o * o, axis=-1, keepdims=True)
        o = o * lax.rsqrt(ms + RMS_EPS) * gain_ref[...] * (1.0 - LAMBDA_INIT)
        o_ref[:, h * HEAD_WIDTH:(h + 1) * HEAD_WIDTH] = o.astype(BF16)


def _diffattn(q, k, vt, lq1, lk1, lq2, lk2, gain):
    bsz, seq, _ = q.shape
    tq = ATT_TQ
    assert tq & (tq - 1) == 0
    n_kv = seq // tq
    grid = (bsz, n_kv)
    c2 = lambda b, i: (0, 0)
    return pl.pallas_call(
        _diffattn_kernel,
        out_shape=jax.ShapeDtypeStruct((bsz, seq, D_ATTN), BF16),
        grid=grid,
        in_specs=[
            pl.BlockSpec((None, tq, D_ATTN), lambda b, i: (b, i, 0)),
            pl.BlockSpec((None, seq, D_ATTN), lambda b, i: (b, 0, 0)),
            pl.BlockSpec((None, N_HEADS, n_kv, VT_ROWS, tq), lambda b, i: (b, 0, 0, 0, 0)),
            pl.BlockSpec(lq1.shape, c2),
            pl.BlockSpec(lk1.shape, c2),
            pl.BlockSpec(lq2.shape, c2),
            pl.BlockSpec(lk2.shape, c2),
            pl.BlockSpec(gain.shape, c2),
        ],
        out_specs=pl.BlockSpec((None, tq, D_ATTN), lambda b, i: (b, i, 0)),
        scratch_shapes=[
            pltpu.VMEM((N_HEADS, HEAD_WIDTH, 2 * tq), BF16),
            pltpu.VMEM((N_HEADS, SUBLANES, 2 * tq), F32),
            pltpu.VMEM((N_HEADS, VT_ROWS, 2 * tq), F32),
            pltpu.VMEM((3, tq, 2 * tq), F32),
        ],
        compiler_params=pltpu.CompilerParams(
            dimension_semantics=("parallel", "arbitrary"),
            vmem_limit_bytes=VMEM_LIMIT_BYTES),
        name="diffattn",
    )(q, k, vt, lq1, lk1, lq2, lk2, gain)


def _merge_kernel(x_ref, g_ref, ys_ref, ya_ref, wps_ref, wpa_ref, wo_ref, lng_ref, lnb_ref,
                  o_ref):
    n_sub = x_ref.shape[0] // MERGE_SUB_ROWS
    rows = [slice(r * MERGE_SUB_ROWS, (r + 1) * MERGE_SUB_ROWS) for r in range(n_sub)]

    def project(r):
        return _dot(ys_ref[rows[r], :], wps_ref[...]), _dot(ya_ref[rows[r], :], wpa_ref[...])

    def norm(r, mix):
        z = DEEPNORM_ALPHA * x_ref[rows[r], :] + mix
        o_ref[rows[r], :] = _layer_norm(z, lng_ref[...], lnb_ref[...])

    proj = project(0)
    pending = None
    for r in range(n_sub):
        p_ssm, p_att = proj
        if r + 1 < n_sub:
            proj = project(r + 1)
        merged = (g_ref[rows[r], 0:D_MODEL].astype(F32) * p_ssm
                  + g_ref[rows[r], D_MODEL:2 * D_MODEL].astype(F32) * p_att)
        mix = _dot(merged.astype(BF16), wo_ref[...])
        if pending is not None:
            norm(*pending)
        pending = (r, mix)
    norm(*pending)


def _merge(x2, gates2, ys2, ya2, wps, wpa, wo, ln_g, ln_b):
    n, _ = x2.shape
    tm = MERGE_TM
    row = lambda i: (i, 0)
    c2 = lambda i: (0, 0)
    return pl.pallas_call(
        _merge_kernel,
        out_shape=jax.ShapeDtypeStruct((n, D_MODEL), F32),
        grid=(n // tm,),
        in_specs=[
            pl.BlockSpec((tm, D_MODEL), row),
            pl.BlockSpec((tm, 2 * D_MODEL), row),
            pl.BlockSpec((tm, D_SSM), row),
            pl.BlockSpec((tm, D_ATTN), row),
            pl.BlockSpec(wps.shape, c2),
            pl.BlockSpec(wpa.shape, c2),
            pl.BlockSpec(wo.shape, c2),
            pl.BlockSpec(ln_g.shape, c2),
            pl.BlockSpec(ln_b.shape, c2),
        ],
        out_specs=pl.BlockSpec((tm, D_MODEL), row),
        compiler_params=pltpu.CompilerParams(
            dimension_semantics=("parallel",),
            vmem_limit_bytes=VMEM_LIMIT_BYTES),
        name="merge",
    )(x2, gates2, ys2, ya2, wps, wpa, wo, ln_g, ln_b)


def _ffn_kernel(x_ref, wg_ref, wu_ref, wd_ref, lng_ref, lnb_ref, o_ref):
    for r in range(x_ref.shape[0] // SUB_ROWS):
        rows = slice(r * SUB_ROWS, (r + 1) * SUB_ROWS)
        x = x_ref[rows, :]
        xb = x.astype(BF16)
        ff = jnp.zeros(x.shape, F32)
        start = 0
        for width in FFN_CHUNKS:
            cols = slice(start, start + width)
            start += width
            a = _dot(xb, wg_ref[:, cols])
            b = _dot(xb, wu_ref[:, cols])
            h = (jax.nn.silu(a) * b).astype(BF16)
            ff = ff + _dot(h, wd_ref[cols, :])
        z = DEEPNORM_ALPHA * x + ff
        o_ref[rows, :] = _layer_norm(z, lng_ref[...], lnb_ref[...])


def _ffn(x2, wg, wu, wd, ln_g, ln_b):
    n, _ = x2.shape
    tm = FFN_TM
    row = lambda i: (i, 0)
    c2 = lambda i: (0, 0)
    resident = functools.partial(pl.BlockSpec, index_map=c2, pipeline_mode=pl.Buffered(1))
    return pl.pallas_call(
        _ffn_kernel,
        out_shape=jax.ShapeDtypeStruct((n, D_MODEL), F32),
        grid=(n // tm,),
        in_specs=[
            pl.BlockSpec((tm, D_MODEL), row),
            resident(wg.shape),
            resident(wu.shape),
            resident(wd.shape),
            pl.BlockSpec(ln_g.shape, c2),
            pl.BlockSpec(ln_b.shape, c2),
        ],
        out_specs=pl.BlockSpec((tm, D_MODEL), row),
        compiler_params=pltpu.CompilerParams(
            dimension_semantics=("parallel",),
            vmem_limit_bytes=VMEM_LIMIT_BYTES),
        name="ffn",
    )(x2, wg, wu, wd, ln_g, ln_b)


def _rope_tables(seq):
    half = HEAD_DIM // 2
    inv_freq = ROPE_THETA ** (-jnp.arange(0, HEAD_DIM, 2, dtype=F32) / HEAD_DIM)
    ang = jnp.arange(seq, dtype=F32)[:, None] * inv_freq[None, :]
    cos, sin = jnp.cos(ang), jnp.sin(ang)
    reps = LANES // half
    cos_t = jnp.tile(cos, (1, reps))
    sign = jnp.tile(jnp.concatenate([-jnp.ones((half,), F32), jnp.ones((half,), F32)]),
                    LANES // HEAD_DIM)
    sin_t = jnp.tile(sin, (1, reps)) * sign[None, :]
    return cos_t, sin_t


def _block_diag_in(b):
    gps = N_SSM_GROUPS // SSM_SLAB
    bt = jnp.swapaxes(b, 1, 2).reshape(SSM_SLAB, gps, SSM_GROUP, SSM_STATE)
    eye = jnp.eye(gps, dtype=b.dtype)
    blk = bt[:, :, :, None, :] * eye[None, :, None, :, None]
    return blk.reshape(SSM_SLAB, gps * SSM_GROUP, gps * SSM_STATE)


def _block_diag_out(c):
    gps = N_SSM_GROUPS // SSM_SLAB
    ct = jnp.swapaxes(c, 1, 2).reshape(SSM_SLAB, gps, SSM_STATE, SSM_GROUP)
    eye = jnp.eye(gps, dtype=c.dtype)
    blk = ct[:, :, :, None, :] * eye[None, :, None, :, None]
    return blk.reshape(SSM_SLAB, gps * SSM_STATE, gps * SSM_GROUP)


def kernel(x, w_in, b_gate, ssm_lambda_re, ssm_lambda_im, ssm_log_step, ssm_b_re, ssm_b_im, ssm_c_re, ssm_c_im, ssm_d, w_glu, lambda_q1, lambda_k1, lambda_q2, lambda_k2, subln_gain, w_proj_ssm, w_proj_attn, w_out, ln1_g, ln1_b, w_ffn_gate, w_ffn_up, w_ffn_down, ln2_g, ln2_b):
    bsz, seq, _ = x.shape
    cos_t, sin_t = _rope_tables(seq)
    for l in range(DEPTH):
        u, q, k, vt, gates = _in_proj(x, w_in[l].astype(BF16), b_gate[l][None, :], cos_t, sin_t)

        lstep = jnp.broadcast_to(ssm_log_step[l][:, None], (N_SSM_GROUPS, SSM_STATE))
        y_ssm = _s5(
            u,
            ssm_lambda_re[l].reshape(1, N_STATES),
            ssm_lambda_im[l].reshape(1, N_STATES),
            lstep.reshape(1, N_STATES),
            _block_diag_in(ssm_b_re[l]), _block_diag_in(ssm_b_im[l]),
            _block_diag_out(ssm_c_re[l]).astype(BF16), _block_diag_out(ssm_c_im[l]).astype(BF16),
            ssm_d[l].reshape(1, D_SSM),
            w_glu[l].astype(BF16))

        y_att = _diffattn(q, k, vt,lambda_q1[l][None, :], lambda_k1[l][None, :],
                          lambda_q2[l][None, :], lambda_k2[l][None, :], subln_gain[l][None, :])

        n = bsz * seq
        x1 = _merge(x.reshape(n, D_MODEL), gates.reshape(n, 2 * D_MODEL),
                    y_ssm.reshape(n, D_SSM), y_att.reshape(n, D_ATTN),
                    w_proj_ssm[l].astype(BF16), w_proj_attn[l].astype(BF16),
                    w_out[l].astype(BF16), ln1_g[l][None, :], ln1_b[l][None, :])
        x2 = _ffn(x1, w_ffn_gate[l].astype(BF16), w_ffn_up[l].astype(BF16),
                  w_ffn_down[l].astype(BF16), ln2_g[l][None, :], ln2_b[l][None, :])
        x = x2.reshape(bsz, seq, D_MODEL)
    return x
```

```python
import functools
import math

import jax
import jax.numpy as jnp
from jax import lax
from jax.experimental import pallas as pl
from jax.experimental.pallas import tpu as pltpu

F32 = jnp.float32
BF16 = jnp.bfloat16

D_MODEL = 1024
D_SSM = 512
SSM_GROUP = 16
N_SSM_GROUPS = 32
SSM_STATE = 64
N_STATES = N_SSM_GROUPS * SSM_STATE
D_ATTN = 512
HEAD_DIM = 64
HEAD_WIDTH = 2 * HEAD_DIM
N_HEADS = 4
ROPE_THETA = 10000.0
D_FF = 2816
DEPTH = 1
DEEPNORM_ALPHA = (2.0 * DEPTH) ** 0.25
LN_EPS = 1e-5
RMS_EPS = 1e-5
LAMBDA_INIT = 0.8 - 0.6 * math.exp(-0.3 * 0)

LANES = 128
SUBLANES = 8
VMEM_LIMIT_BYTES = 56 * 1024 * 1024

IN_TS = 512
SSM_L = 128
SSM_SLAB = 4
ATT_TQ = 256
VT_ROWS = HEAD_WIDTH + 16
MERGE_TM = 1024
FFN_TM = 1024
SUB_ROWS = 512
MERGE_SUB_ROWS = 256
MXU_TILE = 256
FFN_CHUNKS = (6 * MXU_TILE, 5 * MXU_TILE)
assert sum(FFN_CHUNKS) == D_FF


def _dot(a, b):
    return jnp.dot(a, b, preferred_element_type=F32)


def _layer_norm(z, g, b):
    mu = jnp.mean(z, axis=-1, keepdims=True)
    zc = z - mu
    var = jnp.mean(zc * zc, axis=-1, keepdims=True)
    return zc * lax.rsqrt(var + LN_EPS) * g + b


def _in_proj_kernel(n_cast, x_ref, w_ref, bg_ref, cos_ref, sin_ref, *refs):
    cast_in = refs[:n_cast]
    u_ref, qmt_ref, k_ref, vt_ref, g_ref = refs[n_cast:n_cast + 5]
    cast_refs = cast_in + refs[n_cast + 5:]
    xb = x_ref[...].astype(BF16)
    off_q, off_k, off_v = D_SSM, D_SSM + D_ATTN, D_SSM + 2 * D_ATTN
    off_g = off_v + D_ATTN
    u_ref[...] = _dot(xb, w_ref[:, 0:D_SSM])
    cos = cos_ref[...]
    sin = sin_ref[...]
    lane = lax.broadcasted_iota(jnp.int32, cos.shape, 1)
    low_half = (lane & (HEAD_DIM // 2)) == 0
    def rope(slab):
        partner = jnp.where(low_half,
                            pltpu.roll(slab, LANES - HEAD_DIM // 2, 1),
                            pltpu.roll(slab, HEAD_DIM // 2, 1))
        return slab * cos + partner * sin

    hk = _dot(xb, w_ref[:, off_k:off_k + D_ATTN])
    for hd in range(N_HEADS):
        k_ref[:, hd * LANES:(hd + 1) * LANES] = rope(hk[:, hd * LANES:(hd + 1) * LANES]).astype(BF16)

    q_scale = HEAD_DIM ** -0.5 * math.log2(math.e)
    hq = _dot(xb, w_ref[:, off_q:off_q + D_ATTN])
    for hd in range(N_HEADS):
        rq = rope(hq[:, hd * LANES:(hd + 1) * LANES]) * q_scale
        for t in range(qmt_ref.shape[0]):
            qt = rq[t * ATT_TQ:(t + 1) * ATT_TQ, :].T
            d = lax.broadcasted_iota(jnp.int32, qt.shape, 0)
            qmt_ref[t, hd, :, 0:ATT_TQ] = jnp.where(d < HEAD_DIM, qt, 0.0).astype(BF16)
            qmt_ref[t, hd, :, ATT_TQ:2 * ATT_TQ] = jnp.where(d >= HEAD_DIM, qt, 0.0).astype(BF16)
    v = _dot(xb, w_ref[:, off_v:off_v + D_ATTN])
    ones = jnp.ones((VT_ROWS - HEAD_WIDTH, ATT_TQ), BF16)
    for hd in range(N_HEADS):
        for r in range(vt_ref.shape[1]):
            tile = v[r * ATT_TQ:(r + 1) * ATT_TQ, hd * LANES:(hd + 1) * LANES]
            vt_ref[hd, r, 0:HEAD_WIDTH, :] = tile.T.astype(BF16)
            vt_ref[hd, r, HEAD_WIDTH:VT_ROWS, :] = ones
    g = _dot(xb, w_ref[:, off_g:off_g + 2 * D_MODEL]) + bg_ref[...]
    g_ref[...] = jax.nn.sigmoid(g).astype(BF16)
    for src_ref, dst_ref in zip(cast_refs[:len(cast_refs) // 2], cast_refs[len(cast_refs) // 2:]):
        dst_ref[...] = src_ref[...].astype(BF16)


def _cast_block_rows(n_rows, n_steps):
    bf16_rows = 2 * SUBLANES
    rows = -(-n_rows // n_steps)
    rows = -(-rows // bf16_rows) * bf16_rows
    while n_rows % rows:
        rows += bf16_rows
    return rows


def _in_proj(x, w_in_b, b_gate, cos_t, sin_t, cast_weights):
    bsz, seq, _ = x.shape
    ts = IN_TS
    grid = (bsz, seq // ts)
    n_steps = grid[0] * grid[1]
    row = lambda b, i: (b, i, 0)
    const2 = lambda b, i: (0, 0)
    cast_specs = []
    for w in cast_weights:
        rows = _cast_block_rows(w.shape[0], n_steps)
        last = w.shape[0] // rows - 1
        cast_specs.append(pl.BlockSpec(
            (rows, w.shape[1]),
            lambda b, i, last=last: (jnp.minimum(b * grid[1] + i, last), 0)))
    out_shape = (
        jax.ShapeDtypeStruct((bsz, seq, D_SSM), F32),
        jax.ShapeDtypeStruct((bsz, seq // ATT_TQ, N_HEADS, HEAD_WIDTH, 2 * ATT_TQ), BF16),
        jax.ShapeDtypeStruct((bsz, seq, D_ATTN), BF16),
        jax.ShapeDtypeStruct((bsz, N_HEADS, seq // ATT_TQ, VT_ROWS, ATT_TQ), BF16),
        jax.ShapeDtypeStruct((bsz, seq, 2 * D_MODEL), BF16),
    ) + tuple(jax.ShapeDtypeStruct(w.shape, BF16) for w in cast_weights)
    outs = pl.pallas_call(
        functools.partial(_in_proj_kernel, len(cast_weights)),
        out_shape=out_shape,
        grid=grid,
        in_specs=[
            pl.BlockSpec((None, ts, D_MODEL), row),
            pl.BlockSpec(w_in_b.shape, const2),
            pl.BlockSpec(b_gate.shape, const2),
            pl.BlockSpec((ts, LANES), lambda b, i: (i, 0)),
            pl.BlockSpec((ts, LANES), lambda b, i: (i, 0)),
        ] + cast_specs,
        out_specs=(
            pl.BlockSpec((None, ts, D_SSM), row),
            pl.BlockSpec((None, ts // ATT_TQ, N_HEADS, HEAD_WIDTH, 2 * ATT_TQ),
                         lambda b, i: (b, i, 0, 0, 0)),
            pl.BlockSpec((None, ts, D_ATTN), row),
            pl.BlockSpec((None, N_HEADS, ts // ATT_TQ, VT_ROWS, ATT_TQ),
                         lambda b, i: (b, 0, i, 0, 0)),
            pl.BlockSpec((None, ts, 2 * D_MODEL), row),
        ) + tuple(cast_specs),
        compiler_params=pltpu.CompilerParams(
            dimension_semantics=("arbitrary", "arbitrary"),
            vmem_limit_bytes=VMEM_LIMIT_BYTES),
        name="in_proj",
    )(x, w_in_b, b_gate, cos_t, sin_t, *cast_weights)
    return outs[:5], outs[5:]


def _s5_kernel(u_ref, lre_ref, lim_ref, lstep_ref, bre_ref, bim_ref, cre_ref, cim_ref,
               d_ref, wglu_ref, o_ref,
               are_s, aim_s, bb_s, state_s, ut_s, bu_s, yt_s):
    n_b, n_t, _ = u_ref.shape
    rows = n_b * n_t
    chunk = N_STATES // SSM_SLAB

    @pl.when(pl.program_id(0) == 0)
    def _():
        lr = lre_ref[...]
        li = lim_ref[...]
        dt = jnp.exp(lstep_ref[...])
        mag = jnp.exp(lr * dt)
        ar = mag * jnp.cos(li * dt)
        ai = mag * jnp.sin(li * dt)
        are_s[...] = jnp.broadcast_to(ar, are_s.shape)
        aim_s[...] = jnp.broadcast_to(ai, aim_s.shape)
        nr, ni = ar - 1.0, ai
        den = lr * lr + li * li
        fr = (nr * lr + ni * li) / den
        fi = (ni * lr - nr * li) / den
        for j in range(SSM_SLAB):
            frj = fr[:, j * chunk:(j + 1) * chunk]
            fij = fi[:, j * chunk:(j + 1) * chunk]
            br = bre_ref[j]
            bi = bim_ref[j]
            bb_s[j] = (frj * br - fij * bi).astype(BF16)
            bb_s[SSM_SLAB + j] = (frj * bi + fij * br).astype(BF16)
        state_s[...] = jnp.zeros(state_s.shape, F32)

    for b in range(n_b):
        for j in range(SSM_SLAB):
            ut_s[j, pl.ds(b, n_t, stride=n_b), :] = u_ref[b, :, j * LANES:(j + 1) * LANES]

    for j in range(SSM_SLAB):
        lhs = ut_s[j].astype(BF16)
        bu_s[:, j * chunk:(j + 1) * chunk] = _dot(lhs, bb_s[j])
        bu_s[:, N_STATES + j * chunk:N_STATES + (j + 1) * chunk] = _dot(lhs, bb_s[SSM_SLAB + j])

    for j in range(SSM_SLAB):
        c_re = slice(j * chunk, (j + 1) * chunk)
        c_im = slice(N_STATES + j * chunk, N_STATES + (j + 1) * chunk)
        ar = are_s[:, c_re]
        ai = aim_s[:, c_re]
        sr = state_s[0, :, c_re]
        si = state_s[1, :, c_re]
        for t in range(n_t):
            rows_t = slice(t * n_b, (t + 1) * n_b)
            sr, si = (ar * sr - ai * si + bu_s[rows_t, c_re],
                      ar * si + ai * sr + bu_s[rows_t, c_im])
            bu_s[rows_t, c_re] = sr
            bu_s[rows_t, c_im] = si
        state_s[0, :, c_re] = sr
        state_s[1, :, c_re] = si

        s_re = bu_s[:, c_re].astype(BF16)
        s_im = bu_s[:, c_im].astype(BF16)
        y = (_dot(s_re, cre_ref[j]) - _dot(s_im, cim_ref[j])
             + d_ref[:, j * LANES:(j + 1) * LANES] * ut_s[j])
        yt_s[j] = jax.nn.gelu(y)

    yg = jnp.concatenate([yt_s[j] for j in range(SSM_SLAB)], axis=-1).astype(BF16)
    ga = _dot(yg, wglu_ref[...])
    out = ga[:, :D_SSM] * jax.nn.sigmoid(ga[:, D_SSM:])
    for j in range(SSM_SLAB):
        yt_s[j] = out[:, j * LANES:(j + 1) * LANES]
    for b in range(n_b):
        for j in range(SSM_SLAB):
            o_ref[b, :, j * LANES:(j + 1) * LANES] = (
                yt_s[j, pl.ds(b, n_t, stride=n_b), :].astype(BF16))


def _s5(u, lam_re, lam_im, lstep, b_re_blk, b_im_blk, c_re_blk, c_im_blk, d_skip, w_glu_b):
    bsz, seq, _ = u.shape
    assert bsz == SUBLANES
    n_t = SSM_L
    rows = bsz * n_t
    chunk = N_STATES // SSM_SLAB
    c2 = lambda c: (0, 0)
    c3 = lambda c: (0, 0, 0)
    return pl.pallas_call(
        _s5_kernel,
        out_shape=jax.ShapeDtypeStruct((bsz, seq, D_SSM), BF16),
        grid=(seq // n_t,),
        in_specs=[
            pl.BlockSpec((bsz, n_t, D_SSM), lambda c: (0, c, 0)),
            pl.BlockSpec(lam_re.shape, c2),
            pl.BlockSpec(lam_im.shape, c2),
            pl.BlockSpec(lstep.shape, c2),
            pl.BlockSpec(b_re_blk.shape, c3),
            pl.BlockSpec(b_im_blk.shape, c3),
            pl.BlockSpec(c_re_blk.shape, c3),
            pl.BlockSpec(c_im_blk.shape, c3),
            pl.BlockSpec(d_skip.shape, c2),
            pl.BlockSpec(w_glu_b.shape, c2),
        ],
        out_specs=pl.BlockSpec((bsz, n_t, D_SSM), lambda c: (0, c, 0)),
        scratch_shapes=[
            pltpu.VMEM((bsz, N_STATES), F32),
            pltpu.VMEM((bsz, N_STATES), F32),
            pltpu.VMEM((2 * SSM_SLAB, LANES, chunk), BF16),
            pltpu.VMEM((2, bsz, N_STATES), F32),
            pltpu.VMEM((SSM_SLAB, rows, LANES), F32),
            pltpu.VMEM((rows, 2 * N_STATES), F32),
            pltpu.VMEM((SSM_SLAB, rows, LANES), F32),
        ],
        compiler_params=pltpu.CompilerParams(
            dimension_semantics=("arbitrary",),
            vmem_limit_bytes=VMEM_LIMIT_BYTES),
        name="s5",
    )(u, lam_re, lam_im, lstep, b_re_blk, b_im_blk, c_re_blk, c_im_blk, d_skip, w_glu_b)


def _sublane_allreduce(x, op):
    for shift in (4, 2, 1):
        x = op(x, pltpu.roll(x, shift, 0))
    return x


def _diffattn_kernel(qmt_ref, k_ref, vt_ref, lq1_ref, lk1_ref, lq2_ref, lk2_ref, gain_ref,
                     o_ref, m_s, acc_s, sbuf_s):
    qi = pl.program_id(1)
    tq = o_ref.shape[0]
    tk = tq
    n = 2 * tq
    acc_groups = acc_s.shape[1] // SUBLANES

    n_slots = sbuf_s.shape[0]

    def update(blocks):
        items = [(j, masked, first, h) for j, masked, first in blocks for h in range(N_HEADS)]

        def issue_scores(idx):
            j, _, _, h = items[idx]
            r0 = pl.multiple_of(j * tk, tk)
            kt = k_ref[pl.ds(r0, tk), h * HEAD_WIDTH:(h + 1) * HEAD_WIDTH]
            sbuf_s[idx % n_slots] = _dot(kt, qmt_ref[h])

        def load_scores(idx):
            s = sbuf_s[idx % n_slots]
            if items[idx][1]:
                key = lax.broadcasted_iota(jnp.int32, s.shape, 0)
                qry = lax.broadcasted_iota(jnp.int32, s.shape, 1) & (tq - 1)
                s = jnp.where(key <= qry, s, -jnp.inf)
            return s.reshape(tk // SUBLANES, SUBLANES, n)

        def running_max(idx):
            _, _, first, h = items[idx]
            m_cur = _sublane_allreduce(jnp.max(load_scores(idx), axis=0), jnp.maximum)
            if first:
                m_s[h] = m_cur
                return m_cur, None
            m_prev = m_s[h]
            m_new = jnp.maximum(m_prev, m_cur)
            m_s[h] = m_new
            return m_new, jnp.exp2(m_prev - m_new)

        def accumulate(idx, m_new, alpha):
            j, _, first, h = items[idx]
            p3 = jnp.exp2(load_scores(idx) - m_new[None])
            pv = _dot(vt_ref[h, j], p3.reshape(tk, n).astype(BF16))
            if first:
                acc_s[h] = pv
            else:
                acc3 = acc_s[h].reshape(acc_groups, SUBLANES, n)
                acc_s[h] = (alpha[None] * acc3).reshape(acc_s.shape[1], n) + pv

        for idx in range(min(2, len(items))):
            issue_scores(idx)
        stats = running_max(0)
        for idx in range(len(items)):
            if idx + 2 < len(items):
                issue_scores(idx + 2)
            nxt = running_max(idx + 1) if idx + 1 < len(items) else None
            accumulate(idx, *stats)
            stats = nxt

    @pl.when(qi % 2 == 0)
    def _():
        update([(qi, True, True)])

    @pl.when(qi % 2 == 1)
    def _():
        update([(qi, True, True), (qi - 1, False, False)])

    below = qi - qi % 2

    @pl.when(below % 4 == 2)
    def _():
        update([(below - 2, False, False), (below - 1, False, False)])

    def body(jj, carry):
        update([(4 * jj + t, False, False) for t in range(4)])
        return carry

    lax.fori_loop(0, below // 4, body, 0)

    lam = (jnp.exp(jnp.sum(lq1_ref[...] * lk1_ref[...], keepdims=True))
           - jnp.exp(jnp.sum(lq2_ref[...] * lk2_ref[...], keepdims=True)) + LAMBDA_INIT)
    groups = HEAD_WIDTH // SUBLANES
    for h in range(N_HEADS):
        num3 = acc_s[h, 0:HEAD_WIDTH, :].reshape(groups, SUBLANES, n)
        den = acc_s[h, HEAD_WIDTH:HEAD_WIDTH + SUBLANES, :]
        on3 = num3 / den[None]
        ot3 = on3[:, :, 0:tq] - lam * on3[:, :, tq:n]
        ms = _sublane_allreduce(jnp.sum(ot3 * ot3, axis=0), jnp.add) * (1.0 / HEAD_WIDTH)
        ot3 = ot3 * lax.rsqrt(ms + RMS_EPS)[None]
        o = ot3.reshape(HEAD_WIDTH, tq).T * (gain_ref[...] * (1.0 - LAMBDA_INIT))
        o_ref[:, h * HEAD_WIDTH:(h + 1) * HEAD_WIDTH] = o.astype(BF16)


def _diffattn(qmt, k, vt, lq1, lk1, lq2, lk2, gain):
    bsz, seq, _ = k.shape
    tq = ATT_TQ
    assert tq & (tq - 1) == 0
    n_kv = seq // tq
    grid = (bsz, n_kv)
    c2 = lambda b, i: (0, 0)
    return pl.pallas_call(
        _diffattn_kernel,
        out_shape=jax.ShapeDtypeStruct((bsz, seq, D_ATTN), BF16),
        grid=grid,
        in_specs=[
            pl.BlockSpec((None, None, N_HEADS, HEAD_WIDTH, 2 * tq), lambda b, i: (b, i, 0, 0, 0)),
            pl.BlockSpec((None, seq, D_ATTN), lambda b, i: (b, 0, 0)),
            pl.BlockSpec((None, N_HEADS, n_kv, VT_ROWS, tq), lambda b, i: (b, 0, 0, 0, 0)),
            pl.BlockSpec(lq1.shape, c2),
            pl.BlockSpec(lk1.shape, c2),
            pl.BlockSpec(lq2.shape, c2),
            pl.BlockSpec(lk2.shape, c2),
            pl.BlockSpec(gain.shape, c2),
        ],
        out_specs=pl.BlockSpec((None, tq, D_ATTN), lambda b, i: (b, i, 0)),
        scratch_shapes=[
            pltpu.VMEM((N_HEADS, SUBLANES, 2 * tq), F32),
            pltpu.VMEM((N_HEADS, VT_ROWS, 2 * tq), F32),
            pltpu.VMEM((3, tq, 2 * tq), F32),
        ],
        compiler_params=pltpu.CompilerParams(
            dimension_semantics=("parallel", "arbitrary"),
            vmem_limit_bytes=VMEM_LIMIT_BYTES),
        name="diffattn",
    )(qmt, k, vt, lq1, lk1, lq2, lk2, gain)


def _merge_kernel(x_ref, g_ref, ys_ref, ya_ref, wps_ref, wpa_ref, wo_ref, lng_ref, lnb_ref,
                  o_ref):
    n_sub = x_ref.shape[0] // MERGE_SUB_ROWS
    rows = [slice(r * MERGE_SUB_ROWS, (r + 1) * MERGE_SUB_ROWS) for r in range(n_sub)]

    def project(r):
        return _dot(ys_ref[rows[r], :], wps_ref[...]), _dot(ya_ref[rows[r], :], wpa_ref[...])

    def norm(r, mix):
        z = DEEPNORM_ALPHA * x_ref[rows[r], :] + mix
        o_ref[rows[r], :] = _layer_norm(z, lng_ref[...], lnb_ref[...])

    proj = project(0)
    pending = None
    for r in range(n_sub):
        p_ssm, p_att = proj
        if r + 1 < n_sub:
            proj = project(r + 1)
        merged = (g_ref[rows[r], 0:D_MODEL].astype(F32) * p_ssm
                  + g_ref[rows[r], D_MODEL:2 * D_MODEL].astype(F32) * p_att)
        mix = _dot(merged.astype(BF16), wo_ref[...])
        if pending is not None:
            norm(*pending)
        pending = (r, mix)
    norm(*pending)


def _merge(x2, gates2, ys2, ya2, wps, wpa, wo, ln_g, ln_b):
    n, _ = x2.shape
    tm = MERGE_TM
    row = lambda i: (i, 0)
    c2 = lambda i: (0, 0)
    return pl.pallas_call(
        _merge_kernel,
        out_shape=jax.ShapeDtypeStruct((n, D_MODEL), F32),
        grid=(n // tm,),
        in_specs=[
            pl.BlockSpec((tm, D_MODEL), row),
            pl.BlockSpec((tm, 2 * D_MODEL), row),
            pl.BlockSpec((tm, D_SSM), row),
            pl.BlockSpec((tm, D_ATTN), row),
            pl.BlockSpec(wps.shape, c2),
            pl.BlockSpec(wpa.shape, c2),
            pl.BlockSpec(wo.shape, c2),
            pl.BlockSpec(ln_g.shape, c2),
            pl.BlockSpec(ln_b.shape, c2),
        ],
        out_specs=pl.BlockSpec((tm, D_MODEL), row),
        compiler_params=pltpu.CompilerParams(
            dimension_semantics=("parallel",),
            vmem_limit_bytes=VMEM_LIMIT_BYTES),
        name="merge",
    )(x2, gates2, ys2, ya2, wps, wpa, wo, ln_g, ln_b)


def _ffn_kernel(x_ref, wg_ref, wu_ref, wd_ref, lng_ref, lnb_ref, o_ref):
    n_sub = x_ref.shape[0] // SUB_ROWS
    bounds = [sum(FFN_CHUNKS[:c]) for c in range(len(FFN_CHUNKS) + 1)]
    items = [(r, c) for r in range(n_sub) for c in range(len(FFN_CHUNKS))]
    rows = [slice(r * SUB_ROWS, (r + 1) * SUB_ROWS) for r in range(n_sub)]

    def gate_up(item):
        r, c = item
        xb = x_ref[rows[r], :].astype(BF16)
        cols = slice(bounds[c], bounds[c + 1])
        return _dot(xb, wg_ref[:, cols]), _dot(xb, wu_ref[:, cols])

    def norm(r, ff):
        z = DEEPNORM_ALPHA * x_ref[rows[r], :] + ff
        o_ref[rows[r], :] = _layer_norm(z, lng_ref[...], lnb_ref[...])

    ab = gate_up(items[0])
    ff = None
    pending = None
    for idx, (r, c) in enumerate(items):
        a, b = ab
        if idx + 1 < len(items):
            ab = gate_up(items[idx + 1])
        if pending is not None:
            norm(*pending)
            pending = None
        h = (jax.nn.silu(a) * b).astype(BF16)
        down = _dot(h, wd_ref[bounds[c]:bounds[c + 1], :])
        ff = down if c == 0 else ff + down
        if c == len(FFN_CHUNKS) - 1:
            pending = (r, ff)
    norm(*pending)


def _ffn(x2, wg, wu, wd, ln_g, ln_b):
    n, _ = x2.shape
    tm = FFN_TM
    row = lambda i: (i, 0)
    c2 = lambda i: (0, 0)
    resident = functools.partial(pl.BlockSpec, index_map=c2, pipeline_mode=pl.Buffered(1))
    return pl.pallas_call(
        _ffn_kernel,
        out_shape=jax.ShapeDtypeStruct((n, D_MODEL), F32),
        grid=(n // tm,),
        in_specs=[
            pl.BlockSpec((tm, D_MODEL), row),
            resident(wg.shape),
            resident(wu.shape),
            resident(wd.shape),
            pl.BlockSpec(ln_g.shape, c2),
            pl.BlockSpec(ln_b.shape, c2),
        ],
        out_specs=pl.BlockSpec((tm, D_MODEL), row),
        compiler_params=pltpu.CompilerParams(
            dimension_semantics=("parallel",),
            vmem_limit_bytes=VMEM_LIMIT_BYTES),
        name="ffn",
    )(x2, wg, wu, wd, ln_g, ln_b)


def _rope_tables(seq):
    half = HEAD_DIM // 2
    inv_freq = ROPE_THETA ** (-jnp.arange(0, HEAD_DIM, 2, dtype=F32) / HEAD_DIM)
    ang = jnp.arange(seq, dtype=F32)[:, None] * inv_freq[None, :]
    cos, sin = jnp.cos(ang), jnp.sin(ang)
    reps = LANES // half
    cos_t = jnp.tile(cos, (1, reps))
    sign = jnp.tile(jnp.concatenate([-jnp.ones((half,), F32), jnp.ones((half,), F32)]),
                    LANES // HEAD_DIM)
    sin_t = jnp.tile(sin, (1, reps)) * sign[None, :]
    return cos_t, sin_t


def _block_diag_in(b):
    gps = N_SSM_GROUPS // SSM_SLAB
    bt = jnp.swapaxes(b, 1, 2).reshape(SSM_SLAB, gps, SSM_GROUP, SSM_STATE)
    eye = jnp.eye(gps, dtype=b.dtype)
    blk = bt[:, :, :, None, :] * eye[None, :, None, :, None]
    return blk.reshape(SSM_SLAB, gps * SSM_GROUP, gps * SSM_STATE)


def _block_diag_out(c):
    gps = N_SSM_GROUPS // SSM_SLAB
    ct = jnp.swapaxes(c, 1, 2).reshape(SSM_SLAB, gps, SSM_STATE, SSM_GROUP)
    eye = jnp.eye(gps, dtype=c.dtype)
    blk = ct[:, :, :, None, :] * eye[None, :, None, :, None]
    return blk.reshape(SSM_SLAB, gps * SSM_STATE, gps * SSM_GROUP)


def kernel(x, w_in, b_gate, ssm_lambda_re, ssm_lambda_im, ssm_log_step, ssm_b_re, ssm_b_im, ssm_c_re, ssm_c_im, ssm_d, w_glu, lambda_q1, lambda_k1, lambda_q2, lambda_k2, subln_gain, w_proj_ssm, w_proj_attn, w_out, ln1_g, ln1_b, w_ffn_gate, w_ffn_up, w_ffn_down, ln2_g, ln2_b):
    bsz, seq, _ = x.shape
    cos_t, sin_t = _rope_tables(seq)
    for l in range(DEPTH):
        (u, qmt, k, vt, gates), (w_glu_b, wps_b, wpa_b, wo_b, wg_b, wu_b, wd_b) = _in_proj(
            x, w_in[l].astype(BF16), b_gate[l][None, :], cos_t, sin_t,
            (w_glu[l], w_proj_ssm[l], w_proj_attn[l], w_out[l],
             w_ffn_gate[l], w_ffn_up[l], w_ffn_down[l]))

        lstep = jnp.broadcast_to(ssm_log_step[l][:, None], (N_SSM_GROUPS, SSM_STATE))
        y_ssm = _s5(
            u,
            ssm_lambda_re[l].reshape(1, N_STATES),
            ssm_lambda_im[l].reshape(1, N_STATES),
            lstep.reshape(1, N_STATES),
            _block_diag_in(ssm_b_re[l]), _block_diag_in(ssm_b_im[l]),
            _block_diag_out(ssm_c_re[l]).astype(BF16), _block_diag_out(ssm_c_im[l]).astype(BF16),
            ssm_d[l].reshape(1, D_SSM),
            w_glu_b)

        y_att = _diffattn(qmt, k, vt, lambda_q1[l][None, :], lambda_k1[l][None, :],
                          lambda_q2[l][None, :], lambda_k2[l][None, :], subln_gain[l][None, :])

        n = bsz * seq
        x1 = _merge(x.reshape(n, D_MODEL), gates.reshape(n, 2 * D_MODEL),
                    y_ssm.reshape(n, D_SSM), y_att.reshape(n, D_ATTN),
                    wps_b, wpa_b, wo_b, ln1_g[l][None, :], ln1_b[l][None, :])
        x2 = _ffn(x1, wg_b, wu_b, wd_b, ln2_g[l][None, :], ln2_b[l][None, :])
        x = x2.reshape(bsz, seq, D_MODEL)
    return x
```

```python
import functools
import math

import jax
import jax.numpy as jnp
from jax import lax
from jax.experimental import pallas as pl
from jax.experimental.pallas import tpu as pltpu

F32 = jnp.float32
BF16 = jnp.bfloat16

D_MODEL = 1024
D_SSM = 512
SSM_GROUP = 16
N_SSM_GROUPS = 32
SSM_STATE = 64
N_STATES = N_SSM_GROUPS * SSM_STATE
D_ATTN = 512
HEAD_DIM = 64
HEAD_WIDTH = 2 * HEAD_DIM
N_HEADS = 4
ROPE_THETA = 10000.0
D_FF = 2816
DEPTH = 1
DEEPNORM_ALPHA = (2.0 * DEPTH) ** 0.25
LN_EPS = 1e-5
RMS_EPS = 1e-5
LAMBDA_INIT = 0.8 - 0.6 * math.exp(-0.3 * 0)

LANES = 128
SUBLANES = 8
VMEM_LIMIT_BYTES = 56 * 1024 * 1024

IN_TS = 1024
SSM_L = 128
SSM_SLAB = 4
ATT_TQ = 256
VT_ROWS = HEAD_WIDTH + 16
MERGE_TM = 1024
FFN_TM = 1024
SUB_ROWS = 512
MERGE_SUB_ROWS = 256
MXU_TILE = 256
FFN_CHUNKS = (6 * MXU_TILE, 5 * MXU_TILE)
assert sum(FFN_CHUNKS) == D_FF


def _dot(a, b):
    return jnp.dot(a, b, preferred_element_type=F32)


def _layer_norm(z, g, b):
    mu = jnp.mean(z, axis=-1, keepdims=True)
    zc = z - mu
    var = jnp.mean(zc * zc, axis=-1, keepdims=True)
    return zc * lax.rsqrt(var + LN_EPS) * g + b


def _in_proj_kernel(n_cast, x_ref, w_ref, bg_ref, cos_ref, sin_ref, *refs):
    cast_in = refs[:n_cast]
    u_ref, qmt_ref, k_ref, vt_ref, g_ref = refs[n_cast:n_cast + 5]
    cast_refs = cast_in + refs[n_cast + 5:]
    xb = x_ref[...].astype(BF16)
    off_q, off_k, off_v = D_SSM, D_SSM + D_ATTN, D_SSM + 2 * D_ATTN
    off_g = off_v + D_ATTN
    u_ref[...] = _dot(xb, w_ref[:, 0:D_SSM])
    cos = cos_ref[...]
    sin = sin_ref[...]
    lane = lax.broadcasted_iota(jnp.int32, cos.shape, 1)
    low_half = (lane & (HEAD_DIM // 2)) == 0
    def rope(slab):
        partner = jnp.where(low_half,
                            pltpu.roll(slab, LANES - HEAD_DIM // 2, 1),
                            pltpu.roll(slab, HEAD_DIM // 2, 1))
        return slab * cos + partner * sin

    hk = _dot(xb, w_ref[:, off_k:off_k + D_ATTN])
    for hd in range(N_HEADS):
        k_ref[:, hd * LANES:(hd + 1) * LANES] = rope(hk[:, hd * LANES:(hd + 1) * LANES]).astype(BF16)

    q_scale = HEAD_DIM ** -0.5 * math.log2(math.e)
    hq = _dot(xb, w_ref[:, off_q:off_q + D_ATTN])
    for hd in range(N_HEADS):
        rq = rope(hq[:, hd * LANES:(hd + 1) * LANES]) * q_scale
        for t in range(qmt_ref.shape[0]):
            qt = rq[t * ATT_TQ:(t + 1) * ATT_TQ, :].T
            d = lax.broadcasted_iota(jnp.int32, qt.shape, 0)
            qmt_ref[t, hd, :, 0:ATT_TQ] = jnp.where(d < HEAD_DIM, qt, 0.0).astype(BF16)
            qmt_ref[t, hd, :, ATT_TQ:2 * ATT_TQ] = jnp.where(d >= HEAD_DIM, qt, 0.0).astype(BF16)
    v = _dot(xb, w_ref[:, off_v:off_v + D_ATTN])
    ones = jnp.ones((VT_ROWS - HEAD_WIDTH, ATT_TQ), BF16)
    for hd in range(N_HEADS):
        for r in range(vt_ref.shape[1]):
            tile = v[r * ATT_TQ:(r + 1) * ATT_TQ, hd * LANES:(hd + 1) * LANES]
            vt_ref[hd, r, 0:HEAD_WIDTH, :] = tile.T.astype(BF16)
            vt_ref[hd, r, HEAD_WIDTH:VT_ROWS, :] = ones
    g = _dot(xb, w_ref[:, off_g:off_g + 2 * D_MODEL]) + bg_ref[...]
    g_ref[...] = jax.nn.sigmoid(g).astype(BF16)
    for src_ref, dst_ref in zip(cast_refs[:len(cast_refs) // 2], cast_refs[len(cast_refs) // 2:]):
        dst_ref[...] = src_ref[...].astype(BF16)


def _cast_block_rows(n_rows, n_steps):
    bf16_rows = 2 * SUBLANES
    rows = -(-n_rows // n_steps)
    rows = -(-rows // bf16_rows) * bf16_rows
    while n_rows % rows:
        rows += bf16_rows
    return rows


def _in_proj(x, w_in_b, b_gate, cos_t, sin_t, cast_weights):
    bsz, seq, _ = x.shape
    ts = IN_TS
    grid = (bsz, seq // ts)
    n_steps = grid[0] * grid[1]
    row = lambda b, i: (b, i, 0)
    const2 = lambda b, i: (0, 0)
    cast_specs = []
    for w in cast_weights:
        rows = _cast_block_rows(w.shape[0], n_steps)
        last = w.shape[0] // rows - 1
        cast_specs.append(pl.BlockSpec(
            (rows, w.shape[1]),
            lambda b, i, last=last: (jnp.minimum(b * grid[1] + i, last), 0)))
    out_shape = (
        jax.ShapeDtypeStruct((bsz, seq, D_SSM), F32),
        jax.ShapeDtypeStruct((bsz, seq // ATT_TQ, N_HEADS, HEAD_WIDTH, 2 * ATT_TQ), BF16),
        jax.ShapeDtypeStruct((bsz, seq, D_ATTN), BF16),
        jax.ShapeDtypeStruct((bsz, N_HEADS, seq // ATT_TQ, VT_ROWS, ATT_TQ), BF16),
        jax.ShapeDtypeStruct((bsz, seq, 2 * D_MODEL), BF16),
    ) + tuple(jax.ShapeDtypeStruct(w.shape, BF16) for w in cast_weights)
    outs = pl.pallas_call(
        functools.partial(_in_proj_kernel, len(cast_weights)),
        out_shape=out_shape,
        grid=grid,
        in_specs=[
            pl.BlockSpec((None, ts, D_MODEL), row),
            pl.BlockSpec(w_in_b.shape, const2),
            pl.BlockSpec(b_gate.shape, const2),
            pl.BlockSpec((ts, LANES), lambda b, i: (i, 0)),
            pl.BlockSpec((ts, LANES), lambda b, i: (i, 0)),
        ] + cast_specs,
        out_specs=(
            pl.BlockSpec((None, ts, D_SSM), row),
            pl.BlockSpec((None, ts // ATT_TQ, N_HEADS, HEAD_WIDTH, 2 * ATT_TQ),
                         lambda b, i: (b, i, 0, 0, 0)),
            pl.BlockSpec((None, ts, D_ATTN), row),
            pl.BlockSpec((None, N_HEADS, ts // ATT_TQ, VT_ROWS, ATT_TQ),
                         lambda b, i: (b, 0, i, 0, 0)),
            pl.BlockSpec((None, ts, 2 * D_MODEL), row),
        ) + tuple(cast_specs),
        compiler_params=pltpu.CompilerParams(
            dimension_semantics=("arbitrary", "arbitrary"),
            vmem_limit_bytes=VMEM_LIMIT_BYTES),
        name="in_proj",
    )(x, w_in_b, b_gate, cos_t, sin_t, *cast_weights)
    return outs[:5], outs[5:]


def _s5_kernel(u_ref, lre_ref, lim_ref, lstep_ref, bre_ref, bim_ref, cre_ref, cim_ref,
               d_ref, wglu_ref, o_ref,
               are_s, aim_s, bb_s, state_s, ut_s, bu_s, yt_s):
    n_b, n_t, _ = u_ref.shape
    rows = n_b * n_t
    chunk = N_STATES // SSM_SLAB

    @pl.when(pl.program_id(0) == 0)
    def _():
        lr = lre_ref[...]
        li = lim_ref[...]
        dt = jnp.exp(lstep_ref[...])
        mag = jnp.exp(lr * dt)
        ar = mag * jnp.cos(li * dt)
        ai = mag * jnp.sin(li * dt)
        are_s[...] = jnp.broadcast_to(ar, are_s.shape)
        aim_s[...] = jnp.broadcast_to(ai, aim_s.shape)
        nr, ni = ar - 1.0, ai
        den = lr * lr + li * li
        fr = (nr * lr + ni * li) / den
        fi = (ni * lr - nr * li) / den
        for j in range(SSM_SLAB):
            frj = fr[:, j * chunk:(j + 1) * chunk]
            fij = fi[:, j * chunk:(j + 1) * chunk]
            br = bre_ref[j]
            bi = bim_ref[j]
            bb_s[j] = (frj * br - fij * bi).astype(BF16)
            bb_s[SSM_SLAB + j] = (frj * bi + fij * br).astype(BF16)
        state_s[...] = jnp.zeros(state_s.shape, F32)

    for b in range(n_b):
        for j in range(SSM_SLAB):
            ut_s[j, pl.ds(b, n_t, stride=n_b), :] = u_ref[b, :, j * LANES:(j + 1) * LANES]

    for j in range(SSM_SLAB):
        lhs = ut_s[j].astype(BF16)
        bu_s[:, j * chunk:(j + 1) * chunk] = _dot(lhs, bb_s[j])
        bu_s[:, N_STATES + j * chunk:N_STATES + (j + 1) * chunk] = _dot(lhs, bb_s[SSM_SLAB + j])

    for j in range(SSM_SLAB):
        c_re = slice(j * chunk, (j + 1) * chunk)
        c_im = slice(N_STATES + j * chunk, N_STATES + (j + 1) * chunk)
        ar = are_s[:, c_re]
        ai = aim_s[:, c_re]
        sr = state_s[0, :, c_re]
        si = state_s[1, :, c_re]
        for t in range(n_t):
            rows_t = slice(t * n_b, (t + 1) * n_b)
            sr, si = (ar * sr - ai * si + bu_s[rows_t, c_re],
                      ar * si + ai * sr + bu_s[rows_t, c_im])
            bu_s[rows_t, c_re] = sr
            bu_s[rows_t, c_im] = si
        state_s[0, :, c_re] = sr
        state_s[1, :, c_re] = si

        s_re = bu_s[:, c_re].astype(BF16)
        s_im = bu_s[:, c_im].astype(BF16)
        y = (_dot(s_re, cre_ref[j]) - _dot(s_im, cim_ref[j])
             + d_ref[:, j * LANES:(j + 1) * LANES] * ut_s[j])
        yt_s[j] = jax.nn.gelu(y)

    yg = jnp.concatenate([yt_s[j] for j in range(SSM_SLAB)], axis=-1).astype(BF16)
    ga = _dot(yg, wglu_ref[...])
    out = ga[:, :D_SSM] * jax.nn.sigmoid(ga[:, D_SSM:])
    for j in range(SSM_SLAB):
        yt_s[j] = out[:, j * LANES:(j + 1) * LANES]
    for b in range(n_b):
        for j in range(SSM_SLAB):
            o_ref[b, :, j * LANES:(j + 1) * LANES] = (
                yt_s[j, pl.ds(b, n_t, stride=n_b), :].astype(BF16))


def _s5(u, lam_re, lam_im, lstep, b_re_blk, b_im_blk, c_re_blk, c_im_blk, d_skip, w_glu_b):
    bsz, seq, _ = u.shape
    assert bsz == SUBLANES
    n_t = SSM_L
    rows = bsz * n_t
    chunk = N_STATES // SSM_SLAB
    c2 = lambda c: (0, 0)
    c3 = lambda c: (0, 0, 0)
    return pl.pallas_call(
        _s5_kernel,
        out_shape=jax.ShapeDtypeStruct((bsz, seq, D_SSM), BF16),
        grid=(seq // n_t,),
        in_specs=[
            pl.BlockSpec((bsz, n_t, D_SSM), lambda c: (0, c, 0)),
            pl.BlockSpec(lam_re.shape, c2),
            pl.BlockSpec(lam_im.shape, c2),
            pl.BlockSpec(lstep.shape, c2),
            pl.BlockSpec(b_re_blk.shape, c3),
            pl.BlockSpec(b_im_blk.shape, c3),
            pl.BlockSpec(c_re_blk.shape, c3),
            pl.BlockSpec(c_im_blk.shape, c3),
            pl.BlockSpec(d_skip.shape, c2),
            pl.BlockSpec(w_glu_b.shape, c2),
        ],
        out_specs=pl.BlockSpec((bsz, n_t, D_SSM), lambda c: (0, c, 0)),
        scratch_shapes=[
            pltpu.VMEM((bsz, N_STATES), F32),
            pltpu.VMEM((bsz, N_STATES), F32),
            pltpu.VMEM((2 * SSM_SLAB, LANES, chunk), BF16),
            pltpu.VMEM((2, bsz, N_STATES), F32),
            pltpu.VMEM((SSM_SLAB, rows, LANES), F32),
            pltpu.VMEM((rows, 2 * N_STATES), F32),
            pltpu.VMEM((SSM_SLAB, rows, LANES), F32),
        ],
        compiler_params=pltpu.CompilerParams(
            dimension_semantics=("arbitrary",),
            vmem_limit_bytes=VMEM_LIMIT_BYTES),
        name="s5",
    )(u, lam_re, lam_im, lstep, b_re_blk, b_im_blk, c_re_blk, c_im_blk, d_skip, w_glu_b)


def _sublane_allreduce(x, op):
    for shift in (4, 2, 1):
        x = op(x, pltpu.roll(x, shift, 0))
    return x


def _diffattn_kernel(qmt_ref, k_ref, vt_ref, lq1_ref, lk1_ref, lq2_ref, lk2_ref, gain_ref,
                     o_ref, m_s, acc_s, sbuf_s):
    qi = pl.program_id(1)
    tq = o_ref.shape[0]
    tk = tq
    n = 2 * tq
    acc_groups = acc_s.shape[1] // SUBLANES

    n_slots = sbuf_s.shape[0]

    def update(blocks):
        items = [(j, masked, first, h) for j, masked, first in blocks for h in range(N_HEADS)]

        col_max = {}

        def issue_scores(idx):
            j, masked, _, h = items[idx]
            r0 = pl.multiple_of(j * tk, tk)
            kt = k_ref[pl.ds(r0, tk), h * HEAD_WIDTH:(h + 1) * HEAD_WIDTH]
            s = _dot(kt, qmt_ref[h])
            if masked:
                key = lax.broadcasted_iota(jnp.int32, s.shape, 0)
                qry = lax.broadcasted_iota(jnp.int32, s.shape, 1) & (tq - 1)
                s = jnp.where(key <= qry, s, -jnp.inf)
            sbuf_s[idx % n_slots] = s
            col_max[idx] = jnp.max(s.reshape(tk // SUBLANES, SUBLANES, n), axis=0)

        def load_scores(idx):
            return sbuf_s[idx % n_slots].reshape(tk // SUBLANES, SUBLANES, n)

        def running_max(idx):
            _, _, first, h = items[idx]
            m_cur = _sublane_allreduce(col_max.pop(idx), jnp.maximum)
            if first:
                m_s[h] = m_cur
                return m_cur, None
            m_prev = m_s[h]
            m_new = jnp.maximum(m_prev, m_cur)
            m_s[h] = m_new
            return m_new, jnp.exp2(m_prev - m_new)

        def accumulate(idx, m_new, alpha):
            j, _, first, h = items[idx]
            p3 = jnp.exp2(load_scores(idx) - m_new[None])
            pv = _dot(vt_ref[h, j], p3.reshape(tk, n).astype(BF16))
            if first:
                acc_s[h] = pv
            else:
                acc3 = acc_s[h].reshape(acc_groups, SUBLANES, n)
                acc_s[h] = (alpha[None] * acc3).reshape(acc_s.shape[1], n) + pv

        for idx in range(min(2, len(items))):
            issue_scores(idx)
        stats = running_max(0)
        for idx in range(len(items)):
            if idx + 2 < len(items):
                issue_scores(idx + 2)
            nxt = running_max(idx + 1) if idx + 1 < len(items) else None
            accumulate(idx, *stats)
            stats = nxt

    @pl.when(qi % 2 == 0)
    def _():
        update([(qi, True, True)])

    @pl.when(qi % 2 == 1)
    def _():
        update([(qi, True, True), (qi - 1, False, False)])

    below = qi - qi % 2

    @pl.when(below % 4 == 2)
    def _():
        update([(below - 2, False, False), (below - 1, False, False)])

    def body(jj, carry):
        update([(4 * jj + t, False, False) for t in range(4)])
        return carry

    lax.fori_loop(0, below // 4, body, 0)

    lam = (jnp.exp(jnp.sum(lq1_ref[...] * lk1_ref[...], keepdims=True))
           - jnp.exp(jnp.sum(lq2_ref[...] * lk2_ref[...], keepdims=True)) + LAMBDA_INIT)
    groups = HEAD_WIDTH // SUBLANES
    for h in range(N_HEADS):
        num3 = acc_s[h, 0:HEAD_WIDTH, :].reshape(groups, SUBLANES, n)
        den = acc_s[h, HEAD_WIDTH:HEAD_WIDTH + SUBLANES, :]
        on3 = num3 / den[None]
        ot3 = on3[:, :, 0:tq] - lam * on3[:, :, tq:n]
        ms = _sublane_allreduce(jnp.sum(ot3 * ot3, axis=0), jnp.add) * (1.0 / HEAD_WIDTH)
        ot3 = ot3 * lax.rsqrt(ms + RMS_EPS)[None]
        o = ot3.reshape(HEAD_WIDTH, tq).T * (gain_ref[...] * (1.0 - LAMBDA_INIT))
        o_ref[:, h * HEAD_WIDTH:(h + 1) * HEAD_WIDTH] = o.astype(BF16)


def _diffattn(qmt, k, vt, lq1, lk1, lq2, lk2, gain):
    bsz, seq, _ = k.shape
    tq = ATT_TQ
    assert tq & (tq - 1) == 0
    n_kv = seq // tq
    grid = (bsz, n_kv)
    c2 = lambda b, i: (0, 0)
    return pl.pallas_call(
        _diffattn_kernel,
        out_shape=jax.ShapeDtypeStruct((bsz, seq, D_ATTN), BF16),
        grid=grid,
        in_specs=[
            pl.BlockSpec((None, None, N_HEADS, HEAD_WIDTH, 2 * tq), lambda b, i: (b, i, 0, 0, 0)),
            pl.BlockSpec((None, seq, D_ATTN), lambda b, i: (b, 0, 0)),
            pl.BlockSpec((None, N_HEADS, n_kv, VT_ROWS, tq), lambda b, i: (b, 0, 0, 0, 0)),
            pl.BlockSpec(lq1.shape, c2),
            pl.BlockSpec(lk1.shape, c2),
            pl.BlockSpec(lq2.shape, c2),
            pl.BlockSpec(lk2.shape, c2),
            pl.BlockSpec(gain.shape, c2),
        ],
        out_specs=pl.BlockSpec((None, tq, D_ATTN), lambda b, i: (b, i, 0)),
        scratch_shapes=[
            pltpu.VMEM((N_HEADS, SUBLANES, 2 * tq), F32),
            pltpu.VMEM((N_HEADS, VT_ROWS, 2 * tq), F32),
            pltpu.VMEM((3, tq, 2 * tq), F32),
        ],
        compiler_params=pltpu.CompilerParams(
            dimension_semantics=("parallel", "arbitrary"),
            vmem_limit_bytes=VMEM_LIMIT_BYTES),
        name="diffattn",
    )(qmt, k, vt, lq1, lk1, lq2, lk2, gain)


def _merge_kernel(x_ref, g_ref, ys_ref, ya_ref, wps_ref, wpa_ref, wo_ref, lng_ref, lnb_ref,
                  o_ref):
    n_sub = x_ref.shape[0] // MERGE_SUB_ROWS
    rows = [slice(r * MERGE_SUB_ROWS, (r + 1) * MERGE_SUB_ROWS) for r in range(n_sub)]

    def project(r):
        return _dot(ys_ref[rows[r], :], wps_ref[...]), _dot(ya_ref[rows[r], :], wpa_ref[...])

    def norm(r, mix):
        z = DEEPNORM_ALPHA * x_ref[rows[r], :] + mix
        o_ref[rows[r], :] = _layer_norm(z, lng_ref[...], lnb_ref[...])

    proj = project(0)
    pending = None
    for r in range(n_sub):
        p_ssm, p_att = proj
        if r + 1 < n_sub:
            proj = project(r + 1)
        merged = (g_ref[rows[r], 0:D_MODEL].astype(F32) * p_ssm
                  + g_ref[rows[r], D_MODEL:2 * D_MODEL].astype(F32) * p_att)
        mix = _dot(merged.astype(BF16), wo_ref[...])
        if pending is not None:
            norm(*pending)
        pending = (r, mix)
    norm(*pending)


def _merge(x2, gates2, ys2, ya2, wps, wpa, wo, ln_g, ln_b):
    n, _ = x2.shape
    tm = MERGE_TM
    row = lambda i: (i, 0)
    c2 = lambda i: (0, 0)
    return pl.pallas_call(
        _merge_kernel,
        out_shape=jax.ShapeDtypeStruct((n, D_MODEL), F32),
        grid=(n // tm,),
        in_specs=[
            pl.BlockSpec((tm, D_MODEL), row),
            pl.BlockSpec((tm, 2 * D_MODEL), row),
            pl.BlockSpec((tm, D_SSM), row),
            pl.BlockSpec((tm, D_ATTN), row),
            pl.BlockSpec(wps.shape, c2),
            pl.BlockSpec(wpa.shape, c2),
            pl.BlockSpec(wo.shape, c2),
            pl.BlockSpec(ln_g.shape, c2),
            pl.BlockSpec(ln_b.shape, c2),
        ],
        out_specs=pl.BlockSpec((tm, D_MODEL), row),
        compiler_params=pltpu.CompilerParams(
            dimension_semantics=("parallel",),
            vmem_limit_bytes=VMEM_LIMIT_BYTES),
        name="merge",
    )(x2, gates2, ys2, ya2, wps, wpa, wo, ln_g, ln_b)


def _ffn_kernel(x_ref, wg_ref, wu_ref, wd_ref, lng_ref, lnb_ref, o_ref):
    n_sub = x_ref.shape[0] // SUB_ROWS
    bounds = [sum(FFN_CHUNKS[:c]) for c in range(len(FFN_CHUNKS) + 1)]
    items = [(r, c) for r in range(n_sub) for c in range(len(FFN_CHUNKS))]
    rows = [slice(r * SUB_ROWS, (r + 1) * SUB_ROWS) for r in range(n_sub)]

    def gate_up(item):
        r, c = item
        xb = x_ref[rows[r], :].astype(BF16)
        cols = slice(bounds[c], bounds[c + 1])
        return _dot(xb, wg_ref[:, cols]), _dot(xb, wu_ref[:, cols])

    def norm(r, ff):
        z = DEEPNORM_ALPHA * x_ref[rows[r], :] + ff
        o_ref[rows[r], :] = _layer_norm(z, lng_ref[...], lnb_ref[...])

    ab = gate_up(items[0])
    ff = None
    pending = None
    for idx, (r, c) in enumerate(items):
        a, b = ab
        if idx + 1 < len(items):
            ab = gate_up(items[idx + 1])
        if pending is not None:
            norm(*pending)
            pending = None
        h = (jax.nn.silu(a) * b).astype(BF16)
        down = _dot(h, wd_ref[bounds[c]:bounds[c + 1], :])
        ff = down if c == 0 else ff + down
        if c == len(FFN_CHUNKS) - 1:
            pending = (r, ff)
    norm(*pending)


def _ffn(x2, wg, wu, wd, ln_g, ln_b):
    n, _ = x2.shape
    tm = FFN_TM
    row = lambda i: (i, 0)
    c2 = lambda i: (0, 0)
    resident = functools.partial(pl.BlockSpec, index_map=c2, pipeline_mode=pl.Buffered(1))
    return pl.pallas_call(
        _ffn_kernel,
        out_shape=jax.ShapeDtypeStruct((n, D_MODEL), F32),
        grid=(n // tm,),
        in_specs=[
            pl.BlockSpec((tm, D_MODEL), row),
            resident(wg.shape),
            resident(wu.shape),
            resident(wd.shape),
            pl.BlockSpec(ln_g.shape, c2),
            pl.BlockSpec(ln_b.shape, c2),
        ],
        out_specs=pl.BlockSpec((tm, D_MODEL), row),
        compiler_params=pltpu.CompilerParams(
            dimension_semantics=("parallel",),
            vmem_limit_bytes=VMEM_LIMIT_BYTES),
        name="ffn",
    )(x2, wg, wu, wd, ln_g, ln_b)


def _rope_tables(seq):
    half = HEAD_DIM // 2
    inv_freq = ROPE_THETA ** (-jnp.arange(0, HEAD_DIM, 2, dtype=F32) / HEAD_DIM)
    ang = jnp.arange(seq, dtype=F32)[:, None] * inv_freq[None, :]
    cos, sin = jnp.cos(ang), jnp.sin(ang)
    reps = LANES // half
    cos_t = jnp.tile(cos, (1, reps))
    sign = jnp.tile(jnp.concatenate([-jnp.ones((half,), F32), jnp.ones((half,), F32)]),
                    LANES // HEAD_DIM)
    sin_t = jnp.tile(sin, (1, reps)) * sign[None, :]
    return cos_t, sin_t


def _block_diag_in(b):
    gps = N_SSM_GROUPS // SSM_SLAB
    bt = jnp.swapaxes(b, 1, 2).reshape(SSM_SLAB, gps, SSM_GROUP, SSM_STATE)
    eye = jnp.eye(gps, dtype=b.dtype)
    blk = bt[:, :, :, None, :] * eye[None, :, None, :, None]
    return blk.reshape(SSM_SLAB, gps * SSM_GROUP, gps * SSM_STATE)


def _block_diag_out(c):
    gps = N_SSM_GROUPS // SSM_SLAB
    ct = jnp.swapaxes(c, 1, 2).reshape(SSM_SLAB, gps, SSM_STATE, SSM_GROUP)
    eye = jnp.eye(gps, dtype=c.dtype)
    blk = ct[:, :, :, None, :] * eye[None, :, None, :, None]
    return blk.reshape(SSM_SLAB, gps * SSM_STATE, gps * SSM_GROUP)


def kernel(x, w_in, b_gate, ssm_lambda_re, ssm_lambda_im, ssm_log_step, ssm_b_re, ssm_b_im, ssm_c_re, ssm_c_im, ssm_d, w_glu, lambda_q1, lambda_k1, lambda_q2, lambda_k2, subln_gain, w_proj_ssm, w_proj_attn, w_out, ln1_g, ln1_b, w_ffn_gate, w_ffn_up, w_ffn_down, ln2_g, ln2_b):
    bsz, seq, _ = x.shape
    cos_t, sin_t = _rope_tables(seq)
    for l in range(DEPTH):
        (u, qmt, k, vt, gates), (w_glu_b, wps_b, wpa_b, wo_b, wg_b, wu_b, wd_b) = _in_proj(
            x, w_in[l].astype(BF16), b_gate[l][None, :], cos_t, sin_t,
            (w_glu[l], w_proj_ssm[l], w_proj_attn[l], w_out[l],
             w_ffn_gate[l], w_ffn_up[l], w_ffn_down[l]))

        lstep = jnp.broadcast_to(ssm_log_step[l][:, None], (N_SSM_GROUPS, SSM_STATE))
        y_ssm = _s5(
            u,
            ssm_lambda_re[l].reshape(1, N_STATES),
            ssm_lambda_im[l].reshape(1, N_STATES),
            lstep.reshape(1, N_STATES),
            _block_diag_in(ssm_b_re[l]), _block_diag_in(ssm_b_im[l]),
            _block_diag_out(ssm_c_re[l]).astype(BF16), _block_diag_out(ssm_c_im[l]).astype(BF16),
            ssm_d[l].reshape(1, D_SSM),
            w_glu_b)

        y_att = _diffattn(qmt, k, vt, lambda_q1[l][None, :], lambda_k1[l][None, :],
                          lambda_q2[l][None, :], lambda_k2[l][None, :], subln_gain[l][None, :])

        n = bsz * seq
        x1 = _merge(x.reshape(n, D_MODEL), gates.reshape(n, 2 * D_MODEL),
                    y_ssm.reshape(n, D_SSM), y_att.reshape(n, D_ATTN),
                    wps_b, wpa_b, wo_b, ln1_g[l][None, :], ln1_b[l][None, :])
        x2 = _ffn(x1, wg_b, wu_b, wd_b, ln2_g[l][None, :], ln2_b[l][None, :])
        x = x2.reshape(bsz, seq, D_MODEL)
    return x
```

```python
import functools
import math

import jax
import jax.numpy as jnp
from jax import lax
from jax.experimental import pallas as pl
from jax.experimental.pallas import tpu as pltpu

F32 = jnp.float32
BF16 = jnp.bfloat16

D_MODEL = 1024
D_SSM = 512
SSM_GROUP = 16
N_SSM_GROUPS = 32
SSM_STATE = 64
N_STATES = N_SSM_GROUPS * SSM_STATE
D_ATTN = 512
HEAD_DIM = 64
HEAD_WIDTH = 2 * HEAD_DIM
N_HEADS = 4
ROPE_THETA = 10000.0
D_FF = 2816
DEPTH = 1
DEEPNORM_ALPHA = (2.0 * DEPTH) ** 0.25
LN_EPS = 1e-5
RMS_EPS = 1e-5
LAMBDA_INIT = 0.8 - 0.6 * math.exp(-0.3 * 0)

LANES = 128
SUBLANES = 8
BF16_ROWS = 2 * SUBLANES
MXU_TILE = 256
VMEM_LIMIT_BYTES = 56 * 1024 * 1024

IN_TS = 1024
SSM_L = 128
SSM_SLAB = 4
SCAN_TB = 16
ATT_TQ = 256
VT_ROWS = HEAD_WIDTH + BF16_ROWS
SCORE_SLOTS = 3
MERGE_TM = 1024
FFN_TM = 1024
SUB_ROWS = 512
MERGE_SUB_ROWS = 256
FFN_CHUNKS = (6 * MXU_TILE, 5 * MXU_TILE)
assert sum(FFN_CHUNKS) == D_FF


def _dot(a, b):
    return jnp.dot(a, b, preferred_element_type=F32)


def _layer_norm(z, g, b):
    mu = jnp.mean(z, axis=-1, keepdims=True)
    zc = z - mu
    var = jnp.mean(zc * zc, axis=-1, keepdims=True)
    return zc * lax.rsqrt(var + LN_EPS) * g + b


def _in_proj_kernel(n_cast, x_ref, w_ref, bg_ref, cos_ref, sin_ref, *refs):
    cast_in = refs[:n_cast]
    u_ref, qmt_ref, k_ref, vt_ref, g_ref = refs[n_cast:n_cast + 5]
    cast_refs = cast_in + refs[n_cast + 5:]
    xb = x_ref[...].astype(BF16)
    off_q, off_k, off_v = D_SSM, D_SSM + D_ATTN, D_SSM + 2 * D_ATTN
    off_g = off_v + D_ATTN
    u_ref[...] = _dot(xb, w_ref[:, 0:D_SSM])
    cos = cos_ref[...]
    sin = sin_ref[...]
    lane = lax.broadcasted_iota(jnp.int32, cos.shape, 1)
    low_half = (lane & (HEAD_DIM // 2)) == 0
    def rope(slab):
        partner = jnp.where(low_half,
                            pltpu.roll(slab, LANES - HEAD_DIM // 2, 1),
                            pltpu.roll(slab, HEAD_DIM // 2, 1))
        return slab * cos + partner * sin

    hk = _dot(xb, w_ref[:, off_k:off_k + D_ATTN])
    for hd in range(N_HEADS):
        k_ref[:, hd * LANES:(hd + 1) * LANES] = rope(hk[:, hd * LANES:(hd + 1) * LANES]).astype(BF16)

    q_scale = HEAD_DIM ** -0.5 * math.log2(math.e)
    hq = _dot(xb, w_ref[:, off_q:off_q + D_ATTN])
    for hd in range(N_HEADS):
        rq = rope(hq[:, hd * LANES:(hd + 1) * LANES]) * q_scale
        for t in range(qmt_ref.shape[0]):
            qt = rq[t * ATT_TQ:(t + 1) * ATT_TQ, :].T
            d = lax.broadcasted_iota(jnp.int32, qt.shape, 0)
            qmt_ref[t, hd, :, 0:ATT_TQ] = jnp.where(d < HEAD_DIM, qt, 0.0).astype(BF16)
            qmt_ref[t, hd, :, ATT_TQ:2 * ATT_TQ] = jnp.where(d >= HEAD_DIM, qt, 0.0).astype(BF16)
    v = _dot(xb, w_ref[:, off_v:off_v + D_ATTN])
    ones = jnp.ones((VT_ROWS - HEAD_WIDTH, ATT_TQ), BF16)
    for hd in range(N_HEADS):
        for r in range(vt_ref.shape[1]):
            tile = v[r * ATT_TQ:(r + 1) * ATT_TQ, hd * LANES:(hd + 1) * LANES]
            vt_ref[hd, r, 0:HEAD_WIDTH, :] = tile.T.astype(BF16)
            vt_ref[hd, r, HEAD_WIDTH:VT_ROWS, :] = ones
    g = _dot(xb, w_ref[:, off_g:off_g + 2 * D_MODEL]) + bg_ref[...]
    g_ref[...] = jax.nn.sigmoid(g).astype(BF16)
    for src_ref, dst_ref in zip(cast_refs[:len(cast_refs) // 2], cast_refs[len(cast_refs) // 2:]):
        dst_ref[...] = src_ref[...].astype(BF16)


def _cast_block_rows(n_rows, n_steps):
    rows = -(-n_rows // n_steps)
    rows = -(-rows // BF16_ROWS) * BF16_ROWS
    while n_rows % rows:
        rows += BF16_ROWS
    return rows


def _in_proj(x, w_in_b, b_gate, cos_t, sin_t, cast_weights):
    bsz, seq, _ = x.shape
    ts = IN_TS
    grid = (bsz, seq // ts)
    n_steps = grid[0] * grid[1]
    row = lambda b, i: (b, i, 0)
    const2 = lambda b, i: (0, 0)
    cast_specs = []
    for w in cast_weights:
        rows = _cast_block_rows(w.shape[0], n_steps)
        last = w.shape[0] // rows - 1
        cast_specs.append(pl.BlockSpec(
            (rows, w.shape[1]),
            lambda b, i, last=last: (jnp.minimum(b * grid[1] + i, last), 0)))
    out_shape = (
        jax.ShapeDtypeStruct((bsz, seq, D_SSM), F32),
        jax.ShapeDtypeStruct((bsz, seq // ATT_TQ, N_HEADS, HEAD_WIDTH, 2 * ATT_TQ), BF16),
        jax.ShapeDtypeStruct((bsz, seq, D_ATTN), BF16),
        jax.ShapeDtypeStruct((bsz, N_HEADS, seq // ATT_TQ, VT_ROWS, ATT_TQ), BF16),
        jax.ShapeDtypeStruct((bsz, seq, 2 * D_MODEL), BF16),
    ) + tuple(jax.ShapeDtypeStruct(w.shape, BF16) for w in cast_weights)
    outs = pl.pallas_call(
        functools.partial(_in_proj_kernel, len(cast_weights)),
        out_shape=out_shape,
        grid=grid,
        in_specs=[
            pl.BlockSpec((None, ts, D_MODEL), row),
            pl.BlockSpec(w_in_b.shape, const2),
            pl.BlockSpec(b_gate.shape, const2),
            pl.BlockSpec((ts, LANES), lambda b, i: (i, 0)),
            pl.BlockSpec((ts, LANES), lambda b, i: (i, 0)),
        ] + cast_specs,
        out_specs=(
            pl.BlockSpec((None, ts, D_SSM), row),
            pl.BlockSpec((None, ts // ATT_TQ, N_HEADS, HEAD_WIDTH, 2 * ATT_TQ),
                         lambda b, i: (b, i, 0, 0, 0)),
            pl.BlockSpec((None, ts, D_ATTN), row),
            pl.BlockSpec((None, N_HEADS, ts // ATT_TQ, VT_ROWS, ATT_TQ),
                         lambda b, i: (b, 0, i, 0, 0)),
            pl.BlockSpec((None, ts, 2 * D_MODEL), row),
        ) + tuple(cast_specs),
        compiler_params=pltpu.CompilerParams(
            dimension_semantics=("arbitrary", "arbitrary"),
            vmem_limit_bytes=VMEM_LIMIT_BYTES),
        name="in_proj",
    )(x, w_in_b, b_gate, cos_t, sin_t, *cast_weights)
    return outs[:5], outs[5:]


def _s5_kernel(u_ref, lre_ref, lim_ref, lstep_ref, bre_ref, bim_ref, cre_ref, cim_ref,
               d_ref, wglu_ref, o_ref,
               are_s, aim_s, bb_s, state_s, ut_s, st_s, yt_s):
    n_b, n_t, _ = u_ref.shape
    chunk = N_STATES // SSM_SLAB

    @pl.when(pl.program_id(0) == 0)
    def _():
        lr = lre_ref[...]
        li = lim_ref[...]
        dt = jnp.exp(lstep_ref[...])
        mag = jnp.exp(lr * dt)
        ar = mag * jnp.cos(li * dt)
        ai = mag * jnp.sin(li * dt)
        are_s[...] = jnp.broadcast_to(ar, are_s.shape)
        aim_s[...] = jnp.broadcast_to(ai, aim_s.shape)
        nr, ni = ar - 1.0, ai
        den = lr * lr + li * li
        fr = (nr * lr + ni * li) / den
        fi = (ni * lr - nr * li) / den
        for j in range(SSM_SLAB):
            frj = fr[:, j * chunk:(j + 1) * chunk]
            fij = fi[:, j * chunk:(j + 1) * chunk]
            br = bre_ref[j]
            bi = bim_ref[j]
            bb_s[j] = (frj * br - fij * bi).astype(BF16)
            bb_s[SSM_SLAB + j] = (frj * bi + fij * br).astype(BF16)
        state_s[...] = jnp.zeros(state_s.shape, F32)

    for b in range(n_b):
        for j in range(SSM_SLAB):
            ut_s[j, pl.ds(b, n_t, stride=n_b), :] = u_ref[b, :, j * LANES:(j + 1) * LANES]

    tb_rows = SCAN_TB * n_b
    n_tb = n_t // SCAN_TB
    items = [(j, tb) for j in range(SSM_SLAB) for tb in range(n_tb)]

    def bu_block(item):
        j, tb = item
        lhs = ut_s[j, tb * tb_rows:(tb + 1) * tb_rows, :].astype(BF16)
        return _dot(lhs, bb_s[j]), _dot(lhs, bb_s[SSM_SLAB + j])

    bu = bu_block(items[0])
    for idx, (j, tb) in enumerate(items):
        c_re = slice(j * chunk, (j + 1) * chunk)
        c_im = slice(N_STATES + j * chunk, N_STATES + (j + 1) * chunk)
        bu_re, bu_im = bu
        if idx + 1 < len(items):
            bu = bu_block(items[idx + 1])
        if tb == 0:
            ar = are_s[:, c_re]
            ai = aim_s[:, c_re]
            sr = state_s[0, :, c_re]
            si = state_s[1, :, c_re]
        for t in range(SCAN_TB):
            local = slice(t * n_b, (t + 1) * n_b)
            rows_t = slice(tb * tb_rows + t * n_b, tb * tb_rows + (t + 1) * n_b)
            sr, si = (ar * sr - ai * si + bu_re[local, :],
                      ar * si + ai * sr + bu_im[local, :])
            st_s[rows_t, c_re] = sr
            st_s[rows_t, c_im] = si
        if tb == n_tb - 1:
            state_s[0, :, c_re] = sr
            state_s[1, :, c_re] = si
            s_re = st_s[:, c_re].astype(BF16)
            s_im = st_s[:, c_im].astype(BF16)
            y = (_dot(s_re, cre_ref[j]) - _dot(s_im, cim_ref[j])
                 + d_ref[:, j * LANES:(j + 1) * LANES] * ut_s[j])
            yt_s[j] = jax.nn.gelu(y)

    yg = jnp.concatenate([yt_s[j] for j in range(SSM_SLAB)], axis=-1).astype(BF16)
    ga = _dot(yg, wglu_ref[...])
    out = ga[:, :D_SSM] * jax.nn.sigmoid(ga[:, D_SSM:])
    for j in range(SSM_SLAB):
        yt_s[j] = out[:, j * LANES:(j + 1) * LANES]
    for b in range(n_b):
        for j in range(SSM_SLAB):
            o_ref[b, :, j * LANES:(j + 1) * LANES] = (
                yt_s[j, pl.ds(b, n_t, stride=n_b), :].astype(BF16))


def _s5(u, lam_re, lam_im, lstep, b_re_blk, b_im_blk, c_re_blk, c_im_blk, d_skip, w_glu_b):
    bsz, seq, _ = u.shape
    assert bsz == SUBLANES
    n_t = SSM_L
    rows = bsz * n_t
    chunk = N_STATES // SSM_SLAB
    c2 = lambda c: (0, 0)
    c3 = lambda c: (0, 0, 0)
    return pl.pallas_call(
        _s5_kernel,
        out_shape=jax.ShapeDtypeStruct((bsz, seq, D_SSM), BF16),
        grid=(seq // n_t,),
        in_specs=[
            pl.BlockSpec((bsz, n_t, D_SSM), lambda c: (0, c, 0)),
            pl.BlockSpec(lam_re.shape, c2),
            pl.BlockSpec(lam_im.shape, c2),
            pl.BlockSpec(lstep.shape, c2),
            pl.BlockSpec(b_re_blk.shape, c3),
            pl.BlockSpec(b_im_blk.shape, c3),
            pl.BlockSpec(c_re_blk.shape, c3),
            pl.BlockSpec(c_im_blk.shape, c3),
            pl.BlockSpec(d_skip.shape, c2),
            pl.BlockSpec(w_glu_b.shape, c2),
        ],
        out_specs=pl.BlockSpec((bsz, n_t, D_SSM), lambda c: (0, c, 0)),
        scratch_shapes=[
            pltpu.VMEM((bsz, N_STATES), F32),
            pltpu.VMEM((bsz, N_STATES), F32),
            pltpu.VMEM((2 * SSM_SLAB, LANES, chunk), BF16),
            pltpu.VMEM((2, bsz, N_STATES), F32),
            pltpu.VMEM((SSM_SLAB, rows, LANES), F32),
            pltpu.VMEM((rows, 2 * N_STATES), F32),
            pltpu.VMEM((SSM_SLAB, rows, LANES), F32),
        ],
        compiler_params=pltpu.CompilerParams(
            dimension_semantics=("arbitrary",),
            vmem_limit_bytes=VMEM_LIMIT_BYTES),
        name="s5",
    )(u, lam_re, lam_im, lstep, b_re_blk, b_im_blk, c_re_blk, c_im_blk, d_skip, w_glu_b)


def _sublane_allreduce(x, op):
    for shift in (4, 2, 1):
        x = op(x, pltpu.roll(x, shift, 0))
    return x


def _diffattn_kernel(qmt_ref, k_ref, vt_ref, lq1_ref, lk1_ref, lq2_ref, lk2_ref, gain_ref,
                     o_ref, m_s, acc_s, sbuf_s):
    qi = pl.program_id(1)
    tq = o_ref.shape[0]
    tk = tq
    n = 2 * tq
    acc_groups = acc_s.shape[1] // SUBLANES

    n_slots = sbuf_s.shape[0]

    def update(blocks):
        items = [(j, masked, first, h) for j, masked, first in blocks for h in range(N_HEADS)]

        col_max = {}

        def issue_scores(idx):
            j, masked, _, h = items[idx]
            r0 = pl.multiple_of(j * tk, tk)
            kt = k_ref[pl.ds(r0, tk), h * HEAD_WIDTH:(h + 1) * HEAD_WIDTH]
            s = _dot(kt, qmt_ref[h])
            if masked:
                key = lax.broadcasted_iota(jnp.int32, s.shape, 0)
                qry = lax.broadcasted_iota(jnp.int32, s.shape, 1) & (tq - 1)
                s = jnp.where(key <= qry, s, -jnp.inf)
            sbuf_s[idx % n_slots] = s
            col_max[idx] = jnp.max(s.reshape(tk // SUBLANES, SUBLANES, n), axis=0)

        def load_scores(idx):
            return sbuf_s[idx % n_slots].reshape(tk // SUBLANES, SUBLANES, n)

        def running_max(idx):
            _, _, first, h = items[idx]
            m_cur = _sublane_allreduce(col_max.pop(idx), jnp.maximum)
            if first:
                m_s[h] = m_cur
                return m_cur, None
            m_prev = m_s[h]
            m_new = jnp.maximum(m_prev, m_cur)
            m_s[h] = m_new
            return m_new, jnp.exp2(m_prev - m_new)

        def accumulate(idx, m_new, alpha):
            j, _, first, h = items[idx]
            p3 = jnp.exp2(load_scores(idx) - m_new[None])
            pv = _dot(vt_ref[h, j], p3.reshape(tk, n).astype(BF16))
            if first:
                acc_s[h] = pv
            else:
                acc3 = acc_s[h].reshape(acc_groups, SUBLANES, n)
                acc_s[h] = (alpha[None] * acc3).reshape(acc_s.shape[1], n) + pv

        ahead = n_slots - 1
        for idx in range(min(ahead, len(items))):
            issue_scores(idx)
        stats = running_max(0)
        for idx in range(len(items)):
            if idx + ahead < len(items):
                issue_scores(idx + ahead)
            nxt = running_max(idx + 1) if idx + 1 < len(items) else None
            accumulate(idx, *stats)
            stats = nxt

    @pl.when(qi % 2 == 0)
    def _():
        update([(qi, True, True)])

    @pl.when(qi % 2 == 1)
    def _():
        update([(qi, True, True), (qi - 1, False, False)])

    below = qi - qi % 2

    @pl.when(below % 4 == 2)
    def _():
        update([(below - 2, False, False), (below - 1, False, False)])

    def body(jj, carry):
        update([(4 * jj + t, False, False) for t in range(4)])
        return carry

    lax.fori_loop(0, below // 4, body, 0)

    lam = (jnp.exp(jnp.sum(lq1_ref[...] * lk1_ref[...], keepdims=True))
           - jnp.exp(jnp.sum(lq2_ref[...] * lk2_ref[...], keepdims=True)) + LAMBDA_INIT)
    groups = HEAD_WIDTH // SUBLANES
    for h in range(N_HEADS):
        num3 = acc_s[h, 0:HEAD_WIDTH, :].reshape(groups, SUBLANES, n)
        den = acc_s[h, HEAD_WIDTH:HEAD_WIDTH + SUBLANES, :]
        on3 = num3 / den[None]
        ot3 = on3[:, :, 0:tq] - lam * on3[:, :, tq:n]
        ms = _sublane_allreduce(jnp.sum(ot3 * ot3, axis=0), jnp.add) * (1.0 / HEAD_WIDTH)
        ot3 = ot3 * lax.rsqrt(ms + RMS_EPS)[None]
        o = ot3.reshape(HEAD_WIDTH, tq).T * (gain_ref[...] * (1.0 - LAMBDA_INIT))
        o_ref[:, h * HEAD_WIDTH:(h + 1) * HEAD_WIDTH] = o.astype(BF16)


def _diffattn(qmt, k, vt, lq1, lk1, lq2, lk2, gain):
    bsz, seq, _ = k.shape
    tq = ATT_TQ
    assert tq & (tq - 1) == 0
    n_kv = seq // tq
    grid = (bsz, n_kv)
    c2 = lambda b, i: (0, 0)
    return pl.pallas_call(
        _diffattn_kernel,
        out_shape=jax.ShapeDtypeStruct((bsz, seq, D_ATTN), BF16),
        grid=grid,
        in_specs=[
            pl.BlockSpec((None, None, N_HEADS, HEAD_WIDTH, 2 * tq), lambda b, i: (b, i, 0, 0, 0)),
            pl.BlockSpec((None, seq, D_ATTN), lambda b, i: (b, 0, 0)),
            pl.BlockSpec((None, N_HEADS, n_kv, VT_ROWS, tq), lambda b, i: (b, 0, 0, 0, 0)),
            pl.BlockSpec(lq1.shape, c2),
            pl.BlockSpec(lk1.shape, c2),
            pl.BlockSpec(lq2.shape, c2),
            pl.BlockSpec(lk2.shape, c2),
            pl.BlockSpec(gain.shape, c2),
        ],
        out_specs=pl.BlockSpec((None, tq, D_ATTN), lambda b, i: (b, i, 0)),
        scratch_shapes=[
            pltpu.VMEM((N_HEADS, SUBLANES, 2 * tq), F32),
            pltpu.VMEM((N_HEADS, VT_ROWS, 2 * tq), F32),
            pltpu.VMEM((SCORE_SLOTS, tq, 2 * tq), F32),
        ],
        compiler_params=pltpu.CompilerParams(
            dimension_semantics=("parallel", "arbitrary"),
            vmem_limit_bytes=VMEM_LIMIT_BYTES),
        name="diffattn",
    )(qmt, k, vt, lq1, lk1, lq2, lk2, gain)


def _merge_kernel(x_ref, g_ref, ys_ref, ya_ref, wps_ref, wpa_ref, wo_ref, lng_ref, lnb_ref,
                  o_ref):
    n_sub = x_ref.shape[0] // MERGE_SUB_ROWS
    rows = [slice(r * MERGE_SUB_ROWS, (r + 1) * MERGE_SUB_ROWS) for r in range(n_sub)]

    def project(r):
        return _dot(ys_ref[rows[r], :], wps_ref[...]), _dot(ya_ref[rows[r], :], wpa_ref[...])

    def norm(r, mix):
        z = DEEPNORM_ALPHA * x_ref[rows[r], :] + mix
        o_ref[rows[r], :] = _layer_norm(z, lng_ref[...], lnb_ref[...])

    proj = project(0)
    pending = None
    for r in range(n_sub):
        p_ssm, p_att = proj
        if r + 1 < n_sub:
            proj = project(r + 1)
        merged = (g_ref[rows[r], 0:D_MODEL].astype(F32) * p_ssm
                  + g_ref[rows[r], D_MODEL:2 * D_MODEL].astype(F32) * p_att)
        mix = _dot(merged.astype(BF16), wo_ref[...])
        if pending is not None:
            norm(*pending)
        pending = (r, mix)
    norm(*pending)


def _merge(x2, gates2, ys2, ya2, wps, wpa, wo, ln_g, ln_b):
    n, _ = x2.shape
    tm = MERGE_TM
    row = lambda i: (i, 0)
    c2 = lambda i: (0, 0)
    return pl.pallas_call(
        _merge_kernel,
        out_shape=jax.ShapeDtypeStruct((n, D_MODEL), F32),
        grid=(n // tm,),
        in_specs=[
            pl.BlockSpec((tm, D_MODEL), row),
            pl.BlockSpec((tm, 2 * D_MODEL), row),
            pl.BlockSpec((tm, D_SSM), row),
            pl.BlockSpec((tm, D_ATTN), row),
            pl.BlockSpec(wps.shape, c2),
            pl.BlockSpec(wpa.shape, c2),
            pl.BlockSpec(wo.shape, c2),
            pl.BlockSpec(ln_g.shape, c2),
            pl.BlockSpec(ln_b.shape, c2),
        ],
        out_specs=pl.BlockSpec((tm, D_MODEL), row),
        compiler_params=pltpu.CompilerParams(
            dimension_semantics=("parallel",),
            vmem_limit_bytes=VMEM_LIMIT_BYTES),
        name="merge",
    )(x2, gates2, ys2, ya2, wps, wpa, wo, ln_g, ln_b)


def _ffn_kernel(x_ref, wg_ref, wu_ref, wd_ref, lng_ref, lnb_ref, o_ref):
    n_sub = x_ref.shape[0] // SUB_ROWS
    bounds = [sum(FFN_CHUNKS[:c]) for c in range(len(FFN_CHUNKS) + 1)]
    items = [(r, c) for r in range(n_sub) for c in range(len(FFN_CHUNKS))]
    rows = [slice(r * SUB_ROWS, (r + 1) * SUB_ROWS) for r in range(n_sub)]

    def gate_up(item):
        r, c = item
        xb = x_ref[rows[r], :].astype(BF16)
        cols = slice(bounds[c], bounds[c + 1])
        return _dot(xb, wg_ref[:, cols]), _dot(xb, wu_ref[:, cols])

    def norm(r, ff):
        z = DEEPNORM_ALPHA * x_ref[rows[r], :] + ff
        o_ref[rows[r], :] = _layer_norm(z, lng_ref[...], lnb_ref[...])

    ab = gate_up(items[0])
    ff = None
    pending = None
    for idx, (r, c) in enumerate(items):
        a, b = ab
        if idx + 1 < len(items):
            ab = gate_up(items[idx + 1])
        if pending is not None:
            norm(*pending)
            pending = None
        h = (jax.nn.silu(a) * b).astype(BF16)
        down = _dot(h, wd_ref[bounds[c]:bounds[c + 1], :])
        ff = down if c == 0 else ff + down
        if c == len(FFN_CHUNKS) - 1:
            pending = (r, ff)
    norm(*pending)


def _ffn(x2, wg, wu, wd, ln_g, ln_b):
    n, _ = x2.shape
    tm = FFN_TM
    row = lambda i: (i, 0)
    c2 = lambda i: (0, 0)
    resident = functools.partial(pl.BlockSpec, index_map=c2, pipeline_mode=pl.Buffered(1))
    return pl.pallas_call(
        _ffn_kernel,
        out_shape=jax.ShapeDtypeStruct((n, D_MODEL), F32),
        grid=(n // tm,),
        in_specs=[
            pl.BlockSpec((tm, D_MODEL), row),
            resident(wg.shape),
            resident(wu.shape),
            resident(wd.shape),
            pl.BlockSpec(ln_g.shape, c2),
            pl.BlockSpec(ln_b.shape, c2),
        ],
        out_specs=pl.BlockSpec((tm, D_MODEL), row),
        compiler_params=pltpu.CompilerParams(
            dimension_semantics=("parallel",),
            vmem_limit_bytes=VMEM_LIMIT_BYTES),
        name="ffn",
    )(x2, wg, wu, wd, ln_g, ln_b)


def _rope_tables(seq):
    half = HEAD_DIM // 2
    inv_freq = ROPE_THETA ** (-jnp.arange(0, HEAD_DIM, 2, dtype=F32) / HEAD_DIM)
    ang = jnp.arange(seq, dtype=F32)[:, None] * inv_freq[None, :]
    cos, sin = jnp.cos(ang), jnp.sin(ang)
    reps = LANES // half
    cos_t = jnp.tile(cos, (1, reps))
    sign = jnp.tile(jnp.concatenate([-jnp.ones((half,), F32), jnp.ones((half,), F32)]),
                    LANES // HEAD_DIM)
    sin_t = jnp.tile(sin, (1, reps)) * sign[None, :]
    return cos_t, sin_t


def _block_diag_in(b):
    gps = N_SSM_GROUPS // SSM_SLAB
    bt = jnp.swapaxes(b, 1, 2).reshape(SSM_SLAB, gps, SSM_GROUP, SSM_STATE)
    eye = jnp.eye(gps, dtype=b.dtype)
    blk = bt[:, :, :, None, :] * eye[None, :, None, :, None]
    return blk.reshape(SSM_SLAB, gps * SSM_GROUP, gps * SSM_STATE)


def _block_diag_out(c):
    gps = N_SSM_GROUPS // SSM_SLAB
    ct = jnp.swapaxes(c, 1, 2).reshape(SSM_SLAB, gps, SSM_STATE, SSM_GROUP)
    eye = jnp.eye(gps, dtype=c.dtype)
    blk = ct[:, :, :, None, :] * eye[None, :, None, :, None]
    return blk.reshape(SSM_SLAB, gps * SSM_STATE, gps * SSM_GROUP)


def kernel(x, w_in, b_gate, ssm_lambda_re, ssm_lambda_im, ssm_log_step, ssm_b_re, ssm_b_im, ssm_c_re, ssm_c_im, ssm_d, w_glu, lambda_q1, lambda_k1, lambda_q2, lambda_k2, subln_gain, w_proj_ssm, w_proj_attn, w_out, ln1_g, ln1_b, w_ffn_gate, w_ffn_up, w_ffn_down, ln2_g, ln2_b):
    bsz, seq, _ = x.shape
    cos_t, sin_t = _rope_tables(seq)
    for l in range(DEPTH):
        (u, qmt, k, vt, gates), (w_glu_b, wps_b, wpa_b, wo_b, wg_b, wu_b, wd_b) = _in_proj(
            x, w_in[l].astype(BF16), b_gate[l][None, :], cos_t, sin_t,
            (w_glu[l], w_proj_ssm[l], w_proj_attn[l], w_out[l],
             w_ffn_gate[l], w_ffn_up[l], w_ffn_down[l]))

        lstep = jnp.broadcast_to(ssm_log_step[l][:, None], (N_SSM_GROUPS, SSM_STATE))
        y_ssm = _s5(
            u,
            ssm_lambda_re[l].reshape(1, N_STATES),
            ssm_lambda_im[l].reshape(1, N_STATES),
            lstep.reshape(1, N_STATES),
            _block_diag_in(ssm_b_re[l]), _block_diag_in(ssm_b_im[l]),
            _block_diag_out(ssm_c_re[l]).astype(BF16), _block_diag_out(ssm_c_im[l]).astype(BF16),
            ssm_d[l].reshape(1, D_SSM),
            w_glu_b)

        y_att = _diffattn(qmt, k, vt, lambda_q1[l][None, :], lambda_k1[l][None, :],
                          lambda_q2[l][None, :], lambda_k2[l][None, :], subln_gain[l][None, :])

        n = bsz * seq
        x1 = _merge(x.reshape(n, D_MODEL), gates.reshape(n, 2 * D_MODEL),
                    y_ssm.reshape(n, D_SSM), y_att.reshape(n, D_ATTN),
                    wps_b, wpa_b, wo_b, ln1_g[l][None, :], ln1_b[l][None, :])
        x2 = _ffn(x1, wg_b, wu_b, wd_b, ln2_g[l][None, :], ln2_b[l][None, :])
        x = x2.reshape(bsz, seq, D_MODEL)
    return x
```

```python
import functools
import math

import jax
import jax.numpy as jnp
from jax import lax
from jax.experimental import pallas as pl
from jax.experimental.pallas import tpu as pltpu

F32 = jnp.float32
BF16 = jnp.bfloat16

D_MODEL = 1024
D_SSM = 512
SSM_GROUP = 16
N_SSM_GROUPS = 32
SSM_STATE = 64
N_STATES = N_SSM_GROUPS * SSM_STATE
D_ATTN = 512
HEAD_DIM = 64
HEAD_WIDTH = 2 * HEAD_DIM
N_HEADS = 4
ROPE_THETA = 10000.0
D_FF = 2816
DEPTH = 1
DEEPNORM_ALPHA = (2.0 * DEPTH) ** 0.25
LN_EPS = 1e-5
RMS_EPS = 1e-5
LAMBDA_INIT = 0.8 - 0.6 * math.exp(-0.3 * 0)

LANES = 128
SUBLANES = 8
BF16_ROWS = 2 * SUBLANES
MXU_TILE = 256
VMEM_LIMIT_BYTES = 56 * 1024 * 1024

IN_TS = 1024
SSM_L = 128
SSM_SLAB = 4
SCAN_TB = 16
ATT_TQ = 256
VT_ROWS = HEAD_WIDTH + BF16_ROWS
SCORE_SLOTS = 4
MERGE_TM = 1024
FFN_TM = 1024
SUB_ROWS = 512
MERGE_SUB_ROWS = 256
FFN_CHUNKS = (6 * MXU_TILE, 5 * MXU_TILE)
assert sum(FFN_CHUNKS) == D_FF


def _dot(a, b):
    return jnp.dot(a, b, preferred_element_type=F32)


def _layer_norm(z, g, b):
    mu = jnp.mean(z, axis=-1, keepdims=True)
    zc = z - mu
    var = jnp.mean(zc * zc, axis=-1, keepdims=True)
    return zc * lax.rsqrt(var + LN_EPS) * g + b


def _in_proj_kernel(n_cast, x_ref, w_ref, bg_ref, cos_ref, sin_ref, *refs):
    cast_in = refs[:n_cast]
    u_ref, qmt_ref, k_ref, vt_ref, g_ref = refs[n_cast:n_cast + 5]
    cast_refs = cast_in + refs[n_cast + 5:]
    xb = x_ref[...].astype(BF16)
    off_q, off_k, off_v = D_SSM, D_SSM + D_ATTN, D_SSM + 2 * D_ATTN
    off_g = off_v + D_ATTN
    u_ref[...] = _dot(xb, w_ref[:, 0:D_SSM])
    cos = cos_ref[...]
    sin = sin_ref[...]
    lane = lax.broadcasted_iota(jnp.int32, cos.shape, 1)
    low_half = (lane & (HEAD_DIM // 2)) == 0
    def rope(slab):
        partner = jnp.where(low_half,
                            pltpu.roll(slab, LANES - HEAD_DIM // 2, 1),
                            pltpu.roll(slab, HEAD_DIM // 2, 1))
        return slab * cos + partner * sin

    hk = _dot(xb, w_ref[:, off_k:off_k + D_ATTN])
    for hd in range(N_HEADS):
        k_ref[:, hd * LANES:(hd + 1) * LANES] = rope(hk[:, hd * LANES:(hd + 1) * LANES]).astype(BF16)

    q_scale = HEAD_DIM ** -0.5 * math.log2(math.e)
    hq = _dot(xb, w_ref[:, off_q:off_q + D_ATTN])
    for hd in range(N_HEADS):
        rq = rope(hq[:, hd * LANES:(hd + 1) * LANES]) * q_scale
        for t in range(qmt_ref.shape[0]):
            qt = rq[t * ATT_TQ:(t + 1) * ATT_TQ, :].T
            d = lax.broadcasted_iota(jnp.int32, qt.shape, 0)
            qmt_ref[t, hd, :, 0:ATT_TQ] = jnp.where(d < HEAD_DIM, qt, 0.0).astype(BF16)
            qmt_ref[t, hd, :, ATT_TQ:2 * ATT_TQ] = jnp.where(d >= HEAD_DIM, qt, 0.0).astype(BF16)
    v = _dot(xb, w_ref[:, off_v:off_v + D_ATTN])
    ones = jnp.ones((VT_ROWS - HEAD_WIDTH, ATT_TQ), BF16)
    for hd in range(N_HEADS):
        for r in range(vt_ref.shape[1]):
            tile = v[r * ATT_TQ:(r + 1) * ATT_TQ, hd * LANES:(hd + 1) * LANES]
            vt_ref[hd, r, 0:HEAD_WIDTH, :] = tile.T.astype(BF16)
            vt_ref[hd, r, HEAD_WIDTH:VT_ROWS, :] = ones
    g = _dot(xb, w_ref[:, off_g:off_g + 2 * D_MODEL]) + bg_ref[...]
    g_ref[...] = jax.nn.sigmoid(g).astype(BF16)
    for src_ref, dst_ref in zip(cast_refs[:len(cast_refs) // 2], cast_refs[len(cast_refs) // 2:]):
        dst_ref[...] = src_ref[...].astype(BF16)


def _cast_block_rows(n_rows, n_steps):
    rows = -(-n_rows // n_steps)
    rows = -(-rows // BF16_ROWS) * BF16_ROWS
    while n_rows % rows:
        rows += BF16_ROWS
    return rows


def _in_proj(x, w_in_b, b_gate, cos_t, sin_t, cast_weights):
    bsz, seq, _ = x.shape
    ts = IN_TS
    grid = (bsz, seq // ts)
    n_steps = grid[0] * grid[1]
    row = lambda b, i: (b, i, 0)
    const2 = lambda b, i: (0, 0)
    cast_specs = []
    for w in cast_weights:
        rows = _cast_block_rows(w.shape[0], n_steps)
        last = w.shape[0] // rows - 1
        cast_specs.append(pl.BlockSpec(
            (rows, w.shape[1]),
            lambda b, i, last=last: (jnp.minimum(b * grid[1] + i, last), 0)))
    out_shape = (
        jax.ShapeDtypeStruct((bsz, seq, D_SSM), F32),
        jax.ShapeDtypeStruct((bsz, seq // ATT_TQ, N_HEADS, HEAD_WIDTH, 2 * ATT_TQ), BF16),
        jax.ShapeDtypeStruct((bsz, seq, D_ATTN), BF16),
        jax.ShapeDtypeStruct((bsz, N_HEADS, seq // ATT_TQ, VT_ROWS, ATT_TQ), BF16),
        jax.ShapeDtypeStruct((bsz, seq, 2 * D_MODEL), BF16),
    ) + tuple(jax.ShapeDtypeStruct(w.shape, BF16) for w in cast_weights)
    outs = pl.pallas_call(
        functools.partial(_in_proj_kernel, len(cast_weights)),
        out_shape=out_shape,
        grid=grid,
        in_specs=[
            pl.BlockSpec((None, ts, D_MODEL), row),
            pl.BlockSpec(w_in_b.shape, const2),
            pl.BlockSpec(b_gate.shape, const2),
            pl.BlockSpec((ts, LANES), lambda b, i: (i, 0)),
            pl.BlockSpec((ts, LANES), lambda b, i: (i, 0)),
        ] + cast_specs,
        out_specs=(
            pl.BlockSpec((None, ts, D_SSM), row),
            pl.BlockSpec((None, ts // ATT_TQ, N_HEADS, HEAD_WIDTH, 2 * ATT_TQ),
                         lambda b, i: (b, i, 0, 0, 0)),
            pl.BlockSpec((None, ts, D_ATTN), row),
            pl.BlockSpec((None, N_HEADS, ts // ATT_TQ, VT_ROWS, ATT_TQ),
                         lambda b, i: (b, 0, i, 0, 0)),
            pl.BlockSpec((None, ts, 2 * D_MODEL), row),
        ) + tuple(cast_specs),
        compiler_params=pltpu.CompilerParams(
            dimension_semantics=("arbitrary", "arbitrary"),
            vmem_limit_bytes=VMEM_LIMIT_BYTES),
        name="in_proj",
    )(x, w_in_b, b_gate, cos_t, sin_t, *cast_weights)
    return outs[:5], outs[5:]


def _s5_kernel(u_ref, lre_ref, lim_ref, lstep_ref, bre_ref, bim_ref, cre_ref, cim_ref,
               d_ref, wglu_ref, o_ref,
               are_s, aim_s, bb_s, state_s, ut_s, st_s, yt_s):
    n_b, n_t, _ = u_ref.shape
    chunk = N_STATES // SSM_SLAB

    @pl.when(pl.program_id(0) == 0)
    def _():
        lr = lre_ref[...]
        li = lim_ref[...]
        dt = jnp.exp(lstep_ref[...])
        mag = jnp.exp(lr * dt)
        ar = mag * jnp.cos(li * dt)
        ai = mag * jnp.sin(li * dt)
        are_s[...] = jnp.broadcast_to(ar, are_s.shape)
        aim_s[...] = jnp.broadcast_to(ai, aim_s.shape)
        nr, ni = ar - 1.0, ai
        den = lr * lr + li * li
        fr = (nr * lr + ni * li) / den
        fi = (ni * lr - nr * li) / den
        for j in range(SSM_SLAB):
            frj = fr[:, j * chunk:(j + 1) * chunk]
            fij = fi[:, j * chunk:(j + 1) * chunk]
            br = bre_ref[j]
            bi = bim_ref[j]
            bb_s[j] = (frj * br - fij * bi).astype(BF16)
            bb_s[SSM_SLAB + j] = (frj * bi + fij * br).astype(BF16)
        state_s[...] = jnp.zeros(state_s.shape, F32)

    for b in range(n_b):
        for j in range(SSM_SLAB):
            ut_s[j, pl.ds(b, n_t, stride=n_b), :] = u_ref[b, :, j * LANES:(j + 1) * LANES]

    tb_rows = SCAN_TB * n_b
    n_tb = n_t // SCAN_TB
    items = [(j, tb) for j in range(SSM_SLAB) for tb in range(n_tb)]

    def bu_block(item):
        j, tb = item
        lhs = ut_s[j, tb * tb_rows:(tb + 1) * tb_rows, :].astype(BF16)
        return _dot(lhs, bb_s[j]), _dot(lhs, bb_s[SSM_SLAB + j])

    bu = bu_block(items[0])
    for idx, (j, tb) in enumerate(items):
        c_re = slice(j * chunk, (j + 1) * chunk)
        c_im = slice(N_STATES + j * chunk, N_STATES + (j + 1) * chunk)
        bu_re, bu_im = bu
        if idx + 1 < len(items):
            bu = bu_block(items[idx + 1])
        if tb == 0:
            ar = are_s[:, c_re]
            ai = aim_s[:, c_re]
            sr = state_s[0, :, c_re]
            si = state_s[1, :, c_re]
        for t in range(SCAN_TB):
            local = slice(t * n_b, (t + 1) * n_b)
            rows_t = slice(tb * tb_rows + t * n_b, tb * tb_rows + (t + 1) * n_b)
            sr, si = (ar * sr - ai * si + bu_re[local, :],
                      ar * si + ai * sr + bu_im[local, :])
            st_s[rows_t, c_re] = sr
            st_s[rows_t, c_im] = si
        if tb == n_tb - 1:
            state_s[0, :, c_re] = sr
            state_s[1, :, c_re] = si
            s_re = st_s[:, c_re].astype(BF16)
            s_im = st_s[:, c_im].astype(BF16)
            y = (_dot(s_re, cre_ref[j]) - _dot(s_im, cim_ref[j])
                 + d_ref[:, j * LANES:(j + 1) * LANES] * ut_s[j])
            yt_s[j] = jax.nn.gelu(y)

    yg = jnp.concatenate([yt_s[j] for j in range(SSM_SLAB)], axis=-1).astype(BF16)
    ga = _dot(yg, wglu_ref[...])
    out = ga[:, :D_SSM] * jax.nn.sigmoid(ga[:, D_SSM:])
    for j in range(SSM_SLAB):
        yt_s[j] = out[:, j * LANES:(j + 1) * LANES]
    for b in range(n_b):
        for j in range(SSM_SLAB):
            o_ref[b, :, j * LANES:(j + 1) * LANES] = (
                yt_s[j, pl.ds(b, n_t, stride=n_b), :].astype(BF16))


def _s5(u, lam_re, lam_im, lstep, b_re_blk, b_im_blk, c_re_blk, c_im_blk, d_skip, w_glu_b):
    bsz, seq, _ = u.shape
    assert bsz == SUBLANES
    n_t = SSM_L
    rows = bsz * n_t
    chunk = N_STATES // SSM_SLAB
    c2 = lambda c: (0, 0)
    c3 = lambda c: (0, 0, 0)
    return pl.pallas_call(
        _s5_kernel,
        out_shape=jax.ShapeDtypeStruct((bsz, seq, D_SSM), BF16),
        grid=(seq // n_t,),
        in_specs=[
            pl.BlockSpec((bsz, n_t, D_SSM), lambda c: (0, c, 0)),
            pl.BlockSpec(lam_re.shape, c2),
            pl.BlockSpec(lam_im.shape, c2),
            pl.BlockSpec(lstep.shape, c2),
            pl.BlockSpec(b_re_blk.shape, c3),
            pl.BlockSpec(b_im_blk.shape, c3),
            pl.BlockSpec(c_re_blk.shape, c3),
            pl.BlockSpec(c_im_blk.shape, c3),
            pl.BlockSpec(d_skip.shape, c2),
            pl.BlockSpec(w_glu_b.shape, c2),
        ],
        out_specs=pl.BlockSpec((bsz, n_t, D_SSM), lambda c: (0, c, 0)),
        scratch_shapes=[
            pltpu.VMEM((bsz, N_STATES), F32),
            pltpu.VMEM((bsz, N_STATES), F32),
            pltpu.VMEM((2 * SSM_SLAB, LANES, chunk), BF16),
            pltpu.VMEM((2, bsz, N_STATES), F32),
            pltpu.VMEM((SSM_SLAB, rows, LANES), F32),
            pltpu.VMEM((rows, 2 * N_STATES), F32),
            pltpu.VMEM((SSM_SLAB, rows, LANES), F32),
        ],
        compiler_params=pltpu.CompilerParams(
            dimension_semantics=("arbitrary",),
            vmem_limit_bytes=VMEM_LIMIT_BYTES),
        name="s5",
    )(u, lam_re, lam_im, lstep, b_re_blk, b_im_blk, c_re_blk, c_im_blk, d_skip, w_glu_b)


def _sublane_allreduce(x, op):
    for shift in (4, 2, 1):
        x = op(x, pltpu.roll(x, shift, 0))
    return x


def _diffattn_kernel(qmt_ref, k_ref, vt_ref, lq1_ref, lk1_ref, lq2_ref, lk2_ref, gain_ref,
                     o_ref, m_s, acc_s, sbuf_s):
    qi = pl.program_id(1)
    tq = o_ref.shape[0]
    tk = tq
    n = 2 * tq
    acc_groups = acc_s.shape[1] // SUBLANES

    n_slots = sbuf_s.shape[0]

    def update(blocks):
        items = [(j, masked, first, h) for j, masked, first in blocks for h in range(N_HEADS)]

        col_max = {}

        def issue_scores(idx):
            j, masked, _, h = items[idx]
            r0 = pl.multiple_of(j * tk, tk)
            kt = k_ref[pl.ds(r0, tk), h * HEAD_WIDTH:(h + 1) * HEAD_WIDTH]
            s = _dot(kt, qmt_ref[h])
            if masked:
                key = lax.broadcasted_iota(jnp.int32, s.shape, 0)
                qry = lax.broadcasted_iota(jnp.int32, s.shape, 1) & (tq - 1)
                s = jnp.where(key <= qry, s, -jnp.inf)
            sbuf_s[idx % n_slots] = s
            col_max[idx] = jnp.max(s.reshape(tk // SUBLANES, SUBLANES, n), axis=0)

        def load_scores(idx):
            return sbuf_s[idx % n_slots].reshape(tk // SUBLANES, SUBLANES, n)

        def running_max(idx):
            _, _, first, h = items[idx]
            m_cur = _sublane_allreduce(col_max.pop(idx), jnp.maximum)
            if first:
                m_s[h] = m_cur
                return m_cur, None
            m_prev = m_s[h]
            m_new = jnp.maximum(m_prev, m_cur)
            m_s[h] = m_new
            return m_new, jnp.exp2(m_prev - m_new)

        def accumulate(idx, m_new, alpha):
            j, _, first, h = items[idx]
            p3 = jnp.exp2(load_scores(idx) - m_new[None])
            pv = _dot(vt_ref[h, j], p3.reshape(tk, n).astype(BF16))
            if first:
                acc_s[h] = pv
            else:
                acc3 = acc_s[h].reshape(acc_groups, SUBLANES, n)
                acc_s[h] = (alpha[None] * acc3).reshape(acc_s.shape[1], n) + pv

        ahead = n_slots - 1
        for idx in range(min(ahead, len(items))):
            issue_scores(idx)
        stats = running_max(0)
        for idx in range(len(items)):
            if idx + ahead < len(items):
                issue_scores(idx + ahead)
            nxt = running_max(idx + 1) if idx + 1 < len(items) else None
            accumulate(idx, *stats)
            stats = nxt

    @pl.when(qi % 2 == 0)
    def _():
        update([(qi, True, True)])

    @pl.when(qi % 2 == 1)
    def _():
        update([(qi, True, True), (qi - 1, False, False)])

    below = qi - qi % 2

    @pl.when(below % 4 == 2)
    def _():
        update([(below - 2, False, False), (below - 1, False, False)])

    def body(jj, carry):
        update([(4 * jj + t, False, False) for t in range(4)])
        return carry

    lax.fori_loop(0, below // 4, body, 0)

    lam = (jnp.exp(jnp.sum(lq1_ref[...] * lk1_ref[...], keepdims=True))
           - jnp.exp(jnp.sum(lq2_ref[...] * lk2_ref[...], keepdims=True)) + LAMBDA_INIT)
    groups = HEAD_WIDTH // SUBLANES
    for h in range(N_HEADS):
        num3 = acc_s[h, 0:HEAD_WIDTH, :].reshape(groups, SUBLANES, n)
        den = acc_s[h, HEAD_WIDTH:HEAD_WIDTH + SUBLANES, :]
        on3 = num3 / den[None]
        ot3 = on3[:, :, 0:tq] - lam * on3[:, :, tq:n]
        ms = _sublane_allreduce(jnp.sum(ot3 * ot3, axis=0), jnp.add) * (1.0 / HEAD_WIDTH)
        ot3 = ot3 * lax.rsqrt(ms + RMS_EPS)[None]
        o = ot3.reshape(HEAD_WIDTH, tq).T * (gain_ref[...] * (1.0 - LAMBDA_INIT))
        o_ref[:, h * HEAD_WIDTH:(h + 1) * HEAD_WIDTH] = o.astype(BF16)


def _diffattn(qmt, k, vt, lq1, lk1, lq2, lk2, gain):
    bsz, seq, _ = k.shape
    tq = ATT_TQ
    assert tq & (tq - 1) == 0
    n_kv = seq // tq
    grid = (bsz, n_kv)
    c2 = lambda b, i: (0, 0)
    return pl.pallas_call(
        _diffattn_kernel,
        out_shape=jax.ShapeDtypeStruct((bsz, seq, D_ATTN), BF16),
        grid=grid,
        in_specs=[
            pl.BlockSpec((None, None, N_HEADS, HEAD_WIDTH, 2 * tq), lambda b, i: (b, i, 0, 0, 0)),
            pl.BlockSpec((None, seq, D_ATTN), lambda b, i: (b, 0, 0)),
            pl.BlockSpec((None, N_HEADS, n_kv, VT_ROWS, tq), lambda b, i: (b, 0, 0, 0, 0)),
            pl.BlockSpec(lq1.shape, c2),
            pl.BlockSpec(lk1.shape, c2),
            pl.BlockSpec(lq2.shape, c2),
            pl.BlockSpec(lk2.shape, c2),
            pl.BlockSpec(gain.shape, c2),
        ],
        out_specs=pl.BlockSpec((None, tq, D_ATTN), lambda b, i: (b, i, 0)),
        scratch_shapes=[
            pltpu.VMEM((N_HEADS, SUBLANES, 2 * tq), F32),
            pltpu.VMEM((N_HEADS, VT_ROWS, 2 * tq), F32),
            pltpu.VMEM((SCORE_SLOTS, tq, 2 * tq), F32),
        ],
        compiler_params=pltpu.CompilerParams(
            dimension_semantics=("parallel", "arbitrary"),
            vmem_limit_bytes=VMEM_LIMIT_BYTES),
        name="diffattn",
    )(qmt, k, vt, lq1, lk1, lq2, lk2, gain)


def _merge_kernel(x_ref, g_ref, ys_ref, ya_ref, wps_ref, wpa_ref, wo_ref, lng_ref, lnb_ref,
                  o_ref):
    n_sub = x_ref.shape[0] // MERGE_SUB_ROWS
    rows = [slice(r * MERGE_SUB_ROWS, (r + 1) * MERGE_SUB_ROWS) for r in range(n_sub)]

    def project(r):
        return _dot(ys_ref[rows[r], :], wps_ref[...]), _dot(ya_ref[rows[r], :], wpa_ref[...])

    def norm(r, mix):
        z = DEEPNORM_ALPHA * x_ref[rows[r], :] + mix
        o_ref[rows[r], :] = _layer_norm(z, lng_ref[...], lnb_ref[...])

    proj = project(0)
    pending = None
    for r in range(n_sub):
        p_ssm, p_att = proj
        if r + 1 < n_sub:
            proj = project(r + 1)
        merged = (g_ref[rows[r], 0:D_MODEL].astype(F32) * p_ssm
                  + g_ref[rows[r], D_MODEL:2 * D_MODEL].astype(F32) * p_att)
        mix = _dot(merged.astype(BF16), wo_ref[...])
        if pending is not None:
            norm(*pending)
        pending = (r, mix)
    norm(*pending)


def _merge(x2, gates2, ys2, ya2, wps, wpa, wo, ln_g, ln_b):
    n, _ = x2.shape
    tm = MERGE_TM
    row = lambda i: (i, 0)
    c2 = lambda i: (0, 0)
    return pl.pallas_call(
        _merge_kernel,
        out_shape=jax.ShapeDtypeStruct((n, D_MODEL), F32),
        grid=(n // tm,),
        in_specs=[
            pl.BlockSpec((tm, D_MODEL), row),
            pl.BlockSpec((tm, 2 * D_MODEL), row),
            pl.BlockSpec((tm, D_SSM), row),
            pl.BlockSpec((tm, D_ATTN), row),
            pl.BlockSpec(wps.shape, c2),
            pl.BlockSpec(wpa.shape, c2),
            pl.BlockSpec(wo.shape, c2),
            pl.BlockSpec(ln_g.shape, c2),
            pl.BlockSpec(ln_b.shape, c2),
        ],
        out_specs=pl.BlockSpec((tm, D_MODEL), row),
        compiler_params=pltpu.CompilerParams(
            dimension_semantics=("parallel",),
            vmem_limit_bytes=VMEM_LIMIT_BYTES),
        name="merge",
    )(x2, gates2, ys2, ya2, wps, wpa, wo, ln_g, ln_b)


def _ffn_kernel(x_ref, wg_ref, wu_ref, wd_ref, lng_ref, lnb_ref, o_ref):
    n_sub = x_ref.shape[0] // SUB_ROWS
    bounds = [sum(FFN_CHUNKS[:c]) for c in range(len(FFN_CHUNKS) + 1)]
    items = [(r, c) for r in range(n_sub) for c in range(len(FFN_CHUNKS))]
    rows = [slice(r * SUB_ROWS, (r + 1) * SUB_ROWS) for r in range(n_sub)]

    def gate_up(item):
        r, c = item
        xb = x_ref[rows[r], :].astype(BF16)
        cols = slice(bounds[c], bounds[c + 1])
        return _dot(xb, wg_ref[:, cols]), _dot(xb, wu_ref[:, cols])

    def norm(r, ff):
        z = DEEPNORM_ALPHA * x_ref[rows[r], :] + ff
        o_ref[rows[r], :] = _layer_norm(z, lng_ref[...], lnb_ref[...])

    ab = gate_up(items[0])
    ff = None
    pending = None
    for idx, (r, c) in enumerate(items):
        a, b = ab
        if idx + 1 < len(items):
            ab = gate_up(items[idx + 1])
        if pending is not None:
            norm(*pending)
            pending = None
        h = (jax.nn.silu(a) * b).astype(BF16)
        down = _dot(h, wd_ref[bounds[c]:bounds[c + 1], :])
        ff = down if c == 0 else ff + down
        if c == len(FFN_CHUNKS) - 1:
            pending = (r, ff)
    norm(*pending)


def _ffn(x2, wg, wu, wd, ln_g, ln_b):
    n, _ = x2.shape
    tm = FFN_TM
    row = lambda i: (i, 0)
    c2 = lambda i: (0, 0)
    resident = functools.partial(pl.BlockSpec, index_map=c2, pipeline_mode=pl.Buffered(1))
    return pl.pallas_call(
        _ffn_kernel,
        out_shape=jax.ShapeDtypeStruct((n, D_MODEL), F32),
        grid=(n // tm,),
        in_specs=[
            pl.BlockSpec((tm, D_MODEL), row),
            resident(wg.shape),
            resident(wu.shape),
            resident(wd.shape),
            pl.BlockSpec(ln_g.shape, c2),
            pl.BlockSpec(ln_b.shape, c2),
        ],
        out_specs=pl.BlockSpec((tm, D_MODEL), row),
        compiler_params=pltpu.CompilerParams(
            dimension_semantics=("parallel",),
            vmem_limit_bytes=VMEM_LIMIT_BYTES),
        name="ffn",
    )(x2, wg, wu, wd, ln_g, ln_b)


def _rope_tables(seq):
    half = HEAD_DIM // 2
    inv_freq = ROPE_THETA ** (-jnp.arange(0, HEAD_DIM, 2, dtype=F32) / HEAD_DIM)
    ang = jnp.arange(seq, dtype=F32)[:, None] * inv_freq[None, :]
    cos, sin = jnp.cos(ang), jnp.sin(ang)
    reps = LANES // half
    cos_t = jnp.tile(cos, (1, reps))
    sign = jnp.tile(jnp.concatenate([-jnp.ones((half,), F32), jnp.ones((half,), F32)]),
                    LANES // HEAD_DIM)
    sin_t = jnp.tile(sin, (1, reps)) * sign[None, :]
    return cos_t, sin_t


def _block_diag_in(b):
    gps = N_SSM_GROUPS // SSM_SLAB
    bt = jnp.swapaxes(b, 1, 2).reshape(SSM_SLAB, gps, SSM_GROUP, SSM_STATE)
    eye = jnp.eye(gps, dtype=b.dtype)
    blk = bt[:, :, :, None, :] * eye[None, :, None, :, None]
    return blk.reshape(SSM_SLAB, gps * SSM_GROUP, gps * SSM_STATE)


def _block_diag_out(c):
    gps = N_SSM_GROUPS // SSM_SLAB
    ct = jnp.swapaxes(c, 1, 2).reshape(SSM_SLAB, gps, SSM_STATE, SSM_GROUP)
    eye = jnp.eye(gps, dtype=c.dtype)
    blk = ct[:, :, :, None, :] * eye[None, :, None, :, None]
    return blk.reshape(SSM_SLAB, gps * SSM_STATE, gps * SSM_GROUP)


def kernel(x, w_in, b_gate, ssm_lambda_re, ssm_lambda_im, ssm_log_step, ssm_b_re, ssm_b_im, ssm_c_re, ssm_c_im, ssm_d, w_glu, lambda_q1, lambda_k1, lambda_q2, lambda_k2, subln_gain, w_proj_ssm, w_proj_attn, w_out, ln1_g, ln1_b, w_ffn_gate, w_ffn_up, w_ffn_down, ln2_g, ln2_b):
    bsz, seq, _ = x.shape
    cos_t, sin_t = _rope_tables(seq)
    for l in range(DEPTH):
        (u, qmt, k, vt, gates), (w_glu_b, wps_b, wpa_b, wo_b, wg_b, wu_b, wd_b) = _in_proj(
            x, w_in[l].astype(BF16), b_gate[l][None, :], cos_t, sin_t,
            (w_glu[l], w_proj_ssm[l], w_proj_attn[l], w_out[l],
             w_ffn_gate[l], w_ffn_up[l], w_ffn_down[l]))

        lstep = jnp.broadcast_to(ssm_log_step[l][:, None], (N_SSM_GROUPS, SSM_STATE))
        y_ssm = _s5(
            u,
            ssm_lambda_re[l].reshape(1, N_STATES),
            ssm_lambda_im[l].reshape(1, N_STATES),
            lstep.reshape(1, N_STATES),
            _block_diag_in(ssm_b_re[l]), _block_diag_in(ssm_b_im[l]),
            _block_diag_out(ssm_c_re[l]).astype(BF16), _block_diag_out(ssm_c_im[l]).astype(BF16),
            ssm_d[l].reshape(1, D_SSM),
            w_glu_b)

        y_att = _diffattn(qmt, k, vt, lambda_q1[l][None, :], lambda_k1[l][None, :],
                          lambda_q2[l][None, :], lambda_k2[l][None, :], subln_gain[l][None, :])

        n = bsz * seq
        x1 = _merge(x.reshape(n, D_MODEL), gates.reshape(n, 2 * D_MODEL),
                    y_ssm.reshape(n, D_SSM), y_att.reshape(n, D_ATTN),
                    wps_b, wpa_b, wo_b, ln1_g[l][None, :], ln1_b[l][None, :])
        x2 = _ffn(x1, wg_b, wu_b, wd_b, ln2_g[l][None, :], ln2_b[l][None, :])
        x = x2.reshape(bsz, seq, D_MODEL)
    return x
```

```python
import functools
import math

import jax
import jax.numpy as jnp
from jax import lax
from jax.experimental import pallas as pl
from jax.experimental.pallas import tpu as pltpu

F32 = jnp.float32
BF16 = jnp.bfloat16

D_MODEL = 1024
D_SSM = 512
SSM_GROUP = 16
N_SSM_GROUPS = 32
SSM_STATE = 64
N_STATES = N_SSM_GROUPS * SSM_STATE
D_ATTN = 512
HEAD_DIM = 64
HEAD_WIDTH = 2 * HEAD_DIM
N_HEADS = 4
ROPE_THETA = 10000.0
D_FF = 2816
DEPTH = 1
DEEPNORM_ALPHA = (2.0 * DEPTH) ** 0.25
LN_EPS = 1e-5
RMS_EPS = 1e-5
LAMBDA_INIT = 0.8 - 0.6 * math.exp(-0.3 * 0)

LANES = 128
SUBLANES = 8
BF16_ROWS = 2 * SUBLANES
MXU_TILE = 256
VMEM_LIMIT_BYTES = 56 * 1024 * 1024

IN_TS = 1024
SSM_L = 128
SSM_SLAB = 4
SCAN_TB = 16
ATT_TQ = 256
VT_ROWS = HEAD_WIDTH + BF16_ROWS
SCORE_SLOTS = 5
MAX_AHEAD = 1
MERGE_TM = 1024
FFN_TM = 1024
SUB_ROWS = 512
MERGE_SUB_ROWS = 256
FFN_CHUNKS = (6 * MXU_TILE, 5 * MXU_TILE)
assert sum(FFN_CHUNKS) == D_FF


def _dot(a, b):
    return jnp.dot(a, b, preferred_element_type=F32)


def _layer_norm(z, g, b):
    mu = jnp.mean(z, axis=-1, keepdims=True)
    zc = z - mu
    var = jnp.mean(zc * zc, axis=-1, keepdims=True)
    return zc * lax.rsqrt(var + LN_EPS) * g + b


def _in_proj_kernel(n_cast, x_ref, w_ref, bg_ref, cos_ref, sin_ref, *refs):
    cast_in = refs[:n_cast]
    u_ref, qmt_ref, k_ref, vt_ref, g_ref = refs[n_cast:n_cast + 5]
    cast_refs = cast_in + refs[n_cast + 5:]
    xb = x_ref[...].astype(BF16)
    off_q, off_k, off_v = D_SSM, D_SSM + D_ATTN, D_SSM + 2 * D_ATTN
    off_g = off_v + D_ATTN
    u_ref[...] = _dot(xb, w_ref[:, 0:D_SSM])
    cos = cos_ref[...]
    sin = sin_ref[...]
    lane = lax.broadcasted_iota(jnp.int32, cos.shape, 1)
    low_half = (lane & (HEAD_DIM // 2)) == 0
    def rope(slab):
        partner = jnp.where(low_half,
                            pltpu.roll(slab, LANES - HEAD_DIM // 2, 1),
                            pltpu.roll(slab, HEAD_DIM // 2, 1))
        return slab * cos + partner * sin

    hk = _dot(xb, w_ref[:, off_k:off_k + D_ATTN])
    for hd in range(N_HEADS):
        k_ref[:, hd * LANES:(hd + 1) * LANES] = rope(hk[:, hd * LANES:(hd + 1) * LANES]).astype(BF16)

    q_scale = HEAD_DIM ** -0.5 * math.log2(math.e)
    hq = _dot(xb, w_ref[:, off_q:off_q + D_ATTN])
    for hd in range(N_HEADS):
        rq = rope(hq[:, hd * LANES:(hd + 1) * LANES]) * q_scale
        for t in range(qmt_ref.shape[0]):
            qt = rq[t * ATT_TQ:(t + 1) * ATT_TQ, :].T
            d = lax.broadcasted_iota(jnp.int32, qt.shape, 0)
            qmt_ref[t, hd, :, 0:ATT_TQ] = jnp.where(d < HEAD_DIM, qt, 0.0).astype(BF16)
            qmt_ref[t, hd, :, ATT_TQ:2 * ATT_TQ] = jnp.where(d >= HEAD_DIM, qt, 0.0).astype(BF16)
    v = _dot(xb, w_ref[:, off_v:off_v + D_ATTN])
    ones = jnp.ones((VT_ROWS - HEAD_WIDTH, ATT_TQ), BF16)
    for hd in range(N_HEADS):
        for r in range(vt_ref.shape[1]):
            tile = v[r * ATT_TQ:(r + 1) * ATT_TQ, hd * LANES:(hd + 1) * LANES]
            vt_ref[hd, r, 0:HEAD_WIDTH, :] = tile.T.astype(BF16)
            vt_ref[hd, r, HEAD_WIDTH:VT_ROWS, :] = ones
    g = _dot(xb, w_ref[:, off_g:off_g + 2 * D_MODEL]) + bg_ref[...]
    g_ref[...] = jax.nn.sigmoid(g).astype(BF16)
    for src_ref, dst_ref in zip(cast_refs[:len(cast_refs) // 2], cast_refs[len(cast_refs) // 2:]):
        dst_ref[...] = src_ref[...].astype(BF16)


def _cast_block_rows(n_rows, n_steps):
    rows = -(-n_rows // n_steps)
    rows = -(-rows // BF16_ROWS) * BF16_ROWS
    while n_rows % rows:
        rows += BF16_ROWS
    return rows


def _in_proj(x, w_in_b, b_gate, cos_t, sin_t, cast_weights):
    bsz, seq, _ = x.shape
    ts = IN_TS
    grid = (bsz, seq // ts)
    n_steps = grid[0] * grid[1]
    row = lambda b, i: (b, i, 0)
    const2 = lambda b, i: (0, 0)
    cast_specs = []
    for w in cast_weights:
        rows = _cast_block_rows(w.shape[0], n_steps)
        last = w.shape[0] // rows - 1
        cast_specs.append(pl.BlockSpec(
            (rows, w.shape[1]),
            lambda b, i, last=last: (jnp.minimum(b * grid[1] + i, last), 0)))
    out_shape = (
        jax.ShapeDtypeStruct((bsz, seq, D_SSM), F32),
        jax.ShapeDtypeStruct((bsz, seq // ATT_TQ, N_HEADS, HEAD_WIDTH, 2 * ATT_TQ), BF16),
        jax.ShapeDtypeStruct((bsz, seq, D_ATTN), BF16),
        jax.ShapeDtypeStruct((bsz, N_HEADS, seq // ATT_TQ, VT_ROWS, ATT_TQ), BF16),
        jax.ShapeDtypeStruct((bsz, seq, 2 * D_MODEL), BF16),
    ) + tuple(jax.ShapeDtypeStruct(w.shape, BF16) for w in cast_weights)
    outs = pl.pallas_call(
        functools.partial(_in_proj_kernel, len(cast_weights)),
        out_shape=out_shape,
        grid=grid,
        in_specs=[
            pl.BlockSpec((None, ts, D_MODEL), row),
            pl.BlockSpec(w_in_b.shape, const2),
            pl.BlockSpec(b_gate.shape, const2),
            pl.BlockSpec((ts, LANES), lambda b, i: (i, 0)),
            pl.BlockSpec((ts, LANES), lambda b, i: (i, 0)),
        ] + cast_specs,
        out_specs=(
            pl.BlockSpec((None, ts, D_SSM), row),
            pl.BlockSpec((None, ts // ATT_TQ, N_HEADS, HEAD_WIDTH, 2 * ATT_TQ),
                         lambda b, i: (b, i, 0, 0, 0)),
            pl.BlockSpec((None, ts, D_ATTN), row),
            pl.BlockSpec((None, N_HEADS, ts // ATT_TQ, VT_ROWS, ATT_TQ),
                         lambda b, i: (b, 0, i, 0, 0)),
            pl.BlockSpec((None, ts, 2 * D_MODEL), row),
        ) + tuple(cast_specs),
        compiler_params=pltpu.CompilerParams(
            dimension_semantics=("arbitrary", "arbitrary"),
            vmem_limit_bytes=VMEM_LIMIT_BYTES),
        name="in_proj",
    )(x, w_in_b, b_gate, cos_t, sin_t, *cast_weights)
    return outs[:5], outs[5:]


def _s5_kernel(u_ref, lre_ref, lim_ref, lstep_ref, bre_ref, bim_ref, cre_ref, cim_ref,
               d_ref, wglu_ref, o_ref,
               are_s, aim_s, bb_s, state_s, ut_s, st_s, yt_s):
    n_b, n_t, _ = u_ref.shape
    chunk = N_STATES // SSM_SLAB

    @pl.when(pl.program_id(0) == 0)
    def _():
        lr = lre_ref[...]
        li = lim_ref[...]
        dt = jnp.exp(lstep_ref[...])
        mag = jnp.exp(lr * dt)
        ar = mag * jnp.cos(li * dt)
        ai = mag * jnp.sin(li * dt)
        are_s[...] = jnp.broadcast_to(ar, are_s.shape)
        aim_s[...] = jnp.broadcast_to(ai, aim_s.shape)
        nr, ni = ar - 1.0, ai
        den = lr * lr + li * li
        fr = (nr * lr + ni * li) / den
        fi = (ni * lr - nr * li) / den
        for j in range(SSM_SLAB):
            frj = fr[:, j * chunk:(j + 1) * chunk]
            fij = fi[:, j * chunk:(j + 1) * chunk]
            br = bre_ref[j]
            bi = bim_ref[j]
            bb_s[j] = (frj * br - fij * bi).astype(BF16)
            bb_s[SSM_SLAB + j] = (frj * bi + fij * br).astype(BF16)
        state_s[...] = jnp.zeros(state_s.shape, F32)

    for b in range(n_b):
        for j in range(SSM_SLAB):
            ut_s[j, pl.ds(b, n_t, stride=n_b), :] = u_ref[b, :, j * LANES:(j + 1) * LANES]

    tb_rows = SCAN_TB * n_b
    n_tb = n_t // SCAN_TB
    items = [(j, tb) for j in range(SSM_SLAB) for tb in range(n_tb)]

    def bu_block(item):
        j, tb = item
        lhs = ut_s[j, tb * tb_rows:(tb + 1) * tb_rows, :].astype(BF16)
        return _dot(lhs, bb_s[j]), _dot(lhs, bb_s[SSM_SLAB + j])

    bu = bu_block(items[0])
    for idx, (j, tb) in enumerate(items):
        c_re = slice(j * chunk, (j + 1) * chunk)
        c_im = slice(N_STATES + j * chunk, N_STATES + (j + 1) * chunk)
        bu_re, bu_im = bu
        if idx + 1 < len(items):
            bu = bu_block(items[idx + 1])
        if tb == 0:
            ar = are_s[:, c_re]
            ai = aim_s[:, c_re]
            sr = state_s[0, :, c_re]
            si = state_s[1, :, c_re]
        for t in range(SCAN_TB):
            local = slice(t * n_b, (t + 1) * n_b)
            rows_t = slice(tb * tb_rows + t * n_b, tb * tb_rows + (t + 1) * n_b)
            sr, si = (ar * sr - ai * si + bu_re[local, :],
                      ar * si + ai * sr + bu_im[local, :])
            st_s[rows_t, c_re] = sr
            st_s[rows_t, c_im] = si
        if tb == n_tb - 1:
            state_s[0, :, c_re] = sr
            state_s[1, :, c_re] = si
            s_re = st_s[:, c_re].astype(BF16)
            s_im = st_s[:, c_im].astype(BF16)
            y = (_dot(s_re, cre_ref[j]) - _dot(s_im, cim_ref[j])
                 + d_ref[:, j * LANES:(j + 1) * LANES] * ut_s[j])
            yt_s[j] = jax.nn.gelu(y)

    yg = jnp.concatenate([yt_s[j] for j in range(SSM_SLAB)], axis=-1).astype(BF16)
    ga = _dot(yg, wglu_ref[...])
    out = ga[:, :D_SSM] * jax.nn.sigmoid(ga[:, D_SSM:])
    for j in range(SSM_SLAB):
        yt_s[j] = out[:, j * LANES:(j + 1) * LANES]
    for b in range(n_b):
        for j in range(SSM_SLAB):
            o_ref[b, :, j * LANES:(j + 1) * LANES] = (
                yt_s[j, pl.ds(b, n_t, stride=n_b), :].astype(BF16))


def _s5(u, lam_re, lam_im, lstep, b_re_blk, b_im_blk, c_re_blk, c_im_blk, d_skip, w_glu_b):
    bsz, seq, _ = u.shape
    assert bsz == SUBLANES
    n_t = SSM_L
    rows = bsz * n_t
    chunk = N_STATES // SSM_SLAB
    c2 = lambda c: (0, 0)
    c3 = lambda c: (0, 0, 0)
    return pl.pallas_call(
        _s5_kernel,
        out_shape=jax.ShapeDtypeStruct((bsz, seq, D_SSM), BF16),
        grid=(seq // n_t,),
        in_specs=[
            pl.BlockSpec((bsz, n_t, D_SSM), lambda c: (0, c, 0)),
            pl.BlockSpec(lam_re.shape, c2),
            pl.BlockSpec(lam_im.shape, c2),
            pl.BlockSpec(lstep.shape, c2),
            pl.BlockSpec(b_re_blk.shape, c3),
            pl.BlockSpec(b_im_blk.shape, c3),
            pl.BlockSpec(c_re_blk.shape, c3),
            pl.BlockSpec(c_im_blk.shape, c3),
            pl.BlockSpec(d_skip.shape, c2),
            pl.BlockSpec(w_glu_b.shape, c2),
        ],
        out_specs=pl.BlockSpec((bsz, n_t, D_SSM), lambda c: (0, c, 0)),
        scratch_shapes=[
            pltpu.VMEM((bsz, N_STATES), F32),
            pltpu.VMEM((bsz, N_STATES), F32),
            pltpu.VMEM((2 * SSM_SLAB, LANES, chunk), BF16),
            pltpu.VMEM((2, bsz, N_STATES), F32),
            pltpu.VMEM((SSM_SLAB, rows, LANES), F32),
            pltpu.VMEM((rows, 2 * N_STATES), F32),
            pltpu.VMEM((SSM_SLAB, rows, LANES), F32),
        ],
        compiler_params=pltpu.CompilerParams(
            dimension_semantics=("arbitrary",),
            vmem_limit_bytes=VMEM_LIMIT_BYTES),
        name="s5",
    )(u, lam_re, lam_im, lstep, b_re_blk, b_im_blk, c_re_blk, c_im_blk, d_skip, w_glu_b)


def _sublane_allreduce(x, op):
    for shift in (4, 2, 1):
        x = op(x, pltpu.roll(x, shift, 0))
    return x


def _diffattn_kernel(qmt_ref, k_ref, vt_ref, lq1_ref, lk1_ref, lq2_ref, lk2_ref, gain_ref,
                     o_ref, m_s, acc_s, sbuf_s):
    qi = pl.program_id(1)
    tq = o_ref.shape[0]
    tk = tq
    n = 2 * tq
    acc_groups = acc_s.shape[1] // SUBLANES

    n_slots = sbuf_s.shape[0]

    def update(blocks):
        items = [(j, masked, first, h) for j, masked, first in blocks for h in range(N_HEADS)]

        col_max = {}

        def issue_scores(idx):
            j, masked, _, h = items[idx]
            r0 = pl.multiple_of(j * tk, tk)
            kt = k_ref[pl.ds(r0, tk), h * HEAD_WIDTH:(h + 1) * HEAD_WIDTH]
            s = _dot(kt, qmt_ref[h])
            if masked:
                key = lax.broadcasted_iota(jnp.int32, s.shape, 0)
                qry = lax.broadcasted_iota(jnp.int32, s.shape, 1) & (tq - 1)
                s = jnp.where(key <= qry, s, -jnp.inf)
            sbuf_s[idx % n_slots] = s
            col_max[idx] = jnp.max(s.reshape(tk // SUBLANES, SUBLANES, n), axis=0)

        def load_scores(idx):
            return sbuf_s[idx % n_slots].reshape(tk // SUBLANES, SUBLANES, n)

        def running_max(idx):
            _, _, first, h = items[idx]
            m_cur = _sublane_allreduce(col_max.pop(idx), jnp.maximum)
            if first:
                m_s[h] = m_cur
                return m_cur, None
            m_prev = m_s[h]
            m_new = jnp.maximum(m_prev, m_cur)
            m_s[h] = m_new
            return m_new, jnp.exp2(m_prev - m_new)

        def accumulate(idx, m_new, alpha):
            j, _, first, h = items[idx]
            p3 = jnp.exp2(load_scores(idx) - m_new[None])
            pv = _dot(vt_ref[h, j], p3.reshape(tk, n).astype(BF16))
            if first:
                acc_s[h] = pv
            else:
                acc3 = acc_s[h].reshape(acc_groups, SUBLANES, n)
                acc_s[h] = (alpha[None] * acc3).reshape(acc_s.shape[1], n) + pv

        ahead = n_slots - 1
        for idx in range(min(ahead, len(items))):
            issue_scores(idx)
        stats = {idx: running_max(idx) for idx in range(min(MAX_AHEAD, len(items)))}
        for idx in range(len(items)):
            if idx + ahead < len(items):
                issue_scores(idx + ahead)
            if idx + MAX_AHEAD < len(items):
                stats[idx + MAX_AHEAD] = running_max(idx + MAX_AHEAD)
            accumulate(idx, *stats.pop(idx))

    @pl.when(qi % 2 == 0)
    def _():
        update([(qi, True, True)])

    @pl.when(qi % 2 == 1)
    def _():
        update([(qi, True, True), (qi - 1, False, False)])

    below = qi - qi % 2

    @pl.when(below % 4 == 2)
    def _():
        update([(below - 2, False, False), (below - 1, False, False)])

    def body(jj, carry):
        update([(4 * jj + t, False, False) for t in range(4)])
        return carry

    lax.fori_loop(0, below // 4, body, 0)

    lam = (jnp.exp(jnp.sum(lq1_ref[...] * lk1_ref[...], keepdims=True))
           - jnp.exp(jnp.sum(lq2_ref[...] * lk2_ref[...], keepdims=True)) + LAMBDA_INIT)
    groups = HEAD_WIDTH // SUBLANES
    for h in range(N_HEADS):
        num3 = acc_s[h, 0:HEAD_WIDTH, :].reshape(groups, SUBLANES, n)
        den = acc_s[h, HEAD_WIDTH:HEAD_WIDTH + SUBLANES, :]
        on3 = num3 / den[None]
        ot3 = on3[:, :, 0:tq] - lam * on3[:, :, tq:n]
        ms = _sublane_allreduce(jnp.sum(ot3 * ot3, axis=0), jnp.add) * (1.0 / HEAD_WIDTH)
        ot3 = ot3 * lax.rsqrt(ms + RMS_EPS)[None]
        o = ot3.reshape(HEAD_WIDTH, tq).T * (gain_ref[...] * (1.0 - LAMBDA_INIT))
        o_ref[:, h * HEAD_WIDTH:(h + 1) * HEAD_WIDTH] = o.astype(BF16)


def _diffattn(qmt, k, vt, lq1, lk1, lq2, lk2, gain):
    bsz, seq, _ = k.shape
    tq = ATT_TQ
    assert tq & (tq - 1) == 0
    n_kv = seq // tq
    grid = (bsz, n_kv)
    c2 = lambda b, i: (0, 0)
    return pl.pallas_call(
        _diffattn_kernel,
        out_shape=jax.ShapeDtypeStruct((bsz, seq, D_ATTN), BF16),
        grid=grid,
        in_specs=[
            pl.BlockSpec((None, None, N_HEADS, HEAD_WIDTH, 2 * tq), lambda b, i: (b, i, 0, 0, 0)),
            pl.BlockSpec((None, seq, D_ATTN), lambda b, i: (b, 0, 0)),
            pl.BlockSpec((None, N_HEADS, n_kv, VT_ROWS, tq), lambda b, i: (b, 0, 0, 0, 0)),
            pl.BlockSpec(lq1.shape, c2),
            pl.BlockSpec(lk1.shape, c2),
            pl.BlockSpec(lq2.shape, c2),
            pl.BlockSpec(lk2.shape, c2),
            pl.BlockSpec(gain.shape, c2),
        ],
        out_specs=pl.BlockSpec((None, tq, D_ATTN), lambda b, i: (b, i, 0)),
        scratch_shapes=[
            pltpu.VMEM((N_HEADS, SUBLANES, 2 * tq), F32),
            pltpu.VMEM((N_HEADS, VT_ROWS, 2 * tq), F32),
            pltpu.VMEM((SCORE_SLOTS, tq, 2 * tq), F32),
        ],
        compiler_params=pltpu.CompilerParams(
            dimension_semantics=("parallel", "arbitrary"),
            vmem_limit_bytes=VMEM_LIMIT_BYTES),
        name="diffattn",
    )(qmt, k, vt, lq1, lk1, lq2, lk2, gain)


def _merge_kernel(x_ref, g_ref, ys_ref, ya_ref, wps_ref, wpa_ref, wo_ref, lng_ref, lnb_ref,
                  o_ref):
    n_sub = x_ref.shape[0] // MERGE_SUB_ROWS
    rows = [slice(r * MERGE_SUB_ROWS, (r + 1) * MERGE_SUB_ROWS) for r in range(n_sub)]

    def project(r):
        return _dot(ys_ref[rows[r], :], wps_ref[...]), _dot(ya_ref[rows[r], :], wpa_ref[...])

    def norm(r, mix):
        z = DEEPNORM_ALPHA * x_ref[rows[r], :] + mix
        o_ref[rows[r], :] = _layer_norm(z, lng_ref[...], lnb_ref[...])

    proj = project(0)
    pending = None
    for r in range(n_sub):
        p_ssm, p_att = proj
        if r + 1 < n_sub:
            proj = project(r + 1)
        merged = (g_ref[rows[r], 0:D_MODEL].astype(F32) * p_ssm
                  + g_ref[rows[r], D_MODEL:2 * D_MODEL].astype(F32) * p_att)
        mix = _dot(merged.astype(BF16), wo_ref[...])
        if pending is not None:
            norm(*pending)
        pending = (r, mix)
    norm(*pending)


def _merge(x2, gates2, ys2, ya2, wps, wpa, wo, ln_g, ln_b):
    n, _ = x2.shape
    tm = MERGE_TM
    row = lambda i: (i, 0)
    c2 = lambda i: (0, 0)
    return pl.pallas_call(
        _merge_kernel,
        out_shape=jax.ShapeDtypeStruct((n, D_MODEL), F32),
        grid=(n // tm,),
        in_specs=[
            pl.BlockSpec((tm, D_MODEL), row),
            pl.BlockSpec((tm, 2 * D_MODEL), row),
            pl.BlockSpec((tm, D_SSM), row),
            pl.BlockSpec((tm, D_ATTN), row),
            pl.BlockSpec(wps.shape, c2),
            pl.BlockSpec(wpa.shape, c2),
            pl.BlockSpec(wo.shape, c2),
            pl.BlockSpec(ln_g.shape, c2),
            pl.BlockSpec(ln_b.shape, c2),
        ],
        out_specs=pl.BlockSpec((tm, D_MODEL), row),
        compiler_params=pltpu.CompilerParams(
            dimension_semantics=("parallel",),
            vmem_limit_bytes=VMEM_LIMIT_BYTES),
        name="merge",
    )(x2, gates2, ys2, ya2, wps, wpa, wo, ln_g, ln_b)


def _ffn_kernel(x_ref, wg_ref, wu_ref, wd_ref, lng_ref, lnb_ref, o_ref):
    n_sub = x_ref.shape[0] // SUB_ROWS
    bounds = [sum(FFN_CHUNKS[:c]) for c in range(len(FFN_CHUNKS) + 1)]
    items = [(r, c) for r in range(n_sub) for c in range(len(FFN_CHUNKS))]
    rows = [slice(r * SUB_ROWS, (r + 1) * SUB_ROWS) for r in range(n_sub)]

    def gate_up(item):
        r, c = item
        xb = x_ref[rows[r], :].astype(BF16)
        cols = slice(bounds[c], bounds[c + 1])
        return _dot(xb, wg_ref[:, cols]), _dot(xb, wu_ref[:, cols])

    def norm(r, ff):
        z = DEEPNORM_ALPHA * x_ref[rows[r], :] + ff
        o_ref[rows[r], :] = _layer_norm(z, lng_ref[...], lnb_ref[...])

    ab = gate_up(items[0])
    ff = None
    pending = None
    for idx, (r, c) in enumerate(items):
        a, b = ab
        if idx + 1 < len(items):
            ab = gate_up(items[idx + 1])
        if pending is not None:
            norm(*pending)
            pending = None
        h = (jax.nn.silu(a) * b).astype(BF16)
        down = _dot(h, wd_ref[bounds[c]:bounds[c + 1], :])
        ff = down if c == 0 else ff + down
        if c == len(FFN_CHUNKS) - 1:
            pending = (r, ff)
    norm(*pending)


def _ffn(x2, wg, wu, wd, ln_g, ln_b):
    n, _ = x2.shape
    tm = FFN_TM
    row = lambda i: (i, 0)
    c2 = lambda i: (0, 0)
    resident = functools.partial(pl.BlockSpec, index_map=c2, pipeline_mode=pl.Buffered(1))
    return pl.pallas_call(
        _ffn_kernel,
        out_shape=jax.ShapeDtypeStruct((n, D_MODEL), F32),
        grid=(n // tm,),
        in_specs=[
            pl.BlockSpec((tm, D_MODEL), row),
            resident(wg.shape),
            resident(wu.shape),
            resident(wd.shape),
            pl.BlockSpec(ln_g.shape, c2),
            pl.BlockSpec(ln_b.shape, c2),
        ],
        out_specs=pl.BlockSpec((tm, D_MODEL), row),
        compiler_params=pltpu.CompilerParams(
            dimension_semantics=("parallel",),
            vmem_limit_bytes=VMEM_LIMIT_BYTES),
        name="ffn",
    )(x2, wg, wu, wd, ln_g, ln_b)


def _rope_tables(seq):
    half = HEAD_DIM // 2
    inv_freq = ROPE_THETA ** (-jnp.arange(0, HEAD_DIM, 2, dtype=F32) / HEAD_DIM)
    ang = jnp.arange(seq, dtype=F32)[:, None] * inv_freq[None, :]
    cos, sin = jnp.cos(ang), jnp.sin(ang)
    reps = LANES // half
    cos_t = jnp.tile(cos, (1, reps))
    sign = jnp.tile(jnp.concatenate([-jnp.ones((half,), F32), jnp.ones((half,), F32)]),
                    LANES // HEAD_DIM)
    sin_t = jnp.tile(sin, (1, reps)) * sign[None, :]
    return cos_t, sin_t


def _block_diag_in(b):
    gps = N_SSM_GROUPS // SSM_SLAB
    bt = jnp.swapaxes(b, 1, 2).reshape(SSM_SLAB, gps, SSM_GROUP, SSM_STATE)
    eye = jnp.eye(gps, dtype=b.dtype)
    blk = bt[:, :, :, None, :] * eye[None, :, None, :, None]
    return blk.reshape(SSM_SLAB, gps * SSM_GROUP, gps * SSM_STATE)


def _block_diag_out(c):
    gps = N_SSM_GROUPS // SSM_SLAB
    ct = jnp.swapaxes(c, 1, 2).reshape(SSM_SLAB, gps, SSM_STATE, SSM_GROUP)
    eye = jnp.eye(gps, dtype=c.dtype)
    blk = ct[:, :, :, None, :] * eye[None, :, None, :, None]
    return blk.reshape(SSM_SLAB, gps * SSM_STATE, gps * SSM_GROUP)


def kernel(x, w_in, b_gate, ssm_lambda_re, ssm_lambda_im, ssm_log_step, ssm_b_re, ssm_b_im, ssm_c_re, ssm_c_im, ssm_d, w_glu, lambda_q1, lambda_k1, lambda_q2, lambda_k2, subln_gain, w_proj_ssm, w_proj_attn, w_out, ln1_g, ln1_b, w_ffn_gate, w_ffn_up, w_ffn_down, ln2_g, ln2_b):
    bsz, seq, _ = x.shape
    cos_t, sin_t = _rope_tables(seq)
    for l in range(DEPTH):
        (u, qmt, k, vt, gates), (w_glu_b, wps_b, wpa_b, wo_b, wg_b, wu_b, wd_b) = _in_proj(
            x, w_in[l].astype(BF16), b_gate[l][None, :], cos_t, sin_t,
            (w_glu[l], w_proj_ssm[l], w_proj_attn[l], w_out[l],
             w_ffn_gate[l], w_ffn_up[l], w_ffn_down[l]))

        lstep = jnp.broadcast_to(ssm_log_step[l][:, None], (N_SSM_GROUPS, SSM_STATE))
        y_ssm = _s5(
            u,
            ssm_lambda_re[l].reshape(1, N_STATES),
            ssm_lambda_im[l].reshape(1, N_STATES),
            lstep.reshape(1, N_STATES),
            _block_diag_in(ssm_b_re[l]), _block_diag_in(ssm_b_im[l]),
            _block_diag_out(ssm_c_re[l]).astype(BF16), _block_diag_out(ssm_c_im[l]).astype(BF16),
            ssm_d[l].reshape(1, D_SSM),
            w_glu_b)

        y_att = _diffattn(qmt, k, vt, lambda_q1[l][None, :], lambda_k1[l][None, :],
                          lambda_q2[l][None, :], lambda_k2[l][None, :], subln_gain[l][None, :])

        n = bsz * seq
        x1 = _merge(x.reshape(n, D_MODEL), gates.reshape(n, 2 * D_MODEL),
                    y_ssm.reshape(n, D_SSM), y_att.reshape(n, D_ATTN),
                    wps_b, wpa_b, wo_b, ln1_g[l][None, :], ln1_b[l][None, :])
        x2 = _ffn(x1, wg_b, wu_b, wd_b, ln2_g[l][None, :], ln2_b[l][None, :])
        x = x2.reshape(bsz, seq, D_MODEL)
    return x
```

```python
import functools
import math

import jax
import jax.numpy as jnp
from jax import lax
from jax.experimental import pallas as pl
from jax.experimental.pallas import tpu as pltpu

F32 = jnp.float32
BF16 = jnp.bfloat16

D_MODEL = 1024
D_SSM = 512
SSM_GROUP = 16
N_SSM_GROUPS = 32
SSM_STATE = 64
N_STATES = N_SSM_GROUPS * SSM_STATE
D_ATTN = 512
HEAD_DIM = 64
HEAD_WIDTH = 2 * HEAD_DIM
N_HEADS = 4
ROPE_THETA = 10000.0
D_FF = 2816
DEPTH = 1
DEEPNORM_ALPHA = (2.0 * DEPTH) ** 0.25
LN_EPS = 1e-5
RMS_EPS = 1e-5
LAMBDA_INIT = 0.8 - 0.6 * math.exp(-0.3 * 0)

LANES = 128
SUBLANES = 8
BF16_ROWS = 2 * SUBLANES
MXU_TILE = 256
VMEM_LIMIT_BYTES = 56 * 1024 * 1024

IN_TS = 1024
SSM_L = 128
SSM_SLAB = 4
SCAN_TB = 16
ATT_TQ = 256
VT_ROWS = HEAD_WIDTH + BF16_ROWS
KV_GROUP = 4
SCORE_SLOTS = 4
MAX_AHEAD = 1
MERGE_TM = 1024
FFN_TM = 1024
SUB_ROWS = 512
MERGE_SUB_ROWS = 256
FFN_CHUNKS = (6 * MXU_TILE, 5 * MXU_TILE)
assert sum(FFN_CHUNKS) == D_FF


def _dot(a, b):
    return jnp.dot(a, b, preferred_element_type=F32)


def _layer_norm(z, g, b):
    mu = jnp.mean(z, axis=-1, keepdims=True)
    zc = z - mu
    var = jnp.mean(zc * zc, axis=-1, keepdims=True)
    return zc * lax.rsqrt(var + LN_EPS) * g + b


def _in_proj_kernel(n_cast, x_ref, w_ref, bg_ref, cos_ref, sin_ref, *refs):
    cast_in = refs[:n_cast]
    u_ref, qmt_ref, k_ref, vt_ref, g_ref = refs[n_cast:n_cast + 5]
    cast_refs = cast_in + refs[n_cast + 5:]
    xb = x_ref[...].astype(BF16)
    off_q, off_k, off_v = D_SSM, D_SSM + D_ATTN, D_SSM + 2 * D_ATTN
    off_g = off_v + D_ATTN
    u_ref[...] = _dot(xb, w_ref[:, 0:D_SSM])
    cos = cos_ref[...]
    sin = sin_ref[...]
    lane = lax.broadcasted_iota(jnp.int32, cos.shape, 1)
    low_half = (lane & (HEAD_DIM // 2)) == 0
    def rope(slab):
        partner = jnp.where(low_half,
                            pltpu.roll(slab, LANES - HEAD_DIM // 2, 1),
                            pltpu.roll(slab, HEAD_DIM // 2, 1))
        return slab * cos + partner * sin

    hk = _dot(xb, w_ref[:, off_k:off_k + D_ATTN])
    for hd in range(N_HEADS):
        k_ref[:, hd * LANES:(hd + 1) * LANES] = rope(hk[:, hd * LANES:(hd + 1) * LANES]).astype(BF16)

    q_scale = HEAD_DIM ** -0.5 * math.log2(math.e)
    hq = _dot(xb, w_ref[:, off_q:off_q + D_ATTN])
    for hd in range(N_HEADS):
        rq = rope(hq[:, hd * LANES:(hd + 1) * LANES]) * q_scale
        for t in range(qmt_ref.shape[0]):
            qt = rq[t * ATT_TQ:(t + 1) * ATT_TQ, :].T
            d = lax.broadcasted_iota(jnp.int32, qt.shape, 0)
            qmt_ref[t, hd, :, 0:ATT_TQ] = jnp.where(d < HEAD_DIM, qt, 0.0).astype(BF16)
            qmt_ref[t, hd, :, ATT_TQ:2 * ATT_TQ] = jnp.where(d >= HEAD_DIM, qt, 0.0).astype(BF16)
    v = _dot(xb, w_ref[:, off_v:off_v + D_ATTN])
    ones = jnp.ones((VT_ROWS - HEAD_WIDTH, ATT_TQ), BF16)
    for hd in range(N_HEADS):
        for r in range(vt_ref.shape[1]):
            tile = v[r * ATT_TQ:(r + 1) * ATT_TQ, hd * LANES:(hd + 1) * LANES]
            vt_ref[hd, r, 0:HEAD_WIDTH, :] = tile.T.astype(BF16)
            vt_ref[hd, r, HEAD_WIDTH:VT_ROWS, :] = ones
    g = _dot(xb, w_ref[:, off_g:off_g + 2 * D_MODEL]) + bg_ref[...]
    g_ref[...] = jax.nn.sigmoid(g).astype(BF16)
    for src_ref, dst_ref in zip(cast_refs[:len(cast_refs) // 2], cast_refs[len(cast_refs) // 2:]):
        dst_ref[...] = src_ref[...].astype(BF16)


def _cast_block_rows(n_rows, n_steps):
    rows = -(-n_rows // n_steps)
    rows = -(-rows // BF16_ROWS) * BF16_ROWS
    while n_rows % rows:
        rows += BF16_ROWS
    return rows


def _in_proj(x, w_in_b, b_gate, cos_t, sin_t, cast_weights):
    bsz, seq, _ = x.shape
    ts = IN_TS
    grid = (bsz, seq // ts)
    n_steps = grid[0] * grid[1]
    row = lambda b, i: (b, i, 0)
    const2 = lambda b, i: (0, 0)
    cast_specs = []
    for w in cast_weights:
        rows = _cast_block_rows(w.shape[0], n_steps)
        last = w.shape[0] // rows - 1
        cast_specs.append(pl.BlockSpec(
            (rows, w.shape[1]),
            lambda b, i, last=last: (jnp.minimum(b * grid[1] + i, last), 0)))
    out_shape = (
        jax.ShapeDtypeStruct((bsz, seq, D_SSM), F32),
        jax.ShapeDtypeStruct((bsz, seq // ATT_TQ, N_HEADS, HEAD_WIDTH, 2 * ATT_TQ), BF16),
        jax.ShapeDtypeStruct((bsz, seq, D_ATTN), BF16),
        jax.ShapeDtypeStruct((bsz, N_HEADS, seq // ATT_TQ, VT_ROWS, ATT_TQ), BF16),
        jax.ShapeDtypeStruct((bsz, seq, 2 * D_MODEL), BF16),
    ) + tuple(jax.ShapeDtypeStruct(w.shape, BF16) for w in cast_weights)
    outs = pl.pallas_call(
        functools.partial(_in_proj_kernel, len(cast_weights)),
        out_shape=out_shape,
        grid=grid,
        in_specs=[
            pl.BlockSpec((None, ts, D_MODEL), row),
            pl.BlockSpec(w_in_b.shape, const2),
            pl.BlockSpec(b_gate.shape, const2),
            pl.BlockSpec((ts, LANES), lambda b, i: (i, 0)),
            pl.BlockSpec((ts, LANES), lambda b, i: (i, 0)),
        ] + cast_specs,
        out_specs=(
            pl.BlockSpec((None, ts, D_SSM), row),
            pl.BlockSpec((None, ts // ATT_TQ, N_HEADS, HEAD_WIDTH, 2 * ATT_TQ),
                         lambda b, i: (b, i, 0, 0, 0)),
            pl.BlockSpec((None, ts, D_ATTN), row),
            pl.BlockSpec((None, N_HEADS, ts // ATT_TQ, VT_ROWS, ATT_TQ),
                         lambda b, i: (b, 0, i, 0, 0)),
            pl.BlockSpec((None, ts, 2 * D_MODEL), row),
        ) + tuple(cast_specs),
        compiler_params=pltpu.CompilerParams(
            dimension_semantics=("arbitrary", "arbitrary"),
            vmem_limit_bytes=VMEM_LIMIT_BYTES),
        name="in_proj",
    )(x, w_in_b, b_gate, cos_t, sin_t, *cast_weights)
    return outs[:5], outs[5:]


def _s5_kernel(u_ref, lre_ref, lim_ref, lstep_ref, bre_ref, bim_ref, cre_ref, cim_ref,
               d_ref, wglu_ref, o_ref,
               are_s, aim_s, bb_s, state_s, ut_s, st_s, yt_s):
    n_b, n_t, _ = u_ref.shape
    chunk = N_STATES // SSM_SLAB

    @pl.when(pl.program_id(0) == 0)
    def _():
        lr = lre_ref[...]
        li = lim_ref[...]
        dt = jnp.exp(lstep_ref[...])
        mag = jnp.exp(lr * dt)
        ar = mag * jnp.cos(li * dt)
        ai = mag * jnp.sin(li * dt)
        are_s[...] = jnp.broadcast_to(ar, are_s.shape)
        aim_s[...] = jnp.broadcast_to(ai, aim_s.shape)
        nr, ni = ar - 1.0, ai
        den = lr * lr + li * li
        fr = (nr * lr + ni * li) / den
        fi = (ni * lr - nr * li) / den
        for j in range(SSM_SLAB):
            frj = fr[:, j * chunk:(j + 1) * chunk]
            fij = fi[:, j * chunk:(j + 1) * chunk]
            br = bre_ref[j]
            bi = bim_ref[j]
            bb_s[j] = (frj * br - fij * bi).astype(BF16)
            bb_s[SSM_SLAB + j] = (frj * bi + fij * br).astype(BF16)
        state_s[...] = jnp.zeros(state_s.shape, F32)

    for b in range(n_b):
        for j in range(SSM_SLAB):
            ut_s[j, pl.ds(b, n_t, stride=n_b), :] = u_ref[b, :, j * LANES:(j + 1) * LANES]

    tb_rows = SCAN_TB * n_b
    n_tb = n_t // SCAN_TB
    items = [(j, tb) for j in range(SSM_SLAB) for tb in range(n_tb)]

    def bu_block(item):
        j, tb = item
        lhs = ut_s[j, tb * tb_rows:(tb + 1) * tb_rows, :].astype(BF16)
        return _dot(lhs, bb_s[j]), _dot(lhs, bb_s[SSM_SLAB + j])

    bu = bu_block(items[0])
    for idx, (j, tb) in enumerate(items):
        c_re = slice(j * chunk, (j + 1) * chunk)
        c_im = slice(N_STATES + j * chunk, N_STATES + (j + 1) * chunk)
        bu_re, bu_im = bu
        if idx + 1 < len(items):
            bu = bu_block(items[idx + 1])
        if tb == 0:
            ar = are_s[:, c_re]
            ai = aim_s[:, c_re]
            sr = state_s[0, :, c_re]
            si = state_s[1, :, c_re]
        for t in range(SCAN_TB):
            local = slice(t * n_b, (t + 1) * n_b)
            rows_t = slice(tb * tb_rows + t * n_b, tb * tb_rows + (t + 1) * n_b)
            sr, si = (ar * sr - ai * si + bu_re[local, :],
                      ar * si + ai * sr + bu_im[local, :])
            st_s[rows_t, c_re] = sr
            st_s[rows_t, c_im] = si
        if tb == n_tb - 1:
            state_s[0, :, c_re] = sr
            state_s[1, :, c_re] = si
            s_re = st_s[:, c_re].astype(BF16)
            s_im = st_s[:, c_im].astype(BF16)
            y = (_dot(s_re, cre_ref[j]) - _dot(s_im, cim_ref[j])
                 + d_ref[:, j * LANES:(j + 1) * LANES] * ut_s[j])
            yt_s[j] = jax.nn.gelu(y)

    yg = jnp.concatenate([yt_s[j] for j in range(SSM_SLAB)], axis=-1).astype(BF16)
    ga = _dot(yg, wglu_ref[...])
    out = ga[:, :D_SSM] * jax.nn.sigmoid(ga[:, D_SSM:])
    for j in range(SSM_SLAB):
        yt_s[j] = out[:, j * LANES:(j + 1) * LANES]
    for b in range(n_b):
        for j in range(SSM_SLAB):
            o_ref[b, :, j * LANES:(j + 1) * LANES] = (
                yt_s[j, pl.ds(b, n_t, stride=n_b), :].astype(BF16))


def _s5(u, lam_re, lam_im, lstep, b_re_blk, b_im_blk, c_re_blk, c_im_blk, d_skip, w_glu_b):
    bsz, seq, _ = u.shape
    assert bsz == SUBLANES
    n_t = SSM_L
    rows = bsz * n_t
    chunk = N_STATES // SSM_SLAB
    c2 = lambda c: (0, 0)
    c3 = lambda c: (0, 0, 0)
    return pl.pallas_call(
        _s5_kernel,
        out_shape=jax.ShapeDtypeStruct((bsz, seq, D_SSM), BF16),
        grid=(seq // n_t,),
        in_specs=[
            pl.BlockSpec((bsz, n_t, D_SSM), lambda c: (0, c, 0)),
            pl.BlockSpec(lam_re.shape, c2),
            pl.BlockSpec(lam_im.shape, c2),
            pl.BlockSpec(lstep.shape, c2),
            pl.BlockSpec(b_re_blk.shape, c3),
            pl.BlockSpec(b_im_blk.shape, c3),
            pl.BlockSpec(c_re_blk.shape, c3),
            pl.BlockSpec(c_im_blk.shape, c3),
            pl.BlockSpec(d_skip.shape, c2),
            pl.BlockSpec(w_glu_b.shape, c2),
        ],
        out_specs=pl.BlockSpec((bsz, n_t, D_SSM), lambda c: (0, c, 0)),
        scratch_shapes=[
            pltpu.VMEM((bsz, N_STATES), F32),
            pltpu.VMEM((bsz, N_STATES), F32),
            pltpu.VMEM((2 * SSM_SLAB, LANES, chunk), BF16),
            pltpu.VMEM((2, bsz, N_STATES), F32),
            pltpu.VMEM((SSM_SLAB, rows, LANES), F32),
            pltpu.VMEM((rows, 2 * N_STATES), F32),
            pltpu.VMEM((SSM_SLAB, rows, LANES), F32),
        ],
        compiler_params=pltpu.CompilerParams(
            dimension_semantics=("arbitrary",),
            vmem_limit_bytes=VMEM_LIMIT_BYTES),
        name="s5",
    )(u, lam_re, lam_im, lstep, b_re_blk, b_im_blk, c_re_blk, c_im_blk, d_skip, w_glu_b)


def _sublane_allreduce(x, op):
    for shift in (4, 2, 1):
        x = op(x, pltpu.roll(x, shift, 0))
    return x


def _diffattn_kernel(qmt_ref, k_ref, vt_ref, lq1_ref, lk1_ref, lq2_ref, lk2_ref, gain_ref,
                     o_ref, m_s, acc_s, sbuf_s):
    qi = pl.program_id(1)
    tq = o_ref.shape[0]
    tk = tq
    n = 2 * tq
    acc_groups = acc_s.shape[1] // SUBLANES

    n_slots = sbuf_s.shape[0]

    def update(blocks):
        items = [(j, masked, first, h) for j, masked, first in blocks for h in range(N_HEADS)]

        col_max = {}

        def issue_scores(idx):
            j, masked, _, h = items[idx]
            r0 = pl.multiple_of(j * tk, tk)
            kt = k_ref[pl.ds(r0, tk), h * HEAD_WIDTH:(h + 1) * HEAD_WIDTH]
            s = _dot(kt, qmt_ref[h])
            if masked:
                key = lax.broadcasted_iota(jnp.int32, s.shape, 0)
                qry = lax.broadcasted_iota(jnp.int32, s.shape, 1) & (tq - 1)
                s = jnp.where(key <= qry, s, -jnp.inf)
            sbuf_s[idx % n_slots] = s
            col_max[idx] = jnp.max(s.reshape(tk // SUBLANES, SUBLANES, n), axis=0)

        def load_scores(idx):
            return sbuf_s[idx % n_slots].reshape(tk // SUBLANES, SUBLANES, n)

        def running_max(idx):
            _, _, first, h = items[idx]
            m_cur = _sublane_allreduce(col_max.pop(idx), jnp.maximum)
            if first:
                m_s[h] = m_cur
                return m_cur, None
            m_prev = m_s[h]
            m_new = jnp.maximum(m_prev, m_cur)
            m_s[h] = m_new
            return m_new, jnp.exp2(m_prev - m_new)

        def accumulate(idx, m_new, alpha):
            j, _, first, h = items[idx]
            p3 = jnp.exp2(load_scores(idx) - m_new[None])
            pv = _dot(vt_ref[h, j], p3.reshape(tk, n).astype(BF16))
            if first:
                acc_s[h] = pv
            else:
                acc3 = acc_s[h].reshape(acc_groups, SUBLANES, n)
                acc_s[h] = (alpha[None] * acc3).reshape(acc_s.shape[1], n) + pv

        ahead = n_slots - 1
        for idx in range(min(ahead, len(items))):
            issue_scores(idx)
        stats = {idx: running_max(idx) for idx in range(min(MAX_AHEAD, len(items)))}
        for idx in range(len(items)):
            if idx + ahead < len(items):
                issue_scores(idx + ahead)
            if idx + MAX_AHEAD < len(items):
                stats[idx + MAX_AHEAD] = running_max(idx + MAX_AHEAD)
            accumulate(idx, *stats.pop(idx))

    n_tiles = qi + 1
    for lead in range(1, KV_GROUP + 1):
        @pl.when(n_tiles % KV_GROUP == lead % KV_GROUP)
        def _(lead=lead):
            update([(qi, True, True)] + [(qi - t, False, False) for t in range(1, lead)])

    def body(jj, carry):
        update([(KV_GROUP * jj + t, False, False) for t in range(KV_GROUP)])
        return carry

    lax.fori_loop(0, (n_tiles - 1) // KV_GROUP, body, 0)

    lam = (jnp.exp(jnp.sum(lq1_ref[...] * lk1_ref[...], keepdims=True))
           - jnp.exp(jnp.sum(lq2_ref[...] * lk2_ref[...], keepdims=True)) + LAMBDA_INIT)
    groups = HEAD_WIDTH // SUBLANES
    for h in range(N_HEADS):
        num3 = acc_s[h, 0:HEAD_WIDTH, :].reshape(groups, SUBLANES, n)
        den = acc_s[h, HEAD_WIDTH:HEAD_WIDTH + SUBLANES, :]
        on3 = num3 / den[None]
        ot3 = on3[:, :, 0:tq] - lam * on3[:, :, tq:n]
        ms = _sublane_allreduce(jnp.sum(ot3 * ot3, axis=0), jnp.add) * (1.0 / HEAD_WIDTH)
        ot3 = ot3 * lax.rsqrt(ms + RMS_EPS)[None]
        o = ot3.reshape(HEAD_WIDTH, tq).T * (gain_ref[...] * (1.0 - LAMBDA_INIT))
        o_ref[:, h * HEAD_WIDTH:(h + 1) * HEAD_WIDTH] = o.astype(BF16)


def _diffattn(qmt, k, vt, lq1, lk1, lq2, lk2, gain):
    bsz, seq, _ = k.shape
    tq = ATT_TQ
    assert tq & (tq - 1) == 0
    n_kv = seq // tq
    grid = (bsz, n_kv)
    c2 = lambda b, i: (0, 0)
    return pl.pallas_call(
        _diffattn_kernel,
        out_shape=jax.ShapeDtypeStruct((bsz, seq, D_ATTN), BF16),
        grid=grid,
        in_specs=[
            pl.BlockSpec((None, None, N_HEADS, HEAD_WIDTH, 2 * tq), lambda b, i: (b, i, 0, 0, 0)),
            pl.BlockSpec((None, seq, D_ATTN), lambda b, i: (b, 0, 0)),
            pl.BlockSpec((None, N_HEADS, n_kv, VT_ROWS, tq), lambda b, i: (b, 0, 0, 0, 0)),
            pl.BlockSpec(lq1.shape, c2),
            pl.BlockSpec(lk1.shape, c2),
            pl.BlockSpec(lq2.shape, c2),
            pl.BlockSpec(lk2.shape, c2),
            pl.BlockSpec(gain.shape, c2),
        ],
        out_specs=pl.BlockSpec((None, tq, D_ATTN), lambda b, i: (b, i, 0)),
        scratch_shapes=[
            pltpu.VMEM((N_HEADS, SUBLANES, 2 * tq), F32),
            pltpu.VMEM((N_HEADS, VT_ROWS, 2 * tq), F32),
            pltpu.VMEM((SCORE_SLOTS, tq, 2 * tq), F32),
        ],
        compiler_params=pltpu.CompilerParams(
            dimension_semantics=("parallel", "arbitrary"),
            vmem_limit_bytes=VMEM_LIMIT_BYTES),
        name="diffattn",
    )(qmt, k, vt, lq1, lk1, lq2, lk2, gain)


def _merge_kernel(x_ref, g_ref, ys_ref, ya_ref, wps_ref, wpa_ref, wo_ref, lng_ref, lnb_ref,
                  o_ref):
    n_sub = x_ref.shape[0] // MERGE_SUB_ROWS
    rows = [slice(r * MERGE_SUB_ROWS, (r + 1) * MERGE_SUB_ROWS) for r in range(n_sub)]

    def project(r):
        return _dot(ys_ref[rows[r], :], wps_ref[...]), _dot(ya_ref[rows[r], :], wpa_ref[...])

    def norm(r, mix):
        z = DEEPNORM_ALPHA * x_ref[rows[r], :] + mix
        o_ref[rows[r], :] = _layer_norm(z, lng_ref[...], lnb_ref[...])

    proj = project(0)
    pending = None
    for r in range(n_sub):
        p_ssm, p_att = proj
        if r + 1 < n_sub:
            proj = project(r + 1)
        merged = (g_ref[rows[r], 0:D_MODEL].astype(F32) * p_ssm
                  + g_ref[rows[r], D_MODEL:2 * D_MODEL].astype(F32) * p_att)
        mix = _dot(merged.astype(BF16), wo_ref[...])
        if pending is not None:
            norm(*pending)
        pending = (r, mix)
    norm(*pending)


def _merge(x2, gates2, ys2, ya2, wps, wpa, wo, ln_g, ln_b):
    n, _ = x2.shape
    tm = MERGE_TM
    row = lambda i: (i, 0)
    c2 = lambda i: (0, 0)
    return pl.pallas_call(
        _merge_kernel,
        out_shape=jax.ShapeDtypeStruct((n, D_MODEL), F32),
        grid=(n // tm,),
        in_specs=[
            pl.BlockSpec((tm, D_MODEL), row),
            pl.BlockSpec((tm, 2 * D_MODEL), row),
            pl.BlockSpec((tm, D_SSM), row),
            pl.BlockSpec((tm, D_ATTN), row),
            pl.BlockSpec(wps.shape, c2),
            pl.BlockSpec(wpa.shape, c2),
            pl.BlockSpec(wo.shape, c2),
            pl.BlockSpec(ln_g.shape, c2),
            pl.BlockSpec(ln_b.shape, c2),
        ],
        out_specs=pl.BlockSpec((tm, D_MODEL), row),
        compiler_params=pltpu.CompilerParams(
            dimension_semantics=("parallel",),
            vmem_limit_bytes=VMEM_LIMIT_BYTES),
        name="merge",
    )(x2, gates2, ys2, ya2, wps, wpa, wo, ln_g, ln_b)


def _ffn_kernel(x_ref, wg_ref, wu_ref, wd_ref, lng_ref, lnb_ref, o_ref):
    n_sub = x_ref.shape[0] // SUB_ROWS
    bounds = [sum(FFN_CHUNKS[:c]) for c in range(len(FFN_CHUNKS) + 1)]
    items = [(r, c) for r in range(n_sub) for c in range(len(FFN_CHUNKS))]
    rows = [slice(r * SUB_ROWS, (r + 1) * SUB_ROWS) for r in range(n_sub)]

    def gate_up(item):
        r, c = item
        xb = x_ref[rows[r], :].astype(BF16)
        cols = slice(bounds[c], bounds[c + 1])
        return _dot(xb, wg_ref[:, cols]), _dot(xb, wu_ref[:, cols])

    def norm(r, ff):
        z = DEEPNORM_ALPHA * x_ref[rows[r], :] + ff
        o_ref[rows[r], :] = _layer_norm(z, lng_ref[...], lnb_ref[...])

    ab = gate_up(items[0])
    ff = None
    pending = None
    for idx, (r, c) in enumerate(items):
        a, b = ab
        if idx + 1 < len(items):
            ab = gate_up(items[idx + 1])
        if pending is not None:
            norm(*pending)
            pending = None
        h = (jax.nn.silu(a) * b).astype(BF16)
        down = _dot(h, wd_ref[bounds[c]:bounds[c + 1], :])
        ff = down if c == 0 else ff + down
        if c == len(FFN_CHUNKS) - 1:
            pending = (r, ff)
    norm(*pending)


def _ffn(x2, wg, wu, wd, ln_g, ln_b):
    n, _ = x2.shape
    tm = FFN_TM
    row = lambda i: (i, 0)
    c2 = lambda i: (0, 0)
    resident = functools.partial(pl.BlockSpec, index_map=c2, pipeline_mode=pl.Buffered(1))
    return pl.pallas_call(
        _ffn_kernel,
        out_shape=jax.ShapeDtypeStruct((n, D_MODEL), F32),
        grid=(n // tm,),
        in_specs=[
            pl.BlockSpec((tm, D_MODEL), row),
            resident(wg.shape),
            resident(wu.shape),
            resident(wd.shape),
            pl.BlockSpec(ln_g.shape, c2),
            pl.BlockSpec(ln_b.shape, c2),
        ],
        out_specs=pl.BlockSpec((tm, D_MODEL), row),
        compiler_params=pltpu.CompilerParams(
            dimension_semantics=("parallel",),
            vmem_limit_bytes=VMEM_LIMIT_BYTES),
        name="ffn",
    )(x2, wg, wu, wd, ln_g, ln_b)


def _rope_tables(seq):
    half = HEAD_DIM // 2
    inv_freq = ROPE_THETA ** (-jnp.arange(0, HEAD_DIM, 2, dtype=F32) / HEAD_DIM)
    ang = jnp.arange(seq, dtype=F32)[:, None] * inv_freq[None, :]
    cos, sin = jnp.cos(ang), jnp.sin(ang)
    reps = LANES // half
    cos_t = jnp.tile(cos, (1, reps))
    sign = jnp.tile(jnp.concatenate([-jnp.ones((half,), F32), jnp.ones((half,), F32)]),
                    LANES // HEAD_DIM)
    sin_t = jnp.tile(sin, (1, reps)) * sign[None, :]
    return cos_t, sin_t


def _block_diag_in(b):
    gps = N_SSM_GROUPS // SSM_SLAB
    bt = jnp.swapaxes(b, 1, 2).reshape(SSM_SLAB, gps, SSM_GROUP, SSM_STATE)
    eye = jnp.eye(gps, dtype=b.dtype)
    blk = bt[:, :, :, None, :] * eye[None, :, None, :, None]
    return blk.reshape(SSM_SLAB, gps * SSM_GROUP, gps * SSM_STATE)


def _block_diag_out(c):
    gps = N_SSM_GROUPS // SSM_SLAB
    ct = jnp.swapaxes(c, 1, 2).reshape(SSM_SLAB, gps, SSM_STATE, SSM_GROUP)
    eye = jnp.eye(gps, dtype=c.dtype)
    blk = ct[:, :, :, None, :] * eye[None, :, None, :, None]
    return blk.reshape(SSM_SLAB, gps * SSM_STATE, gps * SSM_GROUP)


def kernel(x, w_in, b_gate, ssm_lambda_re, ssm_lambda_im, ssm_log_step, ssm_b_re, ssm_b_im, ssm_c_re, ssm_c_im, ssm_d, w_glu, lambda_q1, lambda_k1, lambda_q2, lambda_k2, subln_gain, w_proj_ssm, w_proj_attn, w_out, ln1_g, ln1_b, w_ffn_gate, w_ffn_up, w_ffn_down, ln2_g, ln2_b):
    bsz, seq, _ = x.shape
    cos_t, sin_t = _rope_tables(seq)
    for l in range(DEPTH):
        (u, qmt, k, vt, gates), (w_glu_b, wps_b, wpa_b, wo_b, wg_b, wu_b, wd_b) = _in_proj(
            x, w_in[l].astype(BF16), b_gate[l][None, :], cos_t, sin_t,
            (w_glu[l], w_proj_ssm[l], w_proj_attn[l], w_out[l],
             w_ffn_gate[l], w_ffn_up[l], w_ffn_down[l]))

        lstep = jnp.broadcast_to(ssm_log_step[l][:, None], (N_SSM_GROUPS, SSM_STATE))
        y_ssm = _s5(
            u,
            ssm_lambda_re[l].reshape(1, N_STATES),
            ssm_lambda_im[l].reshape(1, N_STATES),
            lstep.reshape(1, N_STATES),
            _block_diag_in(ssm_b_re[l]), _block_diag_in(ssm_b_im[l]),
            _block_diag_out(ssm_c_re[l]).astype(BF16), _block_diag_out(ssm_c_im[l]).astype(BF16),
            ssm_d[l].reshape(1, D_SSM),
            w_glu_b)

        y_att = _diffattn(qmt, k, vt, lambda_q1[l][None, :], lambda_k1[l][None, :],
                          lambda_q2[l][None, :], lambda_k2[l][None, :], subln_gain[l][None, :])

        n = bsz * seq
        x1 = _merge(x.reshape(n, D_MODEL), gates.reshape(n, 2 * D_MODEL),
                    y_ssm.reshape(n, D_SSM), y_att.reshape(n, D_ATTN),
                    wps_b, wpa_b, wo_b, ln1_g[l][None, :], ln1_b[l][None, :])
        x2 = _ffn(x1, wg_b, wu_b, wd_b, ln2_g[l][None, :], ln2_b[l][None, :])
        x = x2.reshape(bsz, seq, D_MODEL)
    return x
```

```python
import functools
import math

import jax
import jax.numpy as jnp
from jax import lax
from jax.experimental import pallas as pl
from jax.experimental.pallas import tpu as pltpu

F32 = jnp.float32
BF16 = jnp.bfloat16

D_MODEL = 1024
D_SSM = 512
SSM_GROUP = 16
N_SSM_GROUPS = 32
SSM_STATE = 64
N_STATES = N_SSM_GROUPS * SSM_STATE
D_ATTN = 512
HEAD_DIM = 64
HEAD_WIDTH = 2 * HEAD_DIM
N_HEADS = 4
ROPE_THETA = 10000.0
D_FF = 2816
DEPTH = 1
DEEPNORM_ALPHA = (2.0 * DEPTH) ** 0.25
LN_EPS = 1e-5
RMS_EPS = 1e-5
LAMBDA_INIT = 0.8 - 0.6 * math.exp(-0.3 * 0)

LANES = 128
SUBLANES = 8
BF16_ROWS = 2 * SUBLANES
MXU_TILE = 256
VMEM_LIMIT_BYTES = 56 * 1024 * 1024

IN_TS = 1024
SSM_L = 128
SSM_SLAB = 4
SCAN_TB = 16
ATT_TQ = 256
VT_ROWS = HEAD_WIDTH + BF16_ROWS
KV_GROUP = 6
SCORE_SLOTS = 4
MAX_AHEAD = 1
MERGE_TM = 1024
FFN_TM = 1024
SUB_ROWS = 512
MERGE_SUB_ROWS = 256
FFN_CHUNKS = (6 * MXU_TILE, 5 * MXU_TILE)
assert sum(FFN_CHUNKS) == D_FF


def _dot(a, b):
    return jnp.dot(a, b, preferred_element_type=F32)


def _layer_norm(z, g, b):
    mu = jnp.mean(z, axis=-1, keepdims=True)
    zc = z - mu
    var = jnp.mean(zc * zc, axis=-1, keepdims=True)
    return zc * lax.rsqrt(var + LN_EPS) * g + b


def _in_proj_kernel(n_cast, x_ref, w_ref, bg_ref, cos_ref, sin_ref, *refs):
    cast_in = refs[:n_cast]
    u_ref, qmt_ref, k_ref, vt_ref, g_ref = refs[n_cast:n_cast + 5]
    cast_refs = cast_in + refs[n_cast + 5:]
    xb = x_ref[...].astype(BF16)
    off_q, off_k, off_v = D_SSM, D_SSM + D_ATTN, D_SSM + 2 * D_ATTN
    off_g = off_v + D_ATTN
    u_ref[...] = _dot(xb, w_ref[:, 0:D_SSM])
    cos = cos_ref[...]
    sin = sin_ref[...]
    lane = lax.broadcasted_iota(jnp.int32, cos.shape, 1)
    low_half = (lane & (HEAD_DIM // 2)) == 0
    def rope(slab):
        partner = jnp.where(low_half,
                            pltpu.roll(slab, LANES - HEAD_DIM // 2, 1),
                            pltpu.roll(slab, HEAD_DIM // 2, 1))
        return slab * cos + partner * sin

    hk = _dot(xb, w_ref[:, off_k:off_k + D_ATTN])
    for hd in range(N_HEADS):
        k_ref[:, hd * LANES:(hd + 1) * LANES] = rope(hk[:, hd * LANES:(hd + 1) * LANES]).astype(BF16)

    q_scale = HEAD_DIM ** -0.5 * math.log2(math.e)
    hq = _dot(xb, w_ref[:, off_q:off_q + D_ATTN])
    for hd in range(N_HEADS):
        rq = rope(hq[:, hd * LANES:(hd + 1) * LANES]) * q_scale
        for t in range(qmt_ref.shape[0]):
            qt = rq[t * ATT_TQ:(t + 1) * ATT_TQ, :].T
            d = lax.broadcasted_iota(jnp.int32, qt.shape, 0)
            qmt_ref[t, hd, :, 0:ATT_TQ] = jnp.where(d < HEAD_DIM, qt, 0.0).astype(BF16)
            qmt_ref[t, hd, :, ATT_TQ:2 * ATT_TQ] = jnp.where(d >= HEAD_DIM, qt, 0.0).astype(BF16)
    v = _dot(xb, w_ref[:, off_v:off_v + D_ATTN])
    ones = jnp.ones((VT_ROWS - HEAD_WIDTH, ATT_TQ), BF16)
    for hd in range(N_HEADS):
        for r in range(vt_ref.shape[1]):
            tile = v[r * ATT_TQ:(r + 1) * ATT_TQ, hd * LANES:(hd + 1) * LANES]
            vt_ref[hd, r, 0:HEAD_WIDTH, :] = tile.T.astype(BF16)
            vt_ref[hd, r, HEAD_WIDTH:VT_ROWS, :] = ones
    g = _dot(xb, w_ref[:, off_g:off_g + 2 * D_MODEL]) + bg_ref[...]
    g_ref[...] = jax.nn.sigmoid(g).astype(BF16)
    for src_ref, dst_ref in zip(cast_refs[:len(cast_refs) // 2], cast_refs[len(cast_refs) // 2:]):
        dst_ref[...] = src_ref[...].astype(BF16)


def _cast_block_rows(n_rows, n_steps):
    rows = -(-n_rows // n_steps)
    rows = -(-rows // BF16_ROWS) * BF16_ROWS
    while n_rows % rows:
        rows += BF16_ROWS
    return rows


def _in_proj(x, w_in_b, b_gate, cos_t, sin_t, cast_weights):
    bsz, seq, _ = x.shape
    ts = IN_TS
    grid = (bsz, seq // ts)
    n_steps = grid[0] * grid[1]
    row = lambda b, i: (b, i, 0)
    const2 = lambda b, i: (0, 0)
    cast_specs = []
    for w in cast_weights:
        rows = _cast_block_rows(w.shape[0], n_steps)
        last = w.shape[0] // rows - 1
        cast_specs.append(pl.BlockSpec(
            (rows, w.shape[1]),
            lambda b, i, last=last: (jnp.minimum(b * grid[1] + i, last), 0)))
    out_shape = (
        jax.ShapeDtypeStruct((bsz, seq, D_SSM), F32),
        jax.ShapeDtypeStruct((bsz, seq // ATT_TQ, N_HEADS, HEAD_WIDTH, 2 * ATT_TQ), BF16),
        jax.ShapeDtypeStruct((bsz, seq, D_ATTN), BF16),
        jax.ShapeDtypeStruct((bsz, N_HEADS, seq // ATT_TQ, VT_ROWS, ATT_TQ), BF16),
        jax.ShapeDtypeStruct((bsz, seq, 2 * D_MODEL), BF16),
    ) + tuple(jax.ShapeDtypeStruct(w.shape, BF16) for w in cast_weights)
    outs = pl.pallas_call(
        functools.partial(_in_proj_kernel, len(cast_weights)),
        out_shape=out_shape,
        grid=grid,
        in_specs=[
            pl.BlockSpec((None, ts, D_MODEL), row),
            pl.BlockSpec(w_in_b.shape, const2),
            pl.BlockSpec(b_gate.shape, const2),
            pl.BlockSpec((ts, LANES), lambda b, i: (i, 0)),
            pl.BlockSpec((ts, LANES), lambda b, i: (i, 0)),
        ] + cast_specs,
        out_specs=(
            pl.BlockSpec((None, ts, D_SSM), row),
            pl.BlockSpec((None, ts // ATT_TQ, N_HEADS, HEAD_WIDTH, 2 * ATT_TQ),
                         lambda b, i: (b, i, 0, 0, 0)),
            pl.BlockSpec((None, ts, D_ATTN), row),
            pl.BlockSpec((None, N_HEADS, ts // ATT_TQ, VT_ROWS, ATT_TQ),
                         lambda b, i: (b, 0, i, 0, 0)),
            pl.BlockSpec((None, ts, 2 * D_MODEL), row),
        ) + tuple(cast_specs),
        compiler_params=pltpu.CompilerParams(
            dimension_semantics=("arbitrary", "arbitrary"),
            vmem_limit_bytes=VMEM_LIMIT_BYTES),
        name="in_proj",
    )(x, w_in_b, b_gate, cos_t, sin_t, *cast_weights)
    return outs[:5], outs[5:]


def _s5_kernel(u_ref, lre_ref, lim_ref, lstep_ref, bre_ref, bim_ref, cre_ref, cim_ref,
               d_ref, wglu_ref, o_ref,
               are_s, aim_s, bb_s, state_s, ut_s, st_s, yt_s):
    n_b, n_t, _ = u_ref.shape
    chunk = N_STATES // SSM_SLAB

    @pl.when(pl.program_id(0) == 0)
    def _():
        lr = lre_ref[...]
        li = lim_ref[...]
        dt = jnp.exp(lstep_ref[...])
        mag = jnp.exp(lr * dt)
        ar = mag * jnp.cos(li * dt)
        ai = mag * jnp.sin(li * dt)
        are_s[...] = jnp.broadcast_to(ar, are_s.shape)
        aim_s[...] = jnp.broadcast_to(ai, aim_s.shape)
        nr, ni = ar - 1.0, ai
        den = lr * lr + li * li
        fr = (nr * lr + ni * li) / den
        fi = (ni * lr - nr * li) / den
        for j in range(SSM_SLAB):
            frj = fr[:, j * chunk:(j + 1) * chunk]
            fij = fi[:, j * chunk:(j + 1) * chunk]
            br = bre_ref[j]
            bi = bim_ref[j]
            bb_s[j] = (frj * br - fij * bi).astype(BF16)
            bb_s[SSM_SLAB + j] = (frj * bi + fij * br).astype(BF16)
        state_s[...] = jnp.zeros(state_s.shape, F32)

    for b in range(n_b):
        for j in range(SSM_SLAB):
            ut_s[j, pl.ds(b, n_t, stride=n_b), :] = u_ref[b, :, j * LANES:(j + 1) * LANES]

    tb_rows = SCAN_TB * n_b
    n_tb = n_t // SCAN_TB
    items = [(j, tb) for j in range(SSM_SLAB) for tb in range(n_tb)]

    def bu_block(item):
        j, tb = item
        lhs = ut_s[j, tb * tb_rows:(tb + 1) * tb_rows, :].astype(BF16)
        return _dot(lhs, bb_s[j]), _dot(lhs, bb_s[SSM_SLAB + j])

    bu = bu_block(items[0])
    for idx, (j, tb) in enumerate(items):
        c_re = slice(j * chunk, (j + 1) * chunk)
        c_im = slice(N_STATES + j * chunk, N_STATES + (j + 1) * chunk)
        bu_re, bu_im = bu
        if idx + 1 < len(items):
            bu = bu_block(items[idx + 1])
        if tb == 0:
            ar = are_s[:, c_re]
            ai = aim_s[:, c_re]
            sr = state_s[0, :, c_re]
            si = state_s[1, :, c_re]
        for t in range(SCAN_TB):
            local = slice(t * n_b, (t + 1) * n_b)
            rows_t = slice(tb * tb_rows + t * n_b, tb * tb_rows + (t + 1) * n_b)
            sr, si = (ar * sr - ai * si + bu_re[local, :],
                      ar * si + ai * sr + bu_im[local, :])
            st_s[rows_t, c_re] = sr
            st_s[rows_t, c_im] = si
        if tb == n_tb - 1:
            state_s[0, :, c_re] = sr
            state_s[1, :, c_re] = si
            s_re = st_s[:, c_re].astype(BF16)
            s_im = st_s[:, c_im].astype(BF16)
            y = (_dot(s_re, cre_ref[j]) - _dot(s_im, cim_ref[j])
                 + d_ref[:, j * LANES:(j + 1) * LANES] * ut_s[j])
            yt_s[j] = jax.nn.gelu(y)

    yg = jnp.concatenate([yt_s[j] for j in range(SSM_SLAB)], axis=-1).astype(BF16)
    ga = _dot(yg, wglu_ref[...])
    out = ga[:, :D_SSM] * jax.nn.sigmoid(ga[:, D_SSM:])
    for j in range(SSM_SLAB):
        yt_s[j] = out[:, j * LANES:(j + 1) * LANES]
    for b in range(n_b):
        for j in range(SSM_SLAB):
            o_ref[b, :, j * LANES:(j + 1) * LANES] = (
                yt_s[j, pl.ds(b, n_t, stride=n_b), :].astype(BF16))


def _s5(u, lam_re, lam_im, lstep, b_re_blk, b_im_blk, c_re_blk, c_im_blk, d_skip, w_glu_b):
    bsz, seq, _ = u.shape
    assert bsz == SUBLANES
    n_t = SSM_L
    rows = bsz * n_t
    chunk = N_STATES // SSM_SLAB
    c2 = lambda c: (0, 0)
    c3 = lambda c: (0, 0, 0)
    return pl.pallas_call(
        _s5_kernel,
        out_shape=jax.ShapeDtypeStruct((bsz, seq, D_SSM), BF16),
        grid=(seq // n_t,),
        in_specs=[
            pl.BlockSpec((bsz, n_t, D_SSM), lambda c: (0, c, 0)),
            pl.BlockSpec(lam_re.shape, c2),
            pl.BlockSpec(lam_im.shape, c2),
            pl.BlockSpec(lstep.shape, c2),
            pl.BlockSpec(b_re_blk.shape, c3),
            pl.BlockSpec(b_im_blk.shape, c3),
            pl.BlockSpec(c_re_blk.shape, c3),
            pl.BlockSpec(c_im_blk.shape, c3),
            pl.BlockSpec(d_skip.shape, c2),
            pl.BlockSpec(w_glu_b.shape, c2),
        ],
        out_specs=pl.BlockSpec((bsz, n_t, D_SSM), lambda c: (0, c, 0)),
        scratch_shapes=[
            pltpu.VMEM((bsz, N_STATES), F32),
            pltpu.VMEM((bsz, N_STATES), F32),
            pltpu.VMEM((2 * SSM_SLAB, LANES, chunk), BF16),
            pltpu.VMEM((2, bsz, N_STATES), F32),
            pltpu.VMEM((SSM_SLAB, rows, LANES), F32),
            pltpu.VMEM((rows, 2 * N_STATES), F32),
            pltpu.VMEM((SSM_SLAB, rows, LANES), F32),
        ],
        compiler_params=pltpu.CompilerParams(
            dimension_semantics=("arbitrary",),
            vmem_limit_bytes=VMEM_LIMIT_BYTES),
        name="s5",
    )(u, lam_re, lam_im, lstep, b_re_blk, b_im_blk, c_re_blk, c_im_blk, d_skip, w_glu_b)


def _sublane_allreduce(x, op):
    for shift in (4, 2, 1):
        x = op(x, pltpu.roll(x, shift, 0))
    return x


def _diffattn_kernel(qmt_ref, k_ref, vt_ref, lq1_ref, lk1_ref, lq2_ref, lk2_ref, gain_ref,
                     o_ref, m_s, acc_s, sbuf_s):
    qi = pl.program_id(1)
    tq = o_ref.shape[0]
    tk = tq
    n = 2 * tq
    acc_groups = acc_s.shape[1] // SUBLANES

    n_slots = sbuf_s.shape[0]

    def update(blocks):
        items = [(j, masked, first, h) for j, masked, first in blocks for h in range(N_HEADS)]

        col_max = {}

        def issue_scores(idx):
            j, masked, _, h = items[idx]
            r0 = pl.multiple_of(j * tk, tk)
            kt = k_ref[pl.ds(r0, tk), h * HEAD_WIDTH:(h + 1) * HEAD_WIDTH]
            s = _dot(kt, qmt_ref[h])
            if masked:
                key = lax.broadcasted_iota(jnp.int32, s.shape, 0)
                qry = lax.broadcasted_iota(jnp.int32, s.shape, 1) & (tq - 1)
                s = jnp.where(key <= qry, s, -jnp.inf)
            sbuf_s[idx % n_slots] = s
            col_max[idx] = jnp.max(s.reshape(tk // SUBLANES, SUBLANES, n), axis=0)

        def load_scores(idx):
            return sbuf_s[idx % n_slots].reshape(tk // SUBLANES, SUBLANES, n)

        def running_max(idx):
            _, _, first, h = items[idx]
            m_cur = _sublane_allreduce(col_max.pop(idx), jnp.maximum)
            if first:
                m_s[h] = m_cur
                return m_cur, None
            m_prev = m_s[h]
            m_new = jnp.maximum(m_prev, m_cur)
            m_s[h] = m_new
            return m_new, jnp.exp2(m_prev - m_new)

        def accumulate(idx, m_new, alpha):
            j, _, first, h = items[idx]
            p3 = jnp.exp2(load_scores(idx) - m_new[None])
            pv = _dot(vt_ref[h, j], p3.reshape(tk, n).astype(BF16))
            if first:
                acc_s[h] = pv
            else:
                acc3 = acc_s[h].reshape(acc_groups, SUBLANES, n)
                acc_s[h] = (alpha[None] * acc3).reshape(acc_s.shape[1], n) + pv

        ahead = n_slots - 1
        for idx in range(min(ahead, len(items))):
            issue_scores(idx)
        stats = {idx: running_max(idx) for idx in range(min(MAX_AHEAD, len(items)))}
        for idx in range(len(items)):
            if idx + ahead < len(items):
                issue_scores(idx + ahead)
            if idx + MAX_AHEAD < len(items):
                stats[idx + MAX_AHEAD] = running_max(idx + MAX_AHEAD)
            accumulate(idx, *stats.pop(idx))

    n_tiles = qi + 1
    for lead in range(1, KV_GROUP + 1):
        @pl.when(n_tiles % KV_GROUP == lead % KV_GROUP)
        def _(lead=lead):
            update([(qi, True, True)] + [(qi - t, False, False) for t in range(1, lead)])

    def body(jj, carry):
        update([(KV_GROUP * jj + t, False, False) for t in range(KV_GROUP)])
        return carry

    lax.fori_loop(0, (n_tiles - 1) // KV_GROUP, body, 0)

    lam = (jnp.exp(jnp.sum(lq1_ref[...] * lk1_ref[...], keepdims=True))
           - jnp.exp(jnp.sum(lq2_ref[...] * lk2_ref[...], keepdims=True)) + LAMBDA_INIT)
    groups = HEAD_WIDTH // SUBLANES
    for h in range(N_HEADS):
        num3 = acc_s[h, 0:HEAD_WIDTH, :].reshape(groups, SUBLANES, n)
        den = acc_s[h, HEAD_WIDTH:HEAD_WIDTH + SUBLANES, :]
        on3 = num3 / den[None]
        ot3 = on3[:, :, 0:tq] - lam * on3[:, :, tq:n]
        ms = _sublane_allreduce(jnp.sum(ot3 * ot3, axis=0), jnp.add) * (1.0 / HEAD_WIDTH)
        ot3 = ot3 * lax.rsqrt(ms + RMS_EPS)[None]
        o = ot3.reshape(HEAD_WIDTH, tq).T * (gain_ref[...] * (1.0 - LAMBDA_INIT))
        o_ref[:, h * HEAD_WIDTH:(h + 1) * HEAD_WIDTH] = o.astype(BF16)


def _diffattn(qmt, k, vt, lq1, lk1, lq2, lk2, gain):
    bsz, seq, _ = k.shape
    tq = ATT_TQ
    assert tq & (tq - 1) == 0
    n_kv = seq // tq
    grid = (bsz, n_kv)
    c2 = lambda b, i: (0, 0)
    return pl.pallas_call(
        _diffattn_kernel,
        out_shape=jax.ShapeDtypeStruct((bsz, seq, D_ATTN), BF16),
        grid=grid,
        in_specs=[
            pl.BlockSpec((None, None, N_HEADS, HEAD_WIDTH, 2 * tq), lambda b, i: (b, i, 0, 0, 0)),
            pl.BlockSpec((None, seq, D_ATTN), lambda b, i: (b, 0, 0)),
            pl.BlockSpec((None, N_HEADS, n_kv, VT_ROWS, tq), lambda b, i: (b, 0, 0, 0, 0)),
            pl.BlockSpec(lq1.shape, c2),
            pl.BlockSpec(lk1.shape, c2),
            pl.BlockSpec(lq2.shape, c2),
            pl.BlockSpec(lk2.shape, c2),
            pl.BlockSpec(gain.shape, c2),
        ],
        out_specs=pl.BlockSpec((None, tq, D_ATTN), lambda b, i: (b, i, 0)),
        scratch_shapes=[
            pltpu.VMEM((N_HEADS, SUBLANES, 2 * tq), F32),
            pltpu.VMEM((N_HEADS, VT_ROWS, 2 * tq), F32),
            pltpu.VMEM((SCORE_SLOTS, tq, 2 * tq), F32),
        ],
        compiler_params=pltpu.CompilerParams(
            dimension_semantics=("parallel", "arbitrary"),
            vmem_limit_bytes=VMEM_LIMIT_BYTES),
        name="diffattn",
    )(qmt, k, vt, lq1, lk1, lq2, lk2, gain)


def _merge_kernel(x_ref, g_ref, ys_ref, ya_ref, wps_ref, wpa_ref, wo_ref, lng_ref, lnb_ref,
                  o_ref):
    n_sub = x_ref.shape[0] // MERGE_SUB_ROWS
    rows = [slice(r * MERGE_SUB_ROWS, (r + 1) * MERGE_SUB_ROWS) for r in range(n_sub)]

    def project(r):
        return _dot(ys_ref[rows[r], :], wps_ref[...]), _dot(ya_ref[rows[r], :], wpa_ref[...])

    def norm(r, mix):
        z = DEEPNORM_ALPHA * x_ref[rows[r], :] + mix
        o_ref[rows[r], :] = _layer_norm(z, lng_ref[...], lnb_ref[...])

    proj = project(0)
    pending = None
    for r in range(n_sub):
        p_ssm, p_att = proj
        if r + 1 < n_sub:
            proj = project(r + 1)
        merged = (g_ref[rows[r], 0:D_MODEL].astype(F32) * p_ssm
                  + g_ref[rows[r], D_MODEL:2 * D_MODEL].astype(F32) * p_att)
        mix = _dot(merged.astype(BF16), wo_ref[...])
        if pending is not None:
            norm(*pending)
        pending = (r, mix)
    norm(*pending)


def _merge(x2, gates2, ys2, ya2, wps, wpa, wo, ln_g, ln_b):
    n, _ = x2.shape
    tm = MERGE_TM
    row = lambda i: (i, 0)
    c2 = lambda i: (0, 0)
    return pl.pallas_call(
        _merge_kernel,
        out_shape=jax.ShapeDtypeStruct((n, D_MODEL), F32),
        grid=(n // tm,),
        in_specs=[
            pl.BlockSpec((tm, D_MODEL), row),
            pl.BlockSpec((tm, 2 * D_MODEL), row),
            pl.BlockSpec((tm, D_SSM), row),
            pl.BlockSpec((tm, D_ATTN), row),
            pl.BlockSpec(wps.shape, c2),
            pl.BlockSpec(wpa.shape, c2),
            pl.BlockSpec(wo.shape, c2),
            pl.BlockSpec(ln_g.shape, c2),
            pl.BlockSpec(ln_b.shape, c2),
        ],
        out_specs=pl.BlockSpec((tm, D_MODEL), row),
        compiler_params=pltpu.CompilerParams(
            dimension_semantics=("parallel",),
            vmem_limit_bytes=VMEM_LIMIT_BYTES),
        name="merge",
    )(x2, gates2, ys2, ya2, wps, wpa, wo, ln_g, ln_b)


def _ffn_kernel(x_ref, wg_ref, wu_ref, wd_ref, lng_ref, lnb_ref, o_ref):
    n_sub = x_ref.shape[0] // SUB_ROWS
    bounds = [sum(FFN_CHUNKS[:c]) for c in range(len(FFN_CHUNKS) + 1)]
    items = [(r, c) for r in range(n_sub) for c in range(len(FFN_CHUNKS))]
    rows = [slice(r * SUB_ROWS, (r + 1) * SUB_ROWS) for r in range(n_sub)]

    def gate_up(item):
        r, c = item
        xb = x_ref[rows[r], :].astype(BF16)
        cols = slice(bounds[c], bounds[c + 1])
        return _dot(xb, wg_ref[:, cols]), _dot(xb, wu_ref[:, cols])

    def norm(r, ff):
        z = DEEPNORM_ALPHA * x_ref[rows[r], :] + ff
        o_ref[rows[r], :] = _layer_norm(z, lng_ref[...], lnb_ref[...])

    ab = gate_up(items[0])
    ff = None
    pending = None
    for idx, (r, c) in enumerate(items):
        a, b = ab
        if idx + 1 < len(items):
            ab = gate_up(items[idx + 1])
        if pending is not None:
            norm(*pending)
            pending = None
        h = (jax.nn.silu(a) * b).astype(BF16)
        down = _dot(h, wd_ref[bounds[c]:bounds[c + 1], :])
        ff = down if c == 0 else ff + down
        if c == len(FFN_CHUNKS) - 1:
            pending = (r, ff)
    norm(*pending)


def _ffn(x2, wg, wu, wd, ln_g, ln_b):
    n, _ = x2.shape
    tm = FFN_TM
    row = lambda i: (i, 0)
    c2 = lambda i: (0, 0)
    resident = functools.partial(pl.BlockSpec, index_map=c2, pipeline_mode=pl.Buffered(1))
    return pl.pallas_call(
        _ffn_kernel,
        out_shape=jax.ShapeDtypeStruct((n, D_MODEL), F32),
        grid=(n // tm,),
        in_specs=[
            pl.BlockSpec((tm, D_MODEL), row),
            resident(wg.shape),
            resident(wu.shape),
            resident(wd.shape),
            pl.BlockSpec(ln_g.shape, c2),
            pl.BlockSpec(ln_b.shape, c2),
        ],
        out_specs=pl.BlockSpec((tm, D_MODEL), row),
        compiler_params=pltpu.CompilerParams(
            dimension_semantics=("parallel",),
            vmem_limit_bytes=VMEM_LIMIT_BYTES),
        name="ffn",
    )(x2, wg, wu, wd, ln_g, ln_b)


def _rope_tables(seq):
    half = HEAD_DIM // 2
    inv_freq = ROPE_THETA ** (-jnp.arange(0, HEAD_DIM, 2, dtype=F32) / HEAD_DIM)
    ang = jnp.arange(seq, dtype=F32)[:, None] * inv_freq[None, :]
    cos, sin = jnp.cos(ang), jnp.sin(ang)
    reps = LANES // half
    cos_t = jnp.tile(cos, (1, reps))
    sign = jnp.tile(jnp.concatenate([-jnp.ones((half,), F32), jnp.ones((half,), F32)]),
                    LANES // HEAD_DIM)
    sin_t = jnp.tile(sin, (1, reps)) * sign[None, :]
    return cos_t, sin_t


def _block_diag_in(b):
    gps = N_SSM_GROUPS // SSM_SLAB
    bt = jnp.swapaxes(b, 1, 2).reshape(SSM_SLAB, gps, SSM_GROUP, SSM_STATE)
    eye = jnp.eye(gps, dtype=b.dtype)
    blk = bt[:, :, :, None, :] * eye[None, :, None, :, None]
    return blk.reshape(SSM_SLAB, gps * SSM_GROUP, gps * SSM_STATE)


def _block_diag_out(c):
    gps = N_SSM_GROUPS // SSM_SLAB
    ct = jnp.swapaxes(c, 1, 2).reshape(SSM_SLAB, gps, SSM_STATE, SSM_GROUP)
    eye = jnp.eye(gps, dtype=c.dtype)
    blk = ct[:, :, :, None, :] * eye[None, :, None, :, None]
    return blk.reshape(SSM_SLAB, gps * SSM_STATE, gps * SSM_GROUP)


def kernel(x, w_in, b_gate, ssm_lambda_re, ssm_lambda_im, ssm_log_step, ssm_b_re, ssm_b_im, ssm_c_re, ssm_c_im, ssm_d, w_glu, lambda_q1, lambda_k1, lambda_q2, lambda_k2, subln_gain, w_proj_ssm, w_proj_attn, w_out, ln1_g, ln1_b, w_ffn_gate, w_ffn_up, w_ffn_down, ln2_g, ln2_b):
    bsz, seq, _ = x.shape
    cos_t, sin_t = _rope_tables(seq)
    for l in range(DEPTH):
        (u, qmt, k, vt, gates), (w_glu_b, wps_b, wpa_b, wo_b, wg_b, wu_b, wd_b) = _in_proj(
            x, w_in[l].astype(BF16), b_gate[l][None, :], cos_t, sin_t,
            (w_glu[l], w_proj_ssm[l], w_proj_attn[l], w_out[l],
             w_ffn_gate[l], w_ffn_up[l], w_ffn_down[l]))

        lstep = jnp.broadcast_to(ssm_log_step[l][:, None], (N_SSM_GROUPS, SSM_STATE))
        y_ssm = _s5(
            u,
            ssm_lambda_re[l].reshape(1, N_STATES),
            ssm_lambda_im[l].reshape(1, N_STATES),
            lstep.reshape(1, N_STATES),
            _block_diag_in(ssm_b_re[l]), _block_diag_in(ssm_b_im[l]),
            _block_diag_out(ssm_c_re[l]).astype(BF16), _block_diag_out(ssm_c_im[l]).astype(BF16),
            ssm_d[l].reshape(1, D_SSM),
            w_glu_b)

        y_att = _diffattn(qmt, k, vt, lambda_q1[l][None, :], lambda_k1[l][None, :],
                          lambda_q2[l][None, :], lambda_k2[l][None, :], subln_gain[l][None, :])

        n = bsz * seq
        x1 = _merge(x.reshape(n, D_MODEL), gates.reshape(n, 2 * D_MODEL),
                    y_ssm.reshape(n, D_SSM), y_att.reshape(n, D_ATTN),
                    wps_b, wpa_b, wo_b, ln1_g[l][None, :], ln1_b[l][None, :])
        x2 = _ffn(x1, wg_b, wu_b, wd_b, ln2_g[l][None, :], ln2_b[l][None, :])
        x = x2.reshape(bsz, seq, D_MODEL)
    return x
```

```python
import functools
import math

import jax
import jax.numpy as jnp
from jax import lax
from jax.experimental import pallas as pl
from jax.experimental.pallas import tpu as pltpu

F32 = jnp.float32
BF16 = jnp.bfloat16

D_MODEL = 1024
D_SSM = 512
SSM_GROUP = 16
N_SSM_GROUPS = 32
SSM_STATE = 64
N_STATES = N_SSM_GROUPS * SSM_STATE
D_ATTN = 512
HEAD_DIM = 64
HEAD_WIDTH = 2 * HEAD_DIM
N_HEADS = 4
ROPE_THETA = 10000.0
D_FF = 2816
DEPTH = 1
DEEPNORM_ALPHA = (2.0 * DEPTH) ** 0.25
LN_EPS = 1e-5
RMS_EPS = 1e-5
LAMBDA_INIT = 0.8 - 0.6 * math.exp(-0.3 * 0)

LANES = 128
SUBLANES = 8
BF16_ROWS = 2 * SUBLANES
MXU_TILE = 256
VMEM_LIMIT_BYTES = 56 * 1024 * 1024

IN_TS = 1024
SSM_L = 128
SSM_SLAB = 4
SCAN_TB = 16
ATT_TQ = 256
VT_ROWS = HEAD_WIDTH + BF16_ROWS
KV_GROUP = 8
SCORE_SLOTS = 4
MAX_AHEAD = 1
MERGE_TM = 1024
FFN_TM = 1024
SUB_ROWS = 512
MERGE_SUB_ROWS = 256
FFN_CHUNKS = (6 * MXU_TILE, 5 * MXU_TILE)
assert sum(FFN_CHUNKS) == D_FF


def _dot(a, b):
    return jnp.dot(a, b, preferred_element_type=F32)


def _layer_norm(z, g, b):
    mu = jnp.mean(z, axis=-1, keepdims=True)
    zc = z - mu
    var = jnp.mean(zc * zc, axis=-1, keepdims=True)
    return zc * lax.rsqrt(var + LN_EPS) * g + b


def _in_proj_kernel(n_cast, x_ref, w_ref, bg_ref, cos_ref, sin_ref, *refs):
    cast_in = refs[:n_cast]
    u_ref, qmt_ref, k_ref, vt_ref, g_ref = refs[n_cast:n_cast + 5]
    cast_refs = cast_in + refs[n_cast + 5:]
    xb = x_ref[...].astype(BF16)
    off_q, off_k, off_v = D_SSM, D_SSM + D_ATTN, D_SSM + 2 * D_ATTN
    off_g = off_v + D_ATTN
    u_ref[...] = _dot(xb, w_ref[:, 0:D_SSM])
    cos = cos_ref[...]
    sin = sin_ref[...]
    lane = lax.broadcasted_iota(jnp.int32, cos.shape, 1)
    low_half = (lane & (HEAD_DIM // 2)) == 0
    def rope(slab):
        partner = jnp.where(low_half,
                            pltpu.roll(slab, LANES - HEAD_DIM // 2, 1),
                            pltpu.roll(slab, HEAD_DIM // 2, 1))
        return slab * cos + partner * sin

    hk = _dot(xb, w_ref[:, off_k:off_k + D_ATTN])
    for hd in range(N_HEADS):
        k_ref[:, hd * LANES:(hd + 1) * LANES] = rope(hk[:, hd * LANES:(hd + 1) * LANES]).astype(BF16)

    q_scale = HEAD_DIM ** -0.5 * math.log2(math.e)
    hq = _dot(xb, w_ref[:, off_q:off_q + D_ATTN])
    for hd in range(N_HEADS):
        rq = rope(hq[:, hd * LANES:(hd + 1) * LANES]) * q_scale
        for t in range(qmt_ref.shape[0]):
            qt = rq[t * ATT_TQ:(t + 1) * ATT_TQ, :].T
            d = lax.broadcasted_iota(jnp.int32, qt.shape, 0)
            qmt_ref[t, hd, :, 0:ATT_TQ] = jnp.where(d < HEAD_DIM, qt, 0.0).astype(BF16)
            qmt_ref[t, hd, :, ATT_TQ:2 * ATT_TQ] = jnp.where(d >= HEAD_DIM, qt, 0.0).astype(BF16)
    v = _dot(xb, w_ref[:, off_v:off_v + D_ATTN])
    ones = jnp.ones((VT_ROWS - HEAD_WIDTH, ATT_TQ), BF16)
    for hd in range(N_HEADS):
        for r in range(vt_ref.shape[1]):
            tile = v[r * ATT_TQ:(r + 1) * ATT_TQ, hd * LANES:(hd + 1) * LANES]
            vt_ref[hd, r, 0:HEAD_WIDTH, :] = tile.T.astype(BF16)
            vt_ref[hd, r, HEAD_WIDTH:VT_ROWS, :] = ones
    g = _dot(xb, w_ref[:, off_g:off_g + 2 * D_MODEL]) + bg_ref[...]
    g_ref[...] = jax.nn.sigmoid(g).astype(BF16)
    for src_ref, dst_ref in zip(cast_refs[:len(cast_refs) // 2], cast_refs[len(cast_refs) // 2:]):
        dst_ref[...] = src_ref[...].astype(BF16)


def _cast_block_rows(n_rows, n_steps):
    rows = -(-n_rows // n_steps)
    rows = -(-rows // BF16_ROWS) * BF16_ROWS
    while n_rows % rows:
        rows += BF16_ROWS
    return rows


def _in_proj(x, w_in_b, b_gate, cos_t, sin_t, cast_weights):
    bsz, seq, _ = x.shape
    ts = IN_TS
    grid = (bsz, seq // ts)
    n_steps = grid[0] * grid[1]
    row = lambda b, i: (b, i, 0)
    const2 = lambda b, i: (0, 0)
    cast_specs = []
    for w in cast_weights:
        rows = _cast_block_rows(w.shape[0], n_steps)
        last = w.shape[0] // rows - 1
        cast_specs.append(pl.BlockSpec(
            (rows, w.shape[1]),
            lambda b, i, last=last: (jnp.minimum(b * grid[1] + i, last), 0)))
    out_shape = (
        jax.ShapeDtypeStruct((bsz, seq, D_SSM), F32),
        jax.ShapeDtypeStruct((bsz, seq // ATT_TQ, N_HEADS, HEAD_WIDTH, 2 * ATT_TQ), BF16),
        jax.ShapeDtypeStruct((bsz, seq, D_ATTN), BF16),
        jax.ShapeDtypeStruct((bsz, N_HEADS, seq // ATT_TQ, VT_ROWS, ATT_TQ), BF16),
        jax.ShapeDtypeStruct((bsz, seq, 2 * D_MODEL), BF16),
    ) + tuple(jax.ShapeDtypeStruct(w.shape, BF16) for w in cast_weights)
    outs = pl.pallas_call(
        functools.partial(_in_proj_kernel, len(cast_weights)),
        out_shape=out_shape,
        grid=grid,
        in_specs=[
            pl.BlockSpec((None, ts, D_MODEL), row),
            pl.BlockSpec(w_in_b.shape, const2),
            pl.BlockSpec(b_gate.shape, const2),
            pl.BlockSpec((ts, LANES), lambda b, i: (i, 0)),
            pl.BlockSpec((ts, LANES), lambda b, i: (i, 0)),
        ] + cast_specs,
        out_specs=(
            pl.BlockSpec((None, ts, D_SSM), row),
            pl.BlockSpec((None, ts // ATT_TQ, N_HEADS, HEAD_WIDTH, 2 * ATT_TQ),
                         lambda b, i: (b, i, 0, 0, 0)),
            pl.BlockSpec((None, ts, D_ATTN), row),
            pl.BlockSpec((None, N_HEADS, ts // ATT_TQ, VT_ROWS, ATT_TQ),
                         lambda b, i: (b, 0, i, 0, 0)),
            pl.BlockSpec((None, ts, 2 * D_MODEL), row),
        ) + tuple(cast_specs),
        compiler_params=pltpu.CompilerParams(
            dimension_semantics=("arbitrary", "arbitrary"),
            vmem_limit_bytes=VMEM_LIMIT_BYTES),
        name="in_proj",
    )(x, w_in_b, b_gate, cos_t, sin_t, *cast_weights)
    return outs[:5], outs[5:]


def _s5_kernel(u_ref, lre_ref, lim_ref, lstep_ref, bre_ref, bim_ref, cre_ref, cim_ref,
               d_ref, wglu_ref, o_ref,
               are_s, aim_s, bb_s, state_s, ut_s, st_s, yt_s):
    n_b, n_t, _ = u_ref.shape
    chunk = N_STATES // SSM_SLAB

    @pl.when(pl.program_id(0) == 0)
    def _():
        lr = lre_ref[...]
        li = lim_ref[...]
        dt = jnp.exp(lstep_ref[...])
        mag = jnp.exp(lr * dt)
        ar = mag * jnp.cos(li * dt)
        ai = mag * jnp.sin(li * dt)
        are_s[...] = jnp.broadcast_to(ar, are_s.shape)
        aim_s[...] = jnp.broadcast_to(ai, aim_s.shape)
        nr, ni = ar - 1.0, ai
        den = lr * lr + li * li
        fr = (nr * lr + ni * li) / den
        fi = (ni * lr - nr * li) / den
        for j in range(SSM_SLAB):
            frj = fr[:, j * chunk:(j + 1) * chunk]
            fij = fi[:, j * chunk:(j + 1) * chunk]
            br = bre_ref[j]
            bi = bim_ref[j]
            bb_s[j] = (frj * br - fij * bi).astype(BF16)
            bb_s[SSM_SLAB + j] = (frj * bi + fij * br).astype(BF16)
        state_s[...] = jnp.zeros(state_s.shape, F32)

    for b in range(n_b):
        for j in range(SSM_SLAB):
            ut_s[j, pl.ds(b, n_t, stride=n_b), :] = u_ref[b, :, j * LANES:(j + 1) * LANES]

    tb_rows = SCAN_TB * n_b
    n_tb = n_t // SCAN_TB
    items = [(j, tb) for j in range(SSM_SLAB) for tb in range(n_tb)]

    def bu_block(item):
        j, tb = item
        lhs = ut_s[j, tb * tb_rows:(tb + 1) * tb_rows, :].astype(BF16)
        return _dot(lhs, bb_s[j]), _dot(lhs, bb_s[SSM_SLAB + j])

    bu = bu_block(items[0])
    for idx, (j, tb) in enumerate(items):
        c_re = slice(j * chunk, (j + 1) * chunk)
        c_im = slice(N_STATES + j * chunk, N_STATES + (j + 1) * chunk)
        bu_re, bu_im = bu
        if idx + 1 < len(items):
            bu = bu_block(items[idx + 1])
        if tb == 0:
            ar = are_s[:, c_re]
            ai = aim_s[:, c_re]
            sr = state_s[0, :, c_re]
            si = state_s[1, :, c_re]
        for t in range(SCAN_TB):
            local = slice(t * n_b, (t + 1) * n_b)
            rows_t = slice(tb * tb_rows + t * n_b, tb * tb_rows + (t + 1) * n_b)
            sr, si = (ar * sr - ai * si + bu_re[local, :],
                      ar * si + ai * sr + bu_im[local, :])
            st_s[rows_t, c_re] = sr
            st_s[rows_t, c_im] = si
        if tb == n_tb - 1:
            state_s[0, :, c_re] = sr
            state_s[1, :, c_re] = si
            s_re = st_s[:, c_re].astype(BF16)
            s_im = st_s[:, c_im].astype(BF16)
            y = (_dot(s_re, cre_ref[j]) - _dot(s_im, cim_ref[j])
                 + d_ref[:, j * LANES:(j + 1) * LANES] * ut_s[j])
            yt_s[j] = jax.nn.gelu(y)

    yg = jnp.concatenate([yt_s[j] for j in range(SSM_SLAB)], axis=-1).astype(BF16)
    ga = _dot(yg, wglu_ref[...])
    out = ga[:, :D_SSM] * jax.nn.sigmoid(ga[:, D_SSM:])
    for j in range(SSM_SLAB):
        yt_s[j] = out[:, j * LANES:(j + 1) * LANES]
    for b in range(n_b):
        for j in range(SSM_SLAB):
            o_ref[b, :, j * LANES:(j + 1) * LANES] = (
                yt_s[j, pl.ds(b, n_t, stride=n_b), :].astype(BF16))


def _s5(u, lam_re, lam_im, lstep, b_re_blk, b_im_blk, c_re_blk, c_im_blk, d_skip, w_glu_b):
    bsz, seq, _ = u.shape
    assert bsz == SUBLANES
    n_t = SSM_L
    rows = bsz * n_t
    chunk = N_STATES // SSM_SLAB
    c2 = lambda c: (0, 0)
    c3 = lambda c: (0, 0, 0)
    return pl.pallas_call(
        _s5_kernel,
        out_shape=jax.ShapeDtypeStruct((bsz, seq, D_SSM), BF16),
        grid=(seq // n_t,),
        in_specs=[
            pl.BlockSpec((bsz, n_t, D_SSM), lambda c: (0, c, 0)),
            pl.BlockSpec(lam_re.shape, c2),
            pl.BlockSpec(lam_im.shape, c2),
            pl.BlockSpec(lstep.shape, c2),
            pl.BlockSpec(b_re_blk.shape, c3),
            pl.BlockSpec(b_im_blk.shape, c3),
            pl.BlockSpec(c_re_blk.shape, c3),
            pl.BlockSpec(c_im_blk.shape, c3),
            pl.BlockSpec(d_skip.shape, c2),
            pl.BlockSpec(w_glu_b.shape, c2),
        ],
        out_specs=pl.BlockSpec((bsz, n_t, D_SSM), lambda c: (0, c, 0)),
        scratch_shapes=[
            pltpu.VMEM((bsz, N_STATES), F32),
            pltpu.VMEM((bsz, N_STATES), F32),
            pltpu.VMEM((2 * SSM_SLAB, LANES, chunk), BF16),
            pltpu.VMEM((2, bsz, N_STATES), F32),
            pltpu.VMEM((SSM_SLAB, rows, LANES), F32),
            pltpu.VMEM((rows, 2 * N_STATES), F32),
            pltpu.VMEM((SSM_SLAB, rows, LANES), F32),
        ],
        compiler_params=pltpu.CompilerParams(
            dimension_semantics=("arbitrary",),
            vmem_limit_bytes=VMEM_LIMIT_BYTES),
        name="s5",
    )(u, lam_re, lam_im, lstep, b_re_blk, b_im_blk, c_re_blk, c_im_blk, d_skip, w_glu_b)


def _sublane_allreduce(x, op):
    for shift in (4, 2, 1):
        x = op(x, pltpu.roll(x, shift, 0))
    return x


def _diffattn_kernel(qmt_ref, k_ref, vt_ref, lq1_ref, lk1_ref, lq2_ref, lk2_ref, gain_ref,
                     o_ref, m_s, acc_s, sbuf_s):
    qi = pl.program_id(1)
    tq = o_ref.shape[0]
    tk = tq
    n = 2 * tq
    acc_groups = acc_s.shape[1] // SUBLANES

    n_slots = sbuf_s.shape[0]

    def update(blocks):
        items = [(j, masked, first, h) for j, masked, first in blocks for h in range(N_HEADS)]

        col_max = {}

        def issue_scores(idx):
            j, masked, _, h = items[idx]
            r0 = pl.multiple_of(j * tk, tk)
            kt = k_ref[pl.ds(r0, tk), h * HEAD_WIDTH:(h + 1) * HEAD_WIDTH]
            s = _dot(kt, qmt_ref[h])
            if masked:
                key = lax.broadcasted_iota(jnp.int32, s.shape, 0)
                qry = lax.broadcasted_iota(jnp.int32, s.shape, 1) & (tq - 1)
                s = jnp.where(key <= qry, s, -jnp.inf)
            sbuf_s[idx % n_slots] = s
            col_max[idx] = jnp.max(s.reshape(tk // SUBLANES, SUBLANES, n), axis=0)

        def load_scores(idx):
            return sbuf_s[idx % n_slots].reshape(tk // SUBLANES, SUBLANES, n)

        def running_max(idx):
            _, _, first, h = items[idx]
            m_cur = _sublane_allreduce(col_max.pop(idx), jnp.maximum)
            if first:
                m_s[h] = m_cur
                return m_cur, None
            m_prev = m_s[h]
            m_new = jnp.maximum(m_prev, m_cur)
            m_s[h] = m_new
            return m_new, jnp.exp2(m_prev - m_new)

        def accumulate(idx, m_new, alpha):
            j, _, first, h = items[idx]
            p3 = jnp.exp2(load_scores(idx) - m_new[None])
            pv = _dot(vt_ref[h, j], p3.reshape(tk, n).astype(BF16))
            if first:
                acc_s[h] = pv
            else:
                acc3 = acc_s[h].reshape(acc_groups, SUBLANES, n)
                acc_s[h] = (alpha[None] * acc3).reshape(acc_s.shape[1], n) + pv

        ahead = n_slots - 1
        for idx in range(min(ahead, len(items))):
            issue_scores(idx)
        stats = {idx: running_max(idx) for idx in range(min(MAX_AHEAD, len(items)))}
        for idx in range(len(items)):
            if idx + ahead < len(items):
                issue_scores(idx + ahead)
            if idx + MAX_AHEAD < len(items):
                stats[idx + MAX_AHEAD] = running_max(idx + MAX_AHEAD)
            accumulate(idx, *stats.pop(idx))

    n_tiles = qi + 1
    for lead in range(1, KV_GROUP + 1):
        @pl.when(n_tiles % KV_GROUP == lead % KV_GROUP)
        def _(lead=lead):
            update([(qi, True, True)] + [(qi - t, False, False) for t in range(1, lead)])

    def body(jj, carry):
        update([(KV_GROUP * jj + t, False, False) for t in range(KV_GROUP)])
        return carry

    lax.fori_loop(0, (n_tiles - 1) // KV_GROUP, body, 0)

    lam = (jnp.exp(jnp.sum(lq1_ref[...] * lk1_ref[...], keepdims=True))
           - jnp.exp(jnp.sum(lq2_ref[...] * lk2_ref[...], keepdims=True)) + LAMBDA_INIT)
    groups = HEAD_WIDTH // SUBLANES
    for h in range(N_HEADS):
        num3 = acc_s[h, 0:HEAD_WIDTH, :].reshape(groups, SUBLANES, n)
        den = acc_s[h, HEAD_WIDTH:HEAD_WIDTH + SUBLANES, :]
        on3 = num3 / den[None]
        ot3 = on3[:, :, 0:tq] - lam * on3[:, :, tq:n]
        ms = _sublane_allreduce(jnp.sum(ot3 * ot3, axis=0), jnp.add) * (1.0 / HEAD_WIDTH)
        ot3 = ot3 * lax.rsqrt(ms + RMS_EPS)[None]
        o = ot3.reshape(HEAD_WIDTH, tq).T * (gain_ref[...] * (1.0 - LAMBDA_INIT))
        o_ref[:, h * HEAD_WIDTH:(h + 1) * HEAD_WIDTH] = o.astype(BF16)


def _diffattn(qmt, k, vt, lq1, lk1, lq2, lk2, gain):
    bsz, seq, _ = k.shape
    tq = ATT_TQ
    assert tq & (tq - 1) == 0
    n_kv = seq // tq
    grid = (bsz, n_kv)
    c2 = lambda b, i: (0, 0)
    return pl.pallas_call(
        _diffattn_kernel,
        out_shape=jax.ShapeDtypeStruct((bsz, seq, D_ATTN), BF16),
        grid=grid,
        in_specs=[
            pl.BlockSpec((None, None, N_HEADS, HEAD_WIDTH, 2 * tq), lambda b, i: (b, i, 0, 0, 0)),
            pl.BlockSpec((None, seq, D_ATTN), lambda b, i: (b, 0, 0)),
            pl.BlockSpec((None, N_HEADS, n_kv, VT_ROWS, tq), lambda b, i: (b, 0, 0, 0, 0)),
            pl.BlockSpec(lq1.shape, c2),
            pl.BlockSpec(lk1.shape, c2),
            pl.BlockSpec(lq2.shape, c2),
            pl.BlockSpec(lk2.shape, c2),
            pl.BlockSpec(gain.shape, c2),
        ],
        out_specs=pl.BlockSpec((None, tq, D_ATTN), lambda b, i: (b, i, 0)),
        scratch_shapes=[
            pltpu.VMEM((N_HEADS, SUBLANES, 2 * tq), F32),
            pltpu.VMEM((N_HEADS, VT_ROWS, 2 * tq), F32),
            pltpu.VMEM((SCORE_SLOTS, tq, 2 * tq), F32),
        ],
        compiler_params=pltpu.CompilerParams(
            dimension_semantics=("parallel", "arbitrary"),
            vmem_limit_bytes=VMEM_LIMIT_BYTES),
        name="diffattn",
    )(qmt, k, vt, lq1, lk1, lq2, lk2, gain)


def _merge_kernel(x_ref, g_ref, ys_ref, ya_ref, wps_ref, wpa_ref, wo_ref, lng_ref, lnb_ref,
                  o_ref):
    n_sub = x_ref.shape[0] // MERGE_SUB_ROWS
    rows = [slice(r * MERGE_SUB_ROWS, (r + 1) * MERGE_SUB_ROWS) for r in range(n_sub)]

    def project(r):
        return _dot(ys_ref[rows[r], :], wps_ref[...]), _dot(ya_ref[rows[r], :], wpa_ref[...])

    def norm(r, mix):
        z = DEEPNORM_ALPHA * x_ref[rows[r], :] + mix
        o_ref[rows[r], :] = _layer_norm(z, lng_ref[...], lnb_ref[...])

    proj = project(0)
    pending = None
    for r in range(n_sub):
        p_ssm, p_att = proj
        if r + 1 < n_sub:
            proj = project(r + 1)
        merged = (g_ref[rows[r], 0:D_MODEL].astype(F32) * p_ssm
                  + g_ref[rows[r], D_MODEL:2 * D_MODEL].astype(F32) * p_att)
        mix = _dot(merged.astype(BF16), wo_ref[...])
        if pending is not None:
            norm(*pending)
        pending = (r, mix)
    norm(*pending)


def _merge(x2, gates2, ys2, ya2, wps, wpa, wo, ln_g, ln_b):
    n, _ = x2.shape
    tm = MERGE_TM
    row = lambda i: (i, 0)
    c2 = lambda i: (0, 0)
    return pl.pallas_call(
        _merge_kernel,
        out_shape=jax.ShapeDtypeStruct((n, D_MODEL), F32),
        grid=(n // tm,),
        in_specs=[
            pl.BlockSpec((tm, D_MODEL), row),
            pl.BlockSpec((tm, 2 * D_MODEL), row),
            pl.BlockSpec((tm, D_SSM), row),
            pl.BlockSpec((tm, D_ATTN), row),
            pl.BlockSpec(wps.shape, c2),
            pl.BlockSpec(wpa.shape, c2),
            pl.BlockSpec(wo.shape, c2),
            pl.BlockSpec(ln_g.shape, c2),
            pl.BlockSpec(ln_b.shape, c2),
        ],
        out_specs=pl.BlockSpec((tm, D_MODEL), row),
        compiler_params=pltpu.CompilerParams(
            dimension_semantics=("parallel",),
            vmem_limit_bytes=VMEM_LIMIT_BYTES),
        name="merge",
    )(x2, gates2, ys2, ya2, wps, wpa, wo, ln_g, ln_b)


def _ffn_kernel(x_ref, wg_ref, wu_ref, wd_ref, lng_ref, lnb_ref, o_ref):
    n_sub = x_ref.shape[0] // SUB_ROWS
    bounds = [sum(FFN_CHUNKS[:c]) for c in range(len(FFN_CHUNKS) + 1)]
    items = [(r, c) for r in range(n_sub) for c in range(len(FFN_CHUNKS))]
    rows = [slice(r * SUB_ROWS, (r + 1) * SUB_ROWS) for r in range(n_sub)]

    def gate_up(item):
        r, c = item
        xb = x_ref[rows[r], :].astype(BF16)
        cols = slice(bounds[c], bounds[c + 1])
        return _dot(xb, wg_ref[:, cols]), _dot(xb, wu_ref[:, cols])

    def norm(r, ff):
        z = DEEPNORM_ALPHA * x_ref[rows[r], :] + ff
        o_ref[rows[r], :] = _layer_norm(z, lng_ref[...], lnb_ref[...])

    ab = gate_up(items[0])
    ff = None
    pending = None
    for idx, (r, c) in enumerate(items):
        a, b = ab
        if idx + 1 < len(items):
            ab = gate_up(items[idx + 1])
        if pending is not None:
            norm(*pending)
            pending = None
        h = (jax.nn.silu(a) * b).astype(BF16)
        down = _dot(h, wd_ref[bounds[c]:bounds[c + 1], :])
        ff = down if c == 0 else ff + down
        if c == len(FFN_CHUNKS) - 1:
            pending = (r, ff)
    norm(*pending)


def _ffn(x2, wg, wu, wd, ln_g, ln_b):
    n, _ = x2.shape
    tm = FFN_TM
    row = lambda i: (i, 0)
    c2 = lambda i: (0, 0)
    resident = functools.partial(pl.BlockSpec, index_map=c2, pipeline_mode=pl.Buffered(1))
    return pl.pallas_call(
        _ffn_kernel,
        out_shape=jax.ShapeDtypeStruct((n, D_MODEL), F32),
        grid=(n // tm,),
        in_specs=[
            pl.BlockSpec((tm, D_MODEL), row),
            resident(wg.shape),
            resident(wu.shape),
            resident(wd.shape),
            pl.BlockSpec(ln_g.shape, c2),
            pl.BlockSpec(ln_b.shape, c2),
        ],
        out_specs=pl.BlockSpec((tm, D_MODEL), row),
        compiler_params=pltpu.CompilerParams(
            dimension_semantics=("parallel",),
            vmem_limit_bytes=VMEM_LIMIT_BYTES),
        name="ffn",
    )(x2, wg, wu, wd, ln_g, ln_b)


def _rope_tables(seq):
    half = HEAD_DIM // 2
    inv_freq = ROPE_THETA ** (-jnp.arange(0, HEAD_DIM, 2, dtype=F32) / HEAD_DIM)
    ang = jnp.arange(seq, dtype=F32)[:, None] * inv_freq[None, :]
    cos, sin = jnp.cos(ang), jnp.sin(ang)
    reps = LANES // half
    cos_t = jnp.tile(cos, (1, reps))
    sign = jnp.tile(jnp.concatenate([-jnp.ones((half,), F32), jnp.ones((half,), F32)]),
                    LANES // HEAD_DIM)
    sin_t = jnp.tile(sin, (1, reps)) * sign[None, :]
    return cos_t, sin_t


def _block_diag_in(b):
    gps = N_SSM_GROUPS // SSM_SLAB
    bt = jnp.swapaxes(b, 1, 2).reshape(SSM_SLAB, gps, SSM_GROUP, SSM_STATE)
    eye = jnp.eye(gps, dtype=b.dtype)
    blk = bt[:, :, :, None, :] * eye[None, :, None, :, None]
    return blk.reshape(SSM_SLAB, gps * SSM_GROUP, gps * SSM_STATE)


def _block_diag_out(c):
    gps = N_SSM_GROUPS // SSM_SLAB
    ct = jnp.swapaxes(c, 1, 2).reshape(SSM_SLAB, gps, SSM_STATE, SSM_GROUP)
    eye = jnp.eye(gps, dtype=c.dtype)
    blk = ct[:, :, :, None, :] * eye[None, :, None, :, None]
    return blk.reshape(SSM_SLAB, gps * SSM_STATE, gps * SSM_GROUP)


def kernel(x, w_in, b_gate, ssm_lambda_re, ssm_lambda_im, ssm_log_step, ssm_b_re, ssm_b_im, ssm_c_re, ssm_c_im, ssm_d, w_glu, lambda_q1, lambda_k1, lambda_q2, lambda_k2, subln_gain, w_proj_ssm, w_proj_attn, w_out, ln1_g, ln1_b, w_ffn_gate, w_ffn_up, w_ffn_down, ln2_g, ln2_b):
    bsz, seq, _ = x.shape
    cos_t, sin_t = _rope_tables(seq)
    for l in range(DEPTH):
        (u, qmt, k, vt, gates), (w_glu_b, wps_b, wpa_b, wo_b, wg_b, wu_b, wd_b) = _in_proj(
            x, w_in[l].astype(BF16), b_gate[l][None, :], cos_t, sin_t,
            (w_glu[l], w_proj_ssm[l], w_proj_attn[l], w_out[l],
             w_ffn_gate[l], w_ffn_up[l], w_ffn_down[l]))

        lstep = jnp.broadcast_to(ssm_log_step[l][:, None], (N_SSM_GROUPS, SSM_STATE))
        y_ssm = _s5(
            u,
            ssm_lambda_re[l].reshape(1, N_STATES),
            ssm_lambda_im[l].reshape(1, N_STATES),
            lstep.reshape(1, N_STATES),
            _block_diag_in(ssm_b_re[l]), _block_diag_in(ssm_b_im[l]),
            _block_diag_out(ssm_c_re[l]).astype(BF16), _block_diag_out(ssm_c_im[l]).astype(BF16),
            ssm_d[l].reshape(1, D_SSM),
            w_glu_b)

        y_att = _diffattn(qmt, k, vt, lambda_q1[l][None, :], lambda_k1[l][None, :],
                          lambda_q2[l][None, :], lambda_k2[l][None, :], subln_gain[l][None, :])

        n = bsz * seq
        x1 = _merge(x.reshape(n, D_MODEL), gates.reshape(n, 2 * D_MODEL),
                    y_ssm.reshape(n, D_SSM), y_att.reshape(n, D_ATTN),
                    wps_b, wpa_b, wo_b, ln1_g[l][None, :], ln1_b[l][None, :])
        x2 = _ffn(x1, wg_b, wu_b, wd_b, ln2_g[l][None, :], ln2_b[l][None, :])
        x = x2.reshape(bsz, seq, D_MODEL)
    return x
```
